```python
import math
import jax
import jax.numpy as jnp
from jax import lax
import numpy as np

D_MODEL = 1024
BATCH = 1
SEQ = 16384
DEPTH = 1
DEC_BATCH = 32
DEC_SEQ = 1
PAST_LEN = 16384
PAGE_SIZE = 128

W_ATT = D_MODEL // 2
HD_ATT = 64
H_ATT = W_ATT // HD_ATT
DILATIONS = ((128, 1), (512, 4), (2048, 16))
MAX_WINDOW = max(w for w, _ in DILATIONS)
Q_BLOCK = 128
N_BUCKETS = 32
MAX_EXACT = N_BUCKETS // 2
BUCKET_MAX_DIST = MAX_WINDOW
W_RET = D_MODEL - W_ATT
H_RET = 4
DK_RET = W_RET // H_RET
DV_RET = W_RET // H_RET
RET_CHUNK = 128
ROPE_BASE = 10000.0
W_MIX = W_ATT + W_RET
D_FF = 4 * D_MODEL
SPLITS = (W_ATT, W_ATT, W_ATT, H_RET * DK_RET, H_RET * DK_RET, H_RET * DV_RET, H_RET * DV_RET)
D_IN = sum(SPLITS)
ALPHA = (2.0 * DEPTH) ** 0.25
BETA = (8.0 * DEPTH) ** -0.25
LN_EPS = 1e-5
GN_EPS = 1e-6

kernel_name = 'hybrid_dilated_attn_retention_decoder_step'

F32 = jnp.float32


def layernorm(x, g, b):
    xf = x.astype(F32)
    mu = xf.mean(-1, keepdims=True)
    var = jnp.square(xf - mu).mean(-1, keepdims=True)
    return ((xf - mu) * lax.rsqrt(var + LN_EPS) * g.astype(F32) + b.astype(F32)).astype(x.dtype)


def t5_bucket(dist):
    is_small = dist < MAX_EXACT
    d_f = jnp.maximum(dist, 1).astype(F32)
    large = MAX_EXACT + (jnp.log(d_f / MAX_EXACT) / math.log(BUCKET_MAX_DIST / MAX_EXACT)
                         * (N_BUCKETS - MAX_EXACT)).astype(jnp.int32)
    large = jnp.minimum(large, N_BUCKETS - 1)
    return jnp.where(is_small, dist, large)


def softmax_stats(logits):
    m = logits.max(-1, keepdims=True)
    p = jnp.exp(logits - m)
    s = p.sum(-1, keepdims=True)
    return p / s, (m + jnp.log(s))[..., 0]


def in_projection(x, w_in):
    B, T, _ = x.shape
    u = jnp.einsum('btd,de->bte', x, w_in)
    offs = np.cumsum(SPLITS)[:-1].tolist()
    q_a, k_a, v_a, q_r, k_r, v_r, g_r = jnp.split(u, offs, axis=-1)
    return (q_a.reshape(B, T, H_ATT, HD_ATT), k_a.reshape(B, T, H_ATT, HD_ATT),
            v_a.reshape(B, T, H_ATT, HD_ATT), q_r.reshape(B, T, H_RET, DK_RET),
            k_r.reshape(B, T, H_RET, DK_RET), v_r.reshape(B, T, H_RET, DV_RET), g_r)


def dilated_branch_prompt(q, k, v, window, dil, rel_bias):
    B, S, H, Dh = q.shape
    sub_win = window // dil
    L = S // dil
    Lp = -(-L // Q_BLOCK) * Q_BLOCK
    nb = Lp // Q_BLOCK
    N = B * dil

    def to_sub(t):
        t = t.astype(F32).reshape(B, L, dil, H, Dh).transpose(0, 2, 1, 3, 4).reshape(N, L, H, Dh)
        return jnp.pad(t, ((0, 0), (0, Lp - L), (0, 0), (0, 0)))

    def band(t):
        prev = jnp.pad(t, ((0, 0), (Q_BLOCK, 0), (0, 0), (0, 0)))[:, :Lp]
        return jnp.concatenate([prev.reshape(N, nb, Q_BLOCK, H, Dh),
                                t.reshape(N, nb, Q_BLOCK, H, Dh)], axis=2)

    qs, ks, vs = to_sub(q), to_sub(k), to_sub(v)
    qb = qs.reshape(N, nb, Q_BLOCK, H, Dh)
    kb, vb = band(ks), band(vs)
    rel = jnp.arange(Q_BLOCK)[:, None] + Q_BLOCK - jnp.arange(2 * Q_BLOCK)[None, :]
    key_idx = (jnp.arange(nb)[:, None, None] * Q_BLOCK
               + jnp.arange(2 * Q_BLOCK)[None, None, :] - Q_BLOCK)
    valid = (rel >= 0) & (rel <= sub_win) & (key_idx >= 0)
    bias = rel_bias[t5_bucket(jnp.clip(rel, 0, sub_win) * dil)].astype(F32).transpose(2, 0, 1)
    logits = jnp.einsum('nbqhd,nbkhd->nbhqk', qb, kb) * HD_ATT ** -0.5 + bias[None, None]
    logits = jnp.where(valid[None, :, None], logits, -jnp.inf)
    p, lse = softmax_stats(logits)
    o = jnp.einsum('nbhqk,nbkhd->nbqhd', p, vb)
    o = o.reshape(N, Lp, H, Dh)[:, :L].reshape(B, dil, L, H, Dh).transpose(0, 2, 1, 3, 4)
    lse = lse.transpose(0, 1, 3, 2).reshape(N, Lp, H)[:, :L].reshape(B, dil, L, H).transpose(0, 2, 1, 3)
    return o.reshape(B, S, H, Dh), lse.reshape(B, S, H)


def dilated_branch_sample(q, k_all, v_all, window, dil, rel_bias):
    B, T, H, Dh = q.shape
    W = k_all.shape[1] - T
    taps = jnp.arange(window // dil + 1)
    idx = W + jnp.arange(T)[:, None] - taps[None, :] * dil
    valid = idx >= 0
    idx = jnp.maximum(idx, 0)
    kg = k_all.astype(F32)[:, idx]
    vg = v_all.astype(F32)[:, idx]
    bias = rel_bias[t5_bucket(taps * dil)].astype(F32).T
    logits = jnp.einsum('bthd,btjhd->bthj', q.astype(F32), kg) * HD_ATT ** -0.5 + bias[None, None]
    logits = jnp.where(valid[None, :, None, :], logits, -jnp.inf)
    p, lse = softmax_stats(logits)
    o = jnp.einsum('bthj,btjhd->bthd', p, vg)
    return o, lse


def mix_dilations(branches):
    lse = jnp.stack([b[1] for b in branches], axis=-1)
    w = jax.nn.softmax(lse, axis=-1)
    return sum(w[..., i, None] * o for i, (o, _) in enumerate(branches))


def rotary(x, pos):
    half = x.shape[-1] // 2
    inv_freq = 1.0 / (ROPE_BASE ** jnp.linspace(0.0, 1.0, half, dtype=F32))
    ang = pos.astype(F32)[:, None] * inv_freq[None, :]
    cos, sin = jnp.cos(ang)[None, :, None, :], jnp.sin(ang)[None, :, None, :]
    x1, x2 = x[..., :half].astype(F32), x[..., half:].astype(F32)
    return jnp.concatenate([x1 * cos - x2 * sin, x1 * sin + x2 * cos], axis=-1)


def log_gamma():
    return jnp.log(1.0 - jnp.exp2(-5.0 - jnp.arange(H_RET, dtype=F32)))


def retention_chunk(q, k, v, state):
    C = q.shape[1]
    lg = log_gamma()
    n = jnp.arange(C, dtype=F32)
    diff = n[:, None] - n[None, :]
    decay = jnp.where(diff >= 0, jnp.exp(lg[:, None, None] * jnp.maximum(diff, 0.0)), 0.0)
    scores = jnp.einsum('bqhd,bkhd->bhqk', q, k) * decay
    o_inner = jnp.einsum('bhqk,bkhv->bqhv', scores, v)
    o_cross = jnp.einsum('bqhd,bhdv->bqhv', q, state) * jnp.exp(lg[None, :] * (n[:, None] + 1.0))[None, :, :, None]
    k_dec = k * jnp.exp(lg[None, :] * (C - 1.0 - n[:, None]))[None, :, :, None]
    new_state = jnp.exp(lg * C)[None, :, None, None] * state + jnp.einsum('bkhd,bkhv->bhdv', k_dec, v)
    return o_inner + o_cross, new_state


def retention_prompt(q, k, v):
    B, S, H, Dk = q.shape
    nc = S // RET_CHUNK

    def chunks(t):
        return t.reshape(B, nc, RET_CHUNK, H, t.shape[-1]).transpose(1, 0, 2, 3, 4)

    def step(state, xs):
        qc, kc, vc = xs
        o, state = retention_chunk(qc, kc, vc, state)
        return state, o

    s0 = jnp.zeros((B, H, DK_RET, DV_RET), F32)
    s_fin, o = lax.scan(step, s0, (chunks(q), chunks(k), chunks(v)))
    return o.transpose(1, 0, 2, 3, 4).reshape(B, S, H, DV_RET), s_fin


def retention_output(o, g):
    B, T = o.shape[:2]
    mu = o.mean(-1, keepdims=True)
    var = jnp.square(o - mu).mean(-1, keepdims=True)
    y = ((o - mu) * lax.rsqrt(var + GN_EPS)).reshape(B, T, W_RET)
    return jax.nn.silu(g.astype(F32)) * y


def layer_output(x, o_att, y_ret, w_out, ln1_g, ln1_b, w_up, w_down, ln2_g, ln2_b):
    B, T, _ = x.shape
    heads = jnp.concatenate([o_att.reshape(B, T, W_ATT), y_ret], axis=-1).astype(x.dtype)
    mix = jnp.einsum('bte,ed->btd', heads, w_out)
    x1 = layernorm(ALPHA * x + mix, ln1_g, ln1_b)
    h = jnp.square(jax.nn.relu(jnp.einsum('btd,df->btf', x1, w_up)))
    ffn = jnp.einsum('btf,fd->btd', h, w_down)
    return layernorm(ALPHA * x1 + ffn, ln2_g, ln2_b)


def setup_inputs(seed: int = 0) -> dict:
    key = jax.random.key(seed)
    ks = jax.random.split(key, 16)
    nrm = jax.random.normal
    win_buf = min(MAX_WINDOW, PAST_LEN)
    col_scale = np.ones((D_IN,), np.float32)
    offs = np.cumsum((0,) + SPLITS)
    col_scale[offs[2]:offs[3]] = BETA
    col_scale[offs[5]:offs[6]] = BETA
    return {
        'x_prompt': nrm(ks[0], (BATCH, SEQ, D_MODEL), F32),
        'x_sample': nrm(ks[1], (DEC_BATCH, DEC_SEQ, D_MODEL), F32),
        'cache_kv_win': nrm(ks[2], (DEPTH, DEC_BATCH, win_buf, 2, H_ATT, HD_ATT), F32),
        'state_ret': 0.5 * nrm(ks[3], (DEPTH, DEC_BATCH, H_RET, DK_RET, DV_RET), F32),
        'w_in': nrm(ks[4], (DEPTH, D_MODEL, D_IN), F32) * D_MODEL ** -0.5 * jnp.asarray(col_scale),
        'rel_bias': 0.5 * nrm(ks[5], (N_BUCKETS, H_ATT), F32),
        'w_out': nrm(ks[6], (DEPTH, W_MIX, D_MODEL), F32) * W_MIX ** -0.5 * BETA,
        'ln1_g': 1.0 + 0.05 * nrm(ks[7], (DEPTH, D_MODEL), F32),
        'ln1_b': 0.02 * nrm(ks[8], (DEPTH, D_MODEL), F32),
        'w_up': nrm(ks[9], (DEPTH, D_MODEL, D_FF), F32) * D_MODEL ** -0.5,
        'w_down': nrm(ks[10], (DEPTH, D_FF, D_MODEL), F32) * D_FF ** -0.5 * BETA,
        'ln2_g': 1.0 + 0.05 * nrm(ks[11], (DEPTH, D_MODEL), F32),
        'ln2_b': 0.02 * nrm(ks[12], (DEPTH, D_MODEL), F32),
    }


def reference(x_prompt, x_sample, cache_kv_win, state_ret, w_in, rel_bias, w_out,
              ln1_g, ln1_b, w_up, w_down, ln2_g, ln2_b):
    pos_p = jnp.arange(x_prompt.shape[1], dtype=jnp.int32)
    pos_s = PAST_LEN + jnp.arange(x_sample.shape[1], dtype=jnp.int32)
    hp, hs = x_prompt, x_sample
    kv_p_list, kv_s_list, st_p_list, st_s_list = [], [], [], []
    for l in range(DEPTH):
        q_a, k_a, v_a, q_r, k_r, v_r, g_r = in_projection(hp, w_in[l])
        o_att = mix_dilations([dilated_branch_prompt(q_a, k_a, v_a, w, d, rel_bias) for w, d in DILATIONS])
        o_ret, st_p = retention_prompt(rotary(q_r, pos_p), rotary(k_r, pos_p) * DK_RET ** -0.5,
                                       v_r.astype(F32))
        y_ret = retention_output(o_ret, g_r)
        win_p = min(MAX_WINDOW, hp.shape[1])
        kv_p_list.append(jnp.stack([k_a, v_a], axis=2)[:, -win_p:])
        st_p_list.append(st_p)
        hp_next = layer_output(hp, o_att, y_ret, w_out[l], ln1_g[l], ln1_b[l], w_up[l], w_down[l],
                               ln2_g[l], ln2_b[l])

        q_a, k_a, v_a, q_r, k_r, v_r, g_r = in_projection(hs, w_in[l])
        kv_buf = cache_kv_win[l]
        k_all = jnp.concatenate([kv_buf[:, :, 0].astype(k_a.dtype), k_a], axis=1)
        v_all = jnp.concatenate([kv_buf[:, :, 1].astype(v_a.dtype), v_a], axis=1)
        o_att = mix_dilations([dilated_branch_sample(q_a, k_all, v_all, w, d, rel_bias) for w, d in DILATIONS])
        o_ret, st_s = retention_chunk(rotary(q_r, pos_s), rotary(k_r, pos_s) * DK_RET ** -0.5,
                                      v_r.astype(F32), state_ret[l].astype(F32))
        y_ret = retention_output(o_ret, g_r)
        kv_s_list.append(jnp.stack([k_a, v_a], axis=2))
        st_s_list.append(st_s)
        hs_next = layer_output(hs, o_att, y_ret, w_out[l], ln1_g[l], ln1_b[l], w_up[l], w_down[l],
                               ln2_g[l], ln2_b[l])
        hp, hs = hp_next, hs_next
    kv_win_prompt = jnp.stack(kv_p_list)
    kv_win_sample = jnp.stack(kv_s_list)
    state_ret_prompt = jnp.stack(st_p_list)
    state_ret_sample = jnp.stack(st_s_list)
    return (hp, hs, kv_win_prompt, kv_win_sample, state_ret_prompt, state_ret_sample)
```

```python
import functools
import math

import jax
import jax.numpy as jnp
from jax import lax
from jax.experimental import pallas as pl
from jax.experimental.pallas import tpu as pltpu

F32 = jnp.float32
BF16 = jnp.bfloat16

D_MODEL = 1024
DEPTH = 1
PAST_LEN = 16384
W_ATT = 512
HD_ATT = 64
H_ATT = 8
DILATIONS = ((128, 1), (512, 4), (2048, 16))
TAPS = 128
MAX_WINDOW = 2048
N_BUCKETS = 32
MAX_EXACT = N_BUCKETS // 2
W_RET = 512
H_RET = 4
DK_RET = 128
DV_RET = 128
RET_CHUNK = 128
ROPE_BASE = 10000.0
D_FF = 4096
N_SPLITS = 7
ALPHA = (2.0 * DEPTH) ** 0.25
LN_EPS = 1e-5
GN_EPS = 1e-6
LOG_GAMMA = tuple(math.log(1.0 - 2.0 ** (-5.0 - h)) for h in range(H_RET))

LANES = 128
PAIRS = W_ATT // LANES
NEG = -1e30

ATT_BLOCK = 2048
Q_ROWS = 128
VMEM_LIMIT = 56 * 1024 * 1024


def _cparams(n_axes):
    return pltpu.CompilerParams(dimension_semantics=("arbitrary",) * n_axes, vmem_limit_bytes=VMEM_LIMIT)


def _rotary(u, cos, sin_signed):
    outs = []
    for h in range(H_RET):
        xh = u[:, h * LANES:(h + 1) * LANES]
        outs.append(xh * cos + pltpu.roll(xh, LANES // 2, 1) * sin_signed)
    return jnp.concatenate(outs, axis=1)


def _proj(xb, w_ref, j):
    return jnp.dot(xb, w_ref[:, j * W_ATT:(j + 1) * W_ATT], preferred_element_type=F32)


def _store_pairs(ref, val):
    for p in range(PAIRS):
        ref[p] = val[:, p * LANES:(p + 1) * LANES]


def _retention_inputs(xb, w_ref, cos_ref, sin_ref, qr_ref, kr_ref, vr_ref, g_ref):
    cos = cos_ref[...]
    sin = sin_ref[...]
    qr_ref[...] = _rotary(_proj(xb, w_ref, 3), cos, sin).astype(BF16)
    kr_ref[...] = (_rotary(_proj(xb, w_ref, 4), cos, sin) * DK_RET ** -0.5).astype(BF16)
    vr_ref[...] = _proj(xb, w_ref, 5).astype(BF16)
    g_ref[...] = _proj(xb, w_ref, 6)


def _inproj_prompt_body(x_ref, w_ref, cos_ref, sin_ref, qa_ref, ka_ref, va_ref, kvw_ref,
                        qr_ref, kr_ref, vr_ref, g_ref, *, win_first_step):
    xb = x_ref[...].astype(BF16)
    _store_pairs(qa_ref, (_proj(xb, w_ref, 0) * HD_ATT ** -0.5).astype(BF16))
    k = _proj(xb, w_ref, 1)
    v = _proj(xb, w_ref, 2)
    _store_pairs(ka_ref, k.astype(BF16))
    _store_pairs(va_ref, v.astype(BF16))

    @pl.when(pl.program_id(0) >= win_first_step)
    def _():
        kvw_ref[:W_ATT, :] = k.T
        kvw_ref[W_ATT:, :] = v.T

    _retention_inputs(xb, w_ref, cos_ref, sin_ref, qr_ref, kr_ref, vr_ref, g_ref)


def _inproj_sample_body(x_ref, w_ref, cos_ref, sin_ref, qa_ref, kvw_ref, qr_ref, kr_ref, vr_ref, g_ref):
    xb = x_ref[...].astype(BF16)
    qa_ref[...] = _proj(xb, w_ref, 0) * HD_ATT ** -0.5
    kvw_ref[:, :W_ATT] = _proj(xb, w_ref, 1)
    kvw_ref[:, W_ATT:] = _proj(xb, w_ref, 2)
    _retention_inputs(xb, w_ref, cos_ref, sin_ref, qr_ref, kr_ref, vr_ref, g_ref)


def _rope_tables(pos):
    half = DK_RET // 2
    inv_freq = 1.0 / (ROPE_BASE ** jnp.linspace(0.0, 1.0, half, dtype=F32))
    ang = pos.astype(F32)[:, None] * inv_freq[None, :]
    cos, sin = jnp.cos(ang), jnp.sin(ang)
    return jnp.concatenate([cos, cos], axis=1), jnp.concatenate([-sin, sin], axis=1)


def _inproj_prompt(x, w_in_b, tm):
    S = x.shape[0]
    win = min(MAX_WINDOW, S)
    steps = S // tm
    win_first_step = steps - win // tm
    cos, sin = _rope_tables(jnp.arange(S, dtype=jnp.int32))
    row = lambda i: (i, 0)
    pair_spec = pl.BlockSpec((PAIRS, tm, LANES), lambda i: (0, i, 0))
    pair_shape = jax.ShapeDtypeStruct((PAIRS, S, LANES), BF16)
    half_spec = pl.BlockSpec((tm, W_RET), row)
    return pl.pallas_call(
        functools.partial(_inproj_prompt_body, win_first_step=win_first_step),
        grid=(steps,),
        in_specs=[pl.BlockSpec((tm, D_MODEL), row),
                  pl.BlockSpec((D_MODEL, N_SPLITS * W_ATT), lambda i: (0, 0)),
                  pl.BlockSpec((tm, LANES), row),
                  pl.BlockSpec((tm, LANES), row)],
        out_specs=[pair_spec, pair_spec, pair_spec,
                   pl.BlockSpec((2 * W_ATT, tm), lambda i: (0, jnp.maximum(i - win_first_step, 0))),
                   half_spec, half_spec, half_spec, half_spec],
        out_shape=[pair_shape, pair_shape, pair_shape,
                   jax.ShapeDtypeStruct((2 * W_ATT, win), F32),
                   jax.ShapeDtypeStruct((S, W_RET), BF16),
                   jax.ShapeDtypeStruct((S, W_RET), BF16),
                   jax.ShapeDtypeStruct((S, W_RET), BF16),
                   jax.ShapeDtypeStruct((S, W_RET), F32)],
        compiler_params=_cparams(1),
        name="inproj_prompt",
    )(x, w_in_b, cos, sin)


def _inproj_sample(x, w_in_b):
    B = x.shape[0]
    cos, sin = _rope_tables(jnp.full((B,), PAST_LEN, dtype=jnp.int32))
    full = lambda shape: pl.BlockSpec(shape, lambda i: (0, 0))
    return pl.pallas_call(
        _inproj_sample_body,
        grid=(1,),
        in_specs=[full((B, D_MODEL)), full((D_MODEL, N_SPLITS * W_ATT)), full((B, LANES)), full((B, LANES))],
        out_specs=[full((B, W_ATT)), full((B, 2 * W_ATT)), full((B, W_RET)), full((B, W_RET)),
                   full((B, W_RET)), full((B, W_RET))],
        out_shape=[jax.ShapeDtypeStruct((B, W_ATT), F32),
                   jax.ShapeDtypeStruct((B, 2 * W_ATT), F32),
                   jax.ShapeDtypeStruct((B, W_RET), BF16),
                   jax.ShapeDtypeStruct((B, W_RET), BF16),
                   jax.ShapeDtypeStruct((B, W_RET), BF16),
                   jax.ShapeDtypeStruct((B, W_RET), F32)],
        compiler_params=_cparams(1),
        name="inproj_sample",
    )(x, w_in_b, cos, sin)


def _t5_bucket(dist):
    is_small = dist < MAX_EXACT
    d_f = jnp.maximum(dist, 1).astype(F32)
    large = MAX_EXACT + (jnp.log(d_f / MAX_EXACT) / math.log(MAX_WINDOW / MAX_EXACT)
                         * (N_BUCKETS - MAX_EXACT)).astype(jnp.int32)
    large = jnp.minimum(large, N_BUCKETS - 1)
    return jnp.where(is_small, dist, large)


def _tap_bias(rel_bias):
    taps = jnp.arange(TAPS + 1)
    return jnp.stack([rel_bias[_t5_bucket(taps * d)].astype(F32).T for _, d in DILATIONS])


def _band_bias(tap_bias):
    n = Q_ROWS + 2 * Q_ROWS
    w = jnp.concatenate([tap_bias[..., ::-1], jnp.full(tap_bias.shape[:-1] + (n - TAPS - 1,), NEG, F32)], axis=-1)
    skew = jnp.tile(w, Q_ROWS)[..., :Q_ROWS * (n - 1)].reshape(tap_bias.shape[:-1] + (Q_ROWS, n - 1))
    return skew[..., :2 * Q_ROWS]


def _window_bias(tap_bias, window):
    rows = []
    for i, (_, d) in enumerate(DILATIONS):
        taps_rev = tap_bias[i, :, TAPS:0:-1]
        seg = jnp.concatenate([taps_rev[..., None], jnp.full((H_ATT, TAPS, d - 1), NEG, F32)], axis=-1)
        rows.append(jnp.concatenate([jnp.full((H_ATT, window - TAPS * d), NEG, F32),
                                     seg.reshape(H_ATT, TAPS * d)], axis=-1))
    return jnp.stack(rows)


def _attn_unit(q, k, v, bias2, head0_b, head0):
    qs = jnp.concatenate([q * head0_b, q * (1 - head0_b)], axis=0)
    s = lax.dot_general(qs, k, (((1,), (1,)), ((), ())), preferred_element_type=F32) + bias2
    m = jnp.max(s, axis=1, keepdims=True)
    p = jnp.exp(s - m)
    l = jnp.sum(p, axis=1, keepdims=True)
    o = jnp.dot(p.astype(BF16), v, preferred_element_type=F32) / l
    lse = m + jnp.log(l)
    o_pair = jnp.where(head0, o[:Q_ROWS], o[Q_ROWS:])
    lse_pair = jnp.where(head0, jnp.broadcast_to(lse[:Q_ROWS], (Q_ROWS, LANES)),
                         jnp.broadcast_to(lse[Q_ROWS:], (Q_ROWS, LANES)))
    return o_pair, lse_pair


def _attn_body(q1, q4, q16, k1c, k1p, k4c, k4p, k16c, k16p, v1c, v1p, v4c, v4p, v16c, v16p, bias_ref,
               out_ref, o16, l16, o4, l4):
    first = pl.program_id(0) == 0
    lane = lax.broadcasted_iota(jnp.int32, (Q_ROWS, LANES), 1)
    head0 = lane < HD_ATT
    head0_b = jnp.where(head0, 1.0, 0.0).astype(BF16)
    col = lax.broadcasted_iota(jnp.int32, (2 * Q_ROWS, 2 * Q_ROWS), 1)
    no_prev = jnp.where(jnp.logical_and(first, col < Q_ROWS), NEG, 0.0)

    def bias2(branch, masked_prev):
        b = jnp.concatenate([bias_ref[branch, 0], bias_ref[branch, 1]], axis=0)
        return b + no_prev if masked_prev else b

    def cat(a, b):
        return jnp.concatenate([a, b], axis=0)

    b16 = bias2(2, True)
    for r in range(16):
        sl = slice(r * LANES, (r + 1) * LANES)
        o, l = _attn_unit(q16[:, sl], cat(k16p[:, sl], k16c[:, sl]), cat(v16p[:, sl], v16c[:, sl]),
                          b16, head0_b, head0)
        o16[pl.ds(r, Q_ROWS, stride=16), :] = o
        l16[pl.ds(r, Q_ROWS, stride=16), :] = l

    b4_first = bias2(1, True)
    b4 = bias2(1, False)
    for r in range(4):
        sl = slice(r * LANES, (r + 1) * LANES)
        for b in range(4):
            rows = slice(b * Q_ROWS, (b + 1) * Q_ROWS)
            if b == 0:
                k = cat(k4p[:, sl], k4c[rows, sl])
                v = cat(v4p[:, sl], v4c[rows, sl])
            else:
                both = slice((b - 1) * Q_ROWS, (b + 1) * Q_ROWS)
                k, v = k4c[both, sl], v4c[both, sl]
            o, l = _attn_unit(q4[rows, sl], k, v, b4_first if b == 0 else b4, head0_b, head0)
            o4[pl.ds(b * 4 * Q_ROWS + r, Q_ROWS, stride=4), :] = o
            l4[pl.ds(b * 4 * Q_ROWS + r, Q_ROWS, stride=4), :] = l

    b1_first = bias2(0, True)
    b1 = bias2(0, False)
    for b in range(ATT_BLOCK // Q_ROWS):
        rows = slice(b * Q_ROWS, (b + 1) * Q_ROWS)
        if b == 0:
            k = cat(k1p[...], k1c[rows, :])
            v = cat(v1p[...], v1c[rows, :])
        else:
            both = slice((b - 1) * Q_ROWS, (b + 1) * Q_ROWS)
            k, v = k1c[both, :], v1c[both, :]
        oa, la = _attn_unit(q1[rows, :], k, v, b1_first if b == 0 else b1, head0_b, head0)
        ob, lb = o4[rows, :], l4[rows, :]
        oc, lc = o16[rows, :], l16[rows, :]
        top = jnp.maximum(jnp.maximum(la, lb), lc)
        ea, eb, ec = jnp.exp(la - top), jnp.exp(lb - top), jnp.exp(lc - top)
        out_ref[rows, :] = ((ea * oa + eb * ob + ec * oc) / (ea + eb + ec)).astype(BF16)


def _attn_prompt(qa, ka, va, band_bias):
    S = qa.shape[1]
    assert S % ATT_BLOCK == 0
    steps = S // ATT_BLOCK

    def views(a):
        return a, a.reshape(PAIRS, S // 4, 4 * LANES), a.reshape(PAIRS, S // 16, 16 * LANES)

    def cur(d):
        return pl.BlockSpec((None, ATT_BLOCK // d, d * LANES), lambda c, p: (p, c, 0))

    def prev(d):
        per_step = ATT_BLOCK // d // Q_ROWS
        return pl.BlockSpec((None, Q_ROWS, d * LANES), lambda c, p: (p, jnp.maximum(c * per_step - 1, 0), 0))

    q1, q4, q16 = views(qa)
    k1, k4, k16 = views(ka)
    v1, v4, v16 = views(va)
    scratch = pltpu.VMEM((ATT_BLOCK, LANES), F32)
    return pl.pallas_call(
        _attn_body,
        grid=(steps, PAIRS),
        in_specs=[cur(1), cur(4), cur(16),
                  cur(1), prev(1), cur(4), prev(4), cur(16), prev(16),
                  cur(1), prev(1), cur(4), prev(4), cur(16), prev(16),
                  pl.BlockSpec((len(DILATIONS), 2, Q_ROWS, 2 * Q_ROWS), lambda c, p: (0, p, 0, 0))],
        out_specs=pl.BlockSpec((ATT_BLOCK, LANES), lambda c, p: (c, p)),
        out_shape=jax.ShapeDtypeStruct((S, W_ATT), BF16),
        scratch_shapes=[scratch, scratch, scratch, scratch],
        compiler_params=_cparams(2),
        name="attn_prompt",
    )(q1, q4, q16, k1, k1, k4, k4, k16, k16, v1, v1, v4, v4, v16, v16, band_bias)


def _group_norm_gate(o, g):
    mu = jnp.mean(o, axis=1, keepdims=True)
    var = jnp.mean(jnp.square(o - mu), axis=1, keepdims=True)
    return g * jax.nn.sigmoid(g) * ((o - mu) * lax.rsqrt(var + GN_EPS))


def _ret_prompt_body(q_ref, k_ref, v_ref, g_ref, y_ref, st_ref, state, *, chunks):
    C = RET_CHUNK

    @pl.when(pl.program_id(0) == 0)
    def _():
        state[...] = jnp.zeros_like(state)

    n = lax.broadcasted_iota(jnp.int32, (C, 1), 0).astype(F32)
    diff = (lax.broadcasted_iota(jnp.int32, (C, C), 0) - lax.broadcasted_iota(jnp.int32, (C, C), 1)).astype(F32)
    for h in range(H_RET):
        lg = LOG_GAMMA[h]
        decay = jnp.where(diff >= 0, jnp.exp(lg * jnp.maximum(diff, 0.0)), 0.0)
        q_decay = jnp.exp(lg * (n + 1.0))
        k_decay = jnp.exp(lg * (C - 1.0 - n))
        chunk_decay = math.exp(lg * C)
        cols = slice(h * DK_RET, (h + 1) * DK_RET)
        S = state[h]
        for c in range(chunks):
            rows = slice(c * C, (c + 1) * C)
            q, k, v = q_ref[rows, cols], k_ref[rows, cols], v_ref[rows, cols]
            scores = lax.dot_general(q, k, (((1,), (1,)), ((), ())), preferred_element_type=F32) * decay
            o = (jnp.dot(scores.astype(BF16), v, preferred_element_type=F32)
                 + jnp.dot(q, S.astype(BF16), preferred_element_type=F32) * q_decay)
            k_dec_t = (k.astype(F32) * k_decay).T.astype(BF16)
            S = chunk_decay * S + jnp.dot(k_dec_t, v, preferred_element_type=F32)
            y_ref[rows, cols] = _group_norm_gate(o, g_ref[rows, cols]).astype(BF16)
        state[h] = S
    st_ref[...] = state[...]


def _ret_prompt(qr, kr, vr, g, chunks):
    S = qr.shape[0]
    rows = chunks * RET_CHUNK
    spec = pl.BlockSpec((rows, W_RET), lambda i: (i, 0))
    return pl.pallas_call(
        functools.partial(_ret_prompt_body, chunks=chunks),
        grid=(S // rows,),
        in_specs=[spec, spec, spec, spec],
        out_specs=[spec, pl.BlockSpec((H_RET, DK_RET, DV_RET), lambda i: (0, 0, 0))],
        out_shape=[jax.ShapeDtypeStruct((S, W_RET), BF16),
                   jax.ShapeDtypeStruct((H_RET, DK_RET, DV_RET), F32)],
        scratch_shapes=[pltpu.VMEM((H_RET, DK_RET, DV_RET), F32)],
        compiler_params=_cparams(1),
        name="retention_prompt",
    )(qr, kr, vr, g)


def _as_column(row_pair):
    return jnp.broadcast_to(row_pair, (LANES, LANES)).T


def _sample_mixer_body(qa_ref, kvw_ref, qr_ref, kr_ref, vr_ref, g_ref, cache_ref, st_ref, bias_ref, bias0_ref,
                       oatt_ref, yret_ref, stout_ref):
    W = cache_ref.shape[-1]
    for p in range(PAIRS):
        lanes = slice(p * LANES, (p + 1) * LANES)
        q_pair = qa_ref[:, lanes]
        k_pair = kvw_ref[:, lanes]
        q_col = _as_column(q_pair)
        v_col = _as_column(kvw_ref[:, W_ATT + p * LANES:W_ATT + (p + 1) * LANES])
        out_cols = []
        for hh in range(2):
            h = 2 * p + hh
            feat = slice(hh * HD_ATT, (hh + 1) * HD_ATT)
            qc = q_col[feat, 0:1]
            vc = v_col[feat, 0:1]
            logits = jnp.sum(cache_ref[0, h] * qc, axis=0, keepdims=True)
            s_self = jnp.sum(q_pair[:, feat] * k_pair[:, feat], axis=1, keepdims=True) + bias0_ref[h:h + 1, 0:1]
            outs, lses = [], []
            for i, (_, d) in enumerate(DILATIONS):
                lo = W - TAPS * d
                s = logits[:, lo:] + bias_ref[i, h:h + 1, lo:]
                m = jnp.maximum(jnp.max(s, axis=1, keepdims=True), s_self)
                pr = jnp.exp(s - m)
                p_self = jnp.exp(s_self - m)
                l = jnp.sum(pr, axis=1, keepdims=True) + p_self
                pv = jnp.sum(cache_ref[1, h, :, lo:] * pr, axis=1, keepdims=True)
                outs.append((pv + p_self * vc) / l)
                lses.append(m + jnp.log(l))
            top = jnp.maximum(jnp.maximum(lses[0], lses[1]), lses[2])
            e = [jnp.exp(x - top) for x in lses]
            out_cols.append((e[0] * outs[0] + e[1] * outs[1] + e[2] * outs[2]) / (e[0] + e[1] + e[2]))
        col = jnp.broadcast_to(jnp.concatenate(out_cols, axis=0), (LANES, LANES))
        oatt_ref[:, lanes] = col.T[0:1, :].astype(BF16)

    for h in range(H_RET):
        cols = slice(h * DK_RET, (h + 1) * DK_RET)
        gamma = math.exp(LOG_GAMMA[h])
        qh = qr_ref[:, cols]
        kh = kr_ref[:, cols].astype(F32)
        vh = vr_ref[:, cols].astype(F32)
        S = st_ref[h]
        qk = jnp.sum(qh.astype(F32) * kh, axis=1, keepdims=True)
        cross = jnp.dot(jnp.broadcast_to(qh, (16, DK_RET)), S.astype(BF16), preferred_element_type=F32)[:1]
        o = qk * vh + cross * gamma
        stout_ref[h] = gamma * S + _as_column(kh) * vh
        yret_ref[:, cols] = _group_norm_gate(o, g_ref[:, cols]).astype(BF16)


def _sample_mixer(qa, kvw, qr, kr, vr, g, cache_t, state, tap_bias):
    B, _, _, _, W = cache_t.shape
    assert W == MAX_WINDOW and PAST_LEN >= MAX_WINDOW
    row3 = lambda a: a.reshape(B, 1, a.shape[-1])
    vec = lambda width: pl.BlockSpec((None, 1, width), lambda b: (b, 0, 0))
    bias0 = jnp.broadcast_to(tap_bias[0, :, 0:1], (H_ATT, LANES))
    st_spec = pl.BlockSpec((None, H_RET, DK_RET, DV_RET), lambda b: (b, 0, 0, 0))
    oatt, yret, st = pl.pallas_call(
        _sample_mixer_body,
        grid=(B,),
        in_specs=[vec(W_ATT), vec(2 * W_ATT), vec(W_RET), vec(W_RET), vec(W_RET), vec(W_RET),
                  pl.BlockSpec((None, 2, H_ATT, HD_ATT, W), lambda b: (b, 0, 0, 0, 0)),
                  st_spec,
                  pl.BlockSpec((len(DILATIONS), H_ATT, W), lambda b: (0, 0, 0)),
                  pl.BlockSpec((H_ATT, LANES), lambda b: (0, 0))],
        out_specs=[vec(W_ATT), vec(W_RET), st_spec],
        out_shape=[jax.ShapeDtypeStruct((B, 1, W_ATT), BF16),
                   jax.ShapeDtypeStruct((B, 1, W_RET), BF16),
                   jax.ShapeDtypeStruct(state.shape, F32)],
        compiler_params=_cparams(1),
        name="sample_mixer",
    )(row3(qa), row3(kvw), row3(qr), row3(kr), row3(vr), row3(g), cache_t, state,
      _window_bias(tap_bias, W), bias0)
    return oatt.reshape(B, W_ATT), yret.reshape(B, W_RET), st


FF_CHUNK = 1024


def _layernorm(z, g, b):
    mu = jnp.mean(z, axis=1, keepdims=True)
    var = jnp.mean(jnp.square(z - mu), axis=1, keepdims=True)
    return (z - mu) * lax.rsqrt(var + LN_EPS) * g + b


def _out_ffn_body(x_ref, oatt_ref, yret_ref, wo_ref, g1_ref, b1_ref, wu_ref, wd_ref, g2_ref, b2_ref, y_ref):
    mix = (jnp.dot(oatt_ref[...], wo_ref[:W_ATT, :], preferred_element_type=F32)
           + jnp.dot(yret_ref[...], wo_ref[W_ATT:, :], preferred_element_type=F32))
    x1 = _layernorm(ALPHA * x_ref[...] + mix, g1_ref[...], b1_ref[...])
    x1b = x1.astype(BF16)
    ffn = jnp.zeros_like(x1)
    for j in range(D_FF // FF_CHUNK):
        cols = slice(j * FF_CHUNK, (j + 1) * FF_CHUNK)
        h = jnp.square(jnp.maximum(jnp.dot(x1b, wu_ref[:, cols], preferred_element_type=F32), 0.0))
        ffn = ffn + jnp.dot(h.astype(BF16), wd_ref[cols, :], preferred_element_type=F32)
    y_ref[...] = _layernorm(ALPHA * x1 + ffn, g2_ref[...], b2_ref[...])


def _out_ffn(x, oatt, yret, wo_b, g1, b1, wu_b, wd_b, g2, b2, tm, name):
    M = x.shape[0]
    row = lambda i: (i, 0)
    const = lambda shape: pl.BlockSpec(shape, lambda i: (0, 0))
    return pl.pallas_call(
        _out_ffn_body,
        grid=(M // tm,),
        in_specs=[pl.BlockSpec((tm, D_MODEL), row), pl.BlockSpec((tm, W_ATT), row), pl.BlockSpec((tm, W_RET), row),
                  const((W_ATT + W_RET, D_MODEL)), const((1, D_MODEL)), const((1, D_MODEL)),
                  const((D_MODEL, D_FF)), const((D_FF, D_MODEL)), const((1, D_MODEL)), const((1, D_MODEL))],
        out_specs=pl.BlockSpec((tm, D_MODEL), row),
        out_shape=jax.ShapeDtypeStruct((M, D_MODEL), F32),
        compiler_params=_cparams(1),
        name=name,
    )(x, oatt, yret, wo_b, g1, b1, wu_b, wd_b, g2, b2)


PROMPT_ROWS = 512
RET_CHUNKS_PER_STEP = 4


def kernel(x_prompt, x_sample, cache_kv_win, state_ret, w_in, rel_bias, w_out,
           ln1_g, ln1_b, w_up, w_down, ln2_g, ln2_b):
    assert x_prompt.shape[0] == 1 and x_sample.shape[1] == 1 and w_in.shape[0] == DEPTH
    S = x_prompt.shape[1]
    B = x_sample.shape[0]
    w_in_b, wo_b = w_in[0].astype(BF16), w_out[0].astype(BF16)
    wu_b, wd_b = w_up[0].astype(BF16), w_down[0].astype(BF16)
    g1, b1, g2, b2 = ln1_g[0][None], ln1_b[0][None], ln2_g[0][None], ln2_b[0][None]
    tap_bias = _tap_bias(rel_bias)

    xp = x_prompt[0]
    qa, ka, va, kvw_p, qr, kr, vr, g = _inproj_prompt(xp, w_in_b, PROMPT_ROWS)
    oatt = _attn_prompt(qa, ka, va, _band_bias(tap_bias))
    yret, st_p = _ret_prompt(qr, kr, vr, g, RET_CHUNKS_PER_STEP)
    y_p = _out_ffn(xp, oatt, yret, wo_b, g1, b1, wu_b, wd_b, g2, b2, PROMPT_ROWS, "out_ffn_prompt")

    xs = x_sample[:, 0]
    qa_s, kvw_s, qr_s, kr_s, vr_s, g_s = _inproj_sample(xs, w_in_b)
    cache_t = jnp.transpose(cache_kv_win[0], (0, 2, 3, 4, 1))
    oatt_s, yret_s, st_s = _sample_mixer(qa_s, kvw_s, qr_s, kr_s, vr_s, g_s, cache_t, state_ret[0], tap_bias)
    y_s = _out_ffn(xs, oatt_s, yret_s, wo_b, g1, b1, wu_b, wd_b, g2, b2, B, "out_ffn_sample")

    win = kvw_p.shape[1]
    kv_win_p = jnp.transpose(kvw_p.reshape(2, H_ATT, HD_ATT, win), (3, 0, 1, 2))
    return (y_p[None], y_s[:, None],
            kv_win_p[None, None], kvw_s.reshape(1, B, 1, 2, H_ATT, HD_ATT),
            st_p[None, None], st_s[None])
```

```python
import functools
import math

import jax
import jax.numpy as jnp
from jax import lax
from jax.experimental import pallas as pl
from jax.experimental.pallas import tpu as pltpu

F32 = jnp.float32
BF16 = jnp.bfloat16

D_MODEL = 1024
DEPTH = 1
PAST_LEN = 16384
W_ATT = 512
HD_ATT = 64
H_ATT = 8
DILATIONS = ((128, 1), (512, 4), (2048, 16))
TAPS = 128
MAX_WINDOW = 2048
N_BUCKETS = 32
MAX_EXACT = N_BUCKETS // 2
W_RET = 512
H_RET = 4
DK_RET = 128
DV_RET = 128
RET_CHUNK = 128
ROPE_BASE = 10000.0
D_FF = 4096
N_SPLITS = 7
ALPHA = (2.0 * DEPTH) ** 0.25
LN_EPS = 1e-5
GN_EPS = 1e-6
LOG_GAMMA = tuple(math.log(1.0 - 2.0 ** (-5.0 - h)) for h in range(H_RET))

LANES = 128
PAIRS = W_ATT // LANES
NEG = -1e30

ATT_BLOCK = 2048
Q_ROWS = 128
VMEM_LIMIT = 56 * 1024 * 1024


def _cparams(n_axes):
    return pltpu.CompilerParams(dimension_semantics=("arbitrary",) * n_axes, vmem_limit_bytes=VMEM_LIMIT)


def _rotary(u, cos, sin_signed):
    outs = []
    for h in range(H_RET):
        xh = u[:, h * LANES:(h + 1) * LANES]
        outs.append(xh * cos + pltpu.roll(xh, LANES // 2, 1) * sin_signed)
    return jnp.concatenate(outs, axis=1)


def _proj(xb, w_ref, j):
    return jnp.dot(xb, w_ref[:, j * W_ATT:(j + 1) * W_ATT], preferred_element_type=F32)


def _store_views(val, nat_ref, by4_ref, by16_ref, rows_scr, by4_scr):
    tm = val.shape[0]
    for p in range(PAIRS):
        x = val[:, p * LANES:(p + 1) * LANES]
        nat_ref[p] = x.astype(BF16)
        rows_scr[p] = x
    for p in range(PAIRS):
        for r4 in range(4):
            a = rows_scr[p, pl.ds(r4, tm // 4, stride=4), :]
            by4_ref[p, :, r4 * LANES:(r4 + 1) * LANES] = a.astype(BF16)
            by4_scr[p * 4 + r4] = a
    for p in range(PAIRS):
        for r4 in range(4):
            for rr in range(4):
                r16 = r4 + 4 * rr
                b = by4_scr[p * 4 + r4, pl.ds(rr, tm // 16, stride=4), :]
                by16_ref[p, :, r16 * LANES:(r16 + 1) * LANES] = b.astype(BF16)


def _retention_inputs(xb, w_ref, cos_ref, sin_ref, qr_ref, kr_ref, vr_ref, g_ref):
    cos = cos_ref[...]
    sin = sin_ref[...]
    qr_ref[...] = _rotary(_proj(xb, w_ref, 3), cos, sin).astype(BF16)
    kr_ref[...] = (_rotary(_proj(xb, w_ref, 4), cos, sin) * DK_RET ** -0.5).astype(BF16)
    vr_ref[...] = _proj(xb, w_ref, 5).astype(BF16)
    g_ref[...] = _proj(xb, w_ref, 6)


def _inproj_prompt_body(x_ref, w_ref, cos_ref, sin_ref,
                        q1_ref, q4_ref, q16_ref, k1_ref, k4_ref, k16_ref, v1_ref, v4_ref, v16_ref, kvw_ref,
                        qr_ref, kr_ref, vr_ref, g_ref, rows_scr, by4_scr, *, win_first_step):
    xb = x_ref[...].astype(BF16)
    _store_views(_proj(xb, w_ref, 0) * HD_ATT ** -0.5, q1_ref, q4_ref, q16_ref, rows_scr, by4_scr)
    k = _proj(xb, w_ref, 1)
    v = _proj(xb, w_ref, 2)
    _store_views(k, k1_ref, k4_ref, k16_ref, rows_scr, by4_scr)
    _store_views(v, v1_ref, v4_ref, v16_ref, rows_scr, by4_scr)

    @pl.when(pl.program_id(0) >= win_first_step)
    def _():
        kvw_ref[:W_ATT, :] = k.T
        kvw_ref[W_ATT:, :] = v.T

    _retention_inputs(xb, w_ref, cos_ref, sin_ref, qr_ref, kr_ref, vr_ref, g_ref)


def _inproj_sample_body(x_ref, w_ref, cos_ref, sin_ref, qa_ref, kvw_ref, qr_ref, kr_ref, vr_ref, g_ref):
    xb = x_ref[...].astype(BF16)
    qa_ref[...] = _proj(xb, w_ref, 0) * HD_ATT ** -0.5
    kvw_ref[:, :W_ATT] = _proj(xb, w_ref, 1)
    kvw_ref[:, W_ATT:] = _proj(xb, w_ref, 2)
    _retention_inputs(xb, w_ref, cos_ref, sin_ref, qr_ref, kr_ref, vr_ref, g_ref)


def _rope_tables(pos):
    half = DK_RET // 2
    inv_freq = 1.0 / (ROPE_BASE ** jnp.linspace(0.0, 1.0, half, dtype=F32))
    ang = pos.astype(F32)[:, None] * inv_freq[None, :]
    cos, sin = jnp.cos(ang), jnp.sin(ang)
    return jnp.concatenate([cos, cos], axis=1), jnp.concatenate([-sin, sin], axis=1)


def _inproj_prompt(x, w_in_b, tm):
    S = x.shape[0]
    win = min(MAX_WINDOW, S)
    steps = S // tm
    win_first_step = steps - win // tm
    cos, sin = _rope_tables(jnp.arange(S, dtype=jnp.int32))
    row = lambda i: (i, 0)
    view_specs = [pl.BlockSpec((PAIRS, tm // d, d * LANES), lambda i: (0, i, 0)) for d in (1, 4, 16)] * 3
    view_shapes = [jax.ShapeDtypeStruct((PAIRS, S // d, d * LANES), BF16) for d in (1, 4, 16)] * 3
    half_spec = pl.BlockSpec((tm, W_RET), row)
    return pl.pallas_call(
        functools.partial(_inproj_prompt_body, win_first_step=win_first_step),
        grid=(steps,),
        in_specs=[pl.BlockSpec((tm, D_MODEL), row),
                  pl.BlockSpec((D_MODEL, N_SPLITS * W_ATT), lambda i: (0, 0)),
                  pl.BlockSpec((tm, LANES), row),
                  pl.BlockSpec((tm, LANES), row)],
        out_specs=view_specs + [
            pl.BlockSpec((2 * W_ATT, tm), lambda i: (0, jnp.maximum(i - win_first_step, 0))),
            half_spec, half_spec, half_spec, half_spec],
        out_shape=view_shapes + [
            jax.ShapeDtypeStruct((2 * W_ATT, win), F32),
            jax.ShapeDtypeStruct((S, W_RET), BF16),
            jax.ShapeDtypeStruct((S, W_RET), BF16),
            jax.ShapeDtypeStruct((S, W_RET), BF16),
            jax.ShapeDtypeStruct((S, W_RET), F32)],
        scratch_shapes=[pltpu.VMEM((PAIRS, tm, LANES), F32), pltpu.VMEM((4 * PAIRS, tm // 4, LANES), F32)],
        compiler_params=_cparams(1),
        name="inproj_prompt",
    )(x, w_in_b, cos, sin)


def _inproj_sample(x, w_in_b):
    B = x.shape[0]
    cos, sin = _rope_tables(jnp.full((B,), PAST_LEN, dtype=jnp.int32))
    full = lambda shape: pl.BlockSpec(shape, lambda i: (0, 0))
    return pl.pallas_call(
        _inproj_sample_body,
        grid=(1,),
        in_specs=[full((B, D_MODEL)), full((D_MODEL, N_SPLITS * W_ATT)), full((B, LANES)), full((B, LANES))],
        out_specs=[full((B, W_ATT)), full((B, 2 * W_ATT)), full((B, W_RET)), full((B, W_RET)),
                   full((B, W_RET)), full((B, W_RET))],
        out_shape=[jax.ShapeDtypeStruct((B, W_ATT), F32),
                   jax.ShapeDtypeStruct((B, 2 * W_ATT), F32),
                   jax.ShapeDtypeStruct((B, W_RET), BF16),
                   jax.ShapeDtypeStruct((B, W_RET), BF16),
                   jax.ShapeDtypeStruct((B, W_RET), BF16),
                   jax.ShapeDtypeStruct((B, W_RET), F32)],
        compiler_params=_cparams(1),
        name="inproj_sample",
    )(x, w_in_b, cos, sin)


def _t5_bucket(dist):
    is_small = dist < MAX_EXACT
    d_f = jnp.maximum(dist, 1).astype(F32)
    large = MAX_EXACT + (jnp.log(d_f / MAX_EXACT) / math.log(MAX_WINDOW / MAX_EXACT)
                         * (N_BUCKETS - MAX_EXACT)).astype(jnp.int32)
    large = jnp.minimum(large, N_BUCKETS - 1)
    return jnp.where(is_small, dist, large)


def _tap_bias(rel_bias):
    taps = jnp.arange(TAPS + 1)
    return jnp.stack([rel_bias[_t5_bucket(taps * d)].astype(F32).T for _, d in DILATIONS])


def _band_bias(tap_bias):
    n = Q_ROWS + 2 * Q_ROWS
    w = jnp.concatenate([tap_bias[..., ::-1], jnp.full(tap_bias.shape[:-1] + (n - TAPS - 1,), NEG, F32)], axis=-1)
    skew = jnp.tile(w, Q_ROWS)[..., :Q_ROWS * (n - 1)].reshape(tap_bias.shape[:-1] + (Q_ROWS, n - 1))
    return skew[..., :2 * Q_ROWS]


def _window_bias(tap_bias, window):
    rows = []
    for i, (_, d) in enumerate(DILATIONS):
        taps_rev = tap_bias[i, :, TAPS:0:-1]
        seg = jnp.concatenate([taps_rev[..., None], jnp.full((H_ATT, TAPS, d - 1), NEG, F32)], axis=-1)
        rows.append(jnp.concatenate([jnp.full((H_ATT, window - TAPS * d), NEG, F32),
                                     seg.reshape(H_ATT, TAPS * d)], axis=-1))
    return jnp.stack(rows)


def _attn_unit(q, k, v, bias2, head0_b, head0):
    qs = jnp.concatenate([q * head0_b, q * (1 - head0_b)], axis=0)
    s = lax.dot_general(qs, k, (((1,), (1,)), ((), ())), preferred_element_type=F32) + bias2
    m = jnp.max(s, axis=1, keepdims=True)
    p = jnp.exp(s - m)
    l = jnp.sum(p, axis=1, keepdims=True)
    o = jnp.dot(p.astype(BF16), v, preferred_element_type=F32) / l
    lse = m + jnp.log(l)
    o_pair = jnp.where(head0, o[:Q_ROWS], o[Q_ROWS:])
    lse_pair = jnp.where(head0, jnp.broadcast_to(lse[:Q_ROWS], (Q_ROWS, LANES)),
                         jnp.broadcast_to(lse[Q_ROWS:], (Q_ROWS, LANES)))
    return o_pair, lse_pair


def _attn_body(q1, q4, q16, k1c, k1p, k4c, k4p, k16c, k16p, v1c, v1p, v4c, v4p, v16c, v16p, bias_ref,
               out_ref, o16, l16, o4, l4):
    first = pl.program_id(0) == 0
    lane = lax.broadcasted_iota(jnp.int32, (Q_ROWS, LANES), 1)
    head0 = lane < HD_ATT
    head0_b = jnp.where(head0, 1.0, 0.0).astype(BF16)
    col = lax.broadcasted_iota(jnp.int32, (2 * Q_ROWS, 2 * Q_ROWS), 1)
    no_prev = jnp.where(jnp.logical_and(first, col < Q_ROWS), NEG, 0.0)

    def bias2(branch, masked_prev):
        b = jnp.concatenate([bias_ref[branch, 0], bias_ref[branch, 1]], axis=0)
        return b + no_prev if masked_prev else b

    def cat(a, b):
        return jnp.concatenate([a, b], axis=0)

    b16 = bias2(2, True)
    for r in range(16):
        sl = slice(r * LANES, (r + 1) * LANES)
        o, l = _attn_unit(q16[:, sl], cat(k16p[:, sl], k16c[:, sl]), cat(v16p[:, sl], v16c[:, sl]),
                          b16, head0_b, head0)
        o16[pl.ds(r, Q_ROWS, stride=16), :] = o
        l16[pl.ds(r, Q_ROWS, stride=16), :] = l

    b4_first = bias2(1, True)
    b4 = bias2(1, False)
    for r in range(4):
        sl = slice(r * LANES, (r + 1) * LANES)
        for b in range(4):
            rows = slice(b * Q_ROWS, (b + 1) * Q_ROWS)
            if b == 0:
                k = cat(k4p[:, sl], k4c[rows, sl])
                v = cat(v4p[:, sl], v4c[rows, sl])
            else:
                both = slice((b - 1) * Q_ROWS, (b + 1) * Q_ROWS)
                k, v = k4c[both, sl], v4c[both, sl]
            o, l = _attn_unit(q4[rows, sl], k, v, b4_first if b == 0 else b4, head0_b, head0)
            o4[pl.ds(b * 4 * Q_ROWS + r, Q_ROWS, stride=4), :] = o
            l4[pl.ds(b * 4 * Q_ROWS + r, Q_ROWS, stride=4), :] = l

    b1_first = bias2(0, True)
    b1 = bias2(0, False)
    for b in range(ATT_BLOCK // Q_ROWS):
        rows = slice(b * Q_ROWS, (b + 1) * Q_ROWS)
        if b == 0:
            k = cat(k1p[...], k1c[rows, :])
            v = cat(v1p[...], v1c[rows, :])
        else:
            both = slice((b - 1) * Q_ROWS, (b + 1) * Q_ROWS)
            k, v = k1c[both, :], v1c[both, :]
        oa, la = _attn_unit(q1[rows, :], k, v, b1_first if b == 0 else b1, head0_b, head0)
        ob, lb = o4[rows, :], l4[rows, :]
        oc, lc = o16[rows, :], l16[rows, :]
        top = jnp.maximum(jnp.maximum(la, lb), lc)
        ea, eb, ec = jnp.exp(la - top), jnp.exp(lb - top), jnp.exp(lc - top)
        out_ref[rows, :] = ((ea * oa + eb * ob + ec * oc) / (ea + eb + ec)).astype(BF16)


def _attn_prompt(q1, q4, q16, k1, k4, k16, v1, v4, v16, band_bias):
    S = q1.shape[1]
    assert S % ATT_BLOCK == 0
    steps = S // ATT_BLOCK

    def cur(d):
        return pl.BlockSpec((None, ATT_BLOCK // d, d * LANES), lambda c, p: (p, c, 0))

    def prev(d):
        per_step = ATT_BLOCK // d // Q_ROWS
        return pl.BlockSpec((None, Q_ROWS, d * LANES), lambda c, p: (p, jnp.maximum(c * per_step - 1, 0), 0))

    scratch = pltpu.VMEM((ATT_BLOCK, LANES), F32)
    return pl.pallas_call(
        _attn_body,
        grid=(steps, PAIRS),
        in_specs=[cur(1), cur(4), cur(16),
                  cur(1), prev(1), cur(4), prev(4), cur(16), prev(16),
                  cur(1), prev(1), cur(4), prev(4), cur(16), prev(16),
                  pl.BlockSpec((len(DILATIONS), 2, Q_ROWS, 2 * Q_ROWS), lambda c, p: (0, p, 0, 0))],
        out_specs=pl.BlockSpec((ATT_BLOCK, LANES), lambda c, p: (c, p)),
        out_shape=jax.ShapeDtypeStruct((S, W_ATT), BF16),
        scratch_shapes=[scratch, scratch, scratch, scratch],
        compiler_params=_cparams(2),
        name="attn_prompt",
    )(q1, q4, q16, k1, k1, k4, k4, k16, k16, v1, v1, v4, v4, v16, v16, band_bias)


def _group_norm_gate(o, g):
    mu = jnp.mean(o, axis=1, keepdims=True)
    var = jnp.mean(jnp.square(o - mu), axis=1, keepdims=True)
    return g * jax.nn.sigmoid(g) * ((o - mu) * lax.rsqrt(var + GN_EPS))


def _ret_prompt_body(q_ref, k_ref, v_ref, g_ref, y_ref, st_ref, state, *, chunks):
    C = RET_CHUNK

    @pl.when(pl.program_id(0) == 0)
    def _():
        state[...] = jnp.zeros_like(state)

    n = lax.broadcasted_iota(jnp.int32, (C, 1), 0).astype(F32)
    diff = (lax.broadcasted_iota(jnp.int32, (C, C), 0) - lax.broadcasted_iota(jnp.int32, (C, C), 1)).astype(F32)
    for h in range(H_RET):
        lg = LOG_GAMMA[h]
        decay = jnp.where(diff >= 0, jnp.exp(lg * jnp.maximum(diff, 0.0)), 0.0)
        q_decay = jnp.exp(lg * (n + 1.0))
        k_decay = jnp.exp(lg * (C - 1.0 - n))
        chunk_decay = math.exp(lg * C)
        cols = slice(h * DK_RET, (h + 1) * DK_RET)
        S = state[h]
        for c in range(chunks):
            rows = slice(c * C, (c + 1) * C)
            q, k, v = q_ref[rows, cols], k_ref[rows, cols], v_ref[rows, cols]
            scores = lax.dot_general(q, k, (((1,), (1,)), ((), ())), preferred_element_type=F32) * decay
            o = (jnp.dot(scores.astype(BF16), v, preferred_element_type=F32)
                 + jnp.dot(q, S.astype(BF16), preferred_element_type=F32) * q_decay)
            k_dec_t = (k.astype(F32) * k_decay).T.astype(BF16)
            S = chunk_decay * S + jnp.dot(k_dec_t, v, preferred_element_type=F32)
            y_ref[rows, cols] = _group_norm_gate(o, g_ref[rows, cols]).astype(BF16)
        state[h] = S
    st_ref[...] = state[...]


def _ret_prompt(qr, kr, vr, g, chunks):
    S = qr.shape[0]
    rows = chunks * RET_CHUNK
    spec = pl.BlockSpec((rows, W_RET), lambda i: (i, 0))
    return pl.pallas_call(
        functools.partial(_ret_prompt_body, chunks=chunks),
        grid=(S // rows,),
        in_specs=[spec, spec, spec, spec],
        out_specs=[spec, pl.BlockSpec((H_RET, DK_RET, DV_RET), lambda i: (0, 0, 0))],
        out_shape=[jax.ShapeDtypeStruct((S, W_RET), BF16),
                   jax.ShapeDtypeStruct((H_RET, DK_RET, DV_RET), F32)],
        scratch_shapes=[pltpu.VMEM((H_RET, DK_RET, DV_RET), F32)],
        compiler_params=_cparams(1),
        name="retention_prompt",
    )(qr, kr, vr, g)


def _as_column(row_pair):
    return jnp.broadcast_to(row_pair, (LANES, LANES)).T


def _sample_mixer_body(qa_ref, kvw_ref, qr_ref, kr_ref, vr_ref, g_ref, cache_ref, st_ref, bias_ref, bias0_ref,
                       oatt_ref, yret_ref, stout_ref):
    W = cache_ref.shape[-1]
    logit_rows, self_rows = [], []
    for p in range(PAIRS):
        lanes = slice(p * LANES, (p + 1) * LANES)
        q_pair = qa_ref[:, lanes]
        k_pair = kvw_ref[:, lanes]
        q_col = _as_column(q_pair)
        for hh in range(2):
            feat = slice(hh * HD_ATT, (hh + 1) * HD_ATT)
            logit_rows.append(jnp.sum(cache_ref[0, 2 * p + hh] * q_col[feat, 0:1], axis=0, keepdims=True))
            self_rows.append(jnp.sum(q_pair[:, feat] * k_pair[:, feat], axis=1, keepdims=True))
    logits = jnp.concatenate(logit_rows, axis=0)
    s_self = jnp.concatenate(self_rows, axis=0) + bias0_ref[:, 0:1]

    probs, p_selfs, denoms, lses = [], [], [], []
    for i, (_, d) in enumerate(DILATIONS):
        lo = W - TAPS * d
        s = logits[:, lo:] + bias_ref[i, :, lo:]
        m = jnp.maximum(jnp.max(s, axis=1, keepdims=True), s_self)
        pr = jnp.exp(s - m)
        p_self = jnp.exp(s_self - m)
        l = jnp.sum(pr, axis=1, keepdims=True) + p_self
        probs.append(pr)
        p_selfs.append(p_self)
        denoms.append(l)
        lses.append(m + jnp.log(l))
    top = jnp.maximum(jnp.maximum(lses[0], lses[1]), lses[2])
    e = [jnp.exp(x - top) for x in lses]
    e_sum = e[0] + e[1] + e[2]
    coef = [e[i] / (e_sum * denoms[i]) for i in range(3)]
    lo4, lo1 = W - TAPS * 4, W - TAPS
    p16, p4, p1 = coef[2] * probs[2], coef[1] * probs[1], coef[0] * probs[0]
    p_all = jnp.concatenate([p16[:, :lo4], p16[:, lo4:lo1] + p4[:, :lo1 - lo4],
                             p16[:, lo1:] + p4[:, lo1 - lo4:] + p1], axis=1)
    c_self = coef[0] * p_selfs[0] + coef[1] * p_selfs[1] + coef[2] * p_selfs[2]
    for p in range(PAIRS):
        lanes = slice(p * LANES, (p + 1) * LANES)
        v_col = _as_column(kvw_ref[:, W_ATT + p * LANES:W_ATT + (p + 1) * LANES])
        out_cols = []
        for hh in range(2):
            h = 2 * p + hh
            feat = slice(hh * HD_ATT, (hh + 1) * HD_ATT)
            pv = jnp.sum(cache_ref[1, h] * p_all[h:h + 1, :], axis=1, keepdims=True)
            out_cols.append(pv + c_self[h:h + 1, :] * v_col[feat, 0:1])
        col = jnp.broadcast_to(jnp.concatenate(out_cols, axis=0), (LANES, LANES))
        oatt_ref[:, lanes] = col.T[0:1, :].astype(BF16)

    for h in range(H_RET):
        cols = slice(h * DK_RET, (h + 1) * DK_RET)
        gamma = math.exp(LOG_GAMMA[h])
        qh = qr_ref[:, cols]
        kh = kr_ref[:, cols].astype(F32)
        vh = vr_ref[:, cols].astype(F32)
        S = st_ref[h]
        qk = jnp.sum(qh.astype(F32) * kh, axis=1, keepdims=True)
        cross = jnp.dot(jnp.broadcast_to(qh, (16, DK_RET)), S.astype(BF16), preferred_element_type=F32)[:1]
        o = qk * vh + cross * gamma
        stout_ref[h] = gamma * S + _as_column(kh) * vh
        yret_ref[:, cols] = _group_norm_gate(o, g_ref[:, cols]).astype(BF16)


def _sample_mixer(qa, kvw, qr, kr, vr, g, cache_t, state, tap_bias):
    B, _, _, _, W = cache_t.shape
    assert W == MAX_WINDOW and PAST_LEN >= MAX_WINDOW
    row3 = lambda a: a.reshape(B, 1, a.shape[-1])
    vec = lambda width: pl.BlockSpec((None, 1, width), lambda b: (b, 0, 0))
    bias0 = jnp.broadcast_to(tap_bias[0, :, 0:1], (H_ATT, LANES))
    st_spec = pl.BlockSpec((None, H_RET, DK_RET, DV_RET), lambda b: (b, 0, 0, 0))
    oatt, yret, st = pl.pallas_call(
        _sample_mixer_body,
        grid=(B,),
        in_specs=[vec(W_ATT), vec(2 * W_ATT), vec(W_RET), vec(W_RET), vec(W_RET), vec(W_RET),
                  pl.BlockSpec((None, 2, H_ATT, HD_ATT, W), lambda b: (b, 0, 0, 0, 0)),
                  st_spec,
                  pl.BlockSpec((len(DILATIONS), H_ATT, W), lambda b: (0, 0, 0)),
                  pl.BlockSpec((H_ATT, LANES), lambda b: (0, 0))],
        out_specs=[vec(W_ATT), vec(W_RET), st_spec],
        out_shape=[jax.ShapeDtypeStruct((B, 1, W_ATT), BF16),
                   jax.ShapeDtypeStruct((B, 1, W_RET), BF16),
                   jax.ShapeDtypeStruct(state.shape, F32)],
        compiler_params=_cparams(1),
        name="sample_mixer",
    )(row3(qa), row3(kvw), row3(qr), row3(kr), row3(vr), row3(g), cache_t, state,
      _window_bias(tap_bias, W), bias0)
    return oatt.reshape(B, W_ATT), yret.reshape(B, W_RET), st


FF_CHUNK = 1024


def _layernorm(z, g, b):
    mu = jnp.mean(z, axis=1, keepdims=True)
    var = jnp.mean(jnp.square(z - mu), axis=1, keepdims=True)
    return (z - mu) * lax.rsqrt(var + LN_EPS) * g + b


def _out_ffn_body(x_ref, oatt_ref, yret_ref, wo_ref, g1_ref, b1_ref, wu_ref, wd_ref, g2_ref, b2_ref, y_ref):
    mix = (jnp.dot(oatt_ref[...], wo_ref[:W_ATT, :], preferred_element_type=F32)
           + jnp.dot(yret_ref[...], wo_ref[W_ATT:, :], preferred_element_type=F32))
    x1 = _layernorm(ALPHA * x_ref[...] + mix, g1_ref[...], b1_ref[...])
    x1b = x1.astype(BF16)
    ffn = jnp.zeros_like(x1)
    for j in range(D_FF // FF_CHUNK):
        cols = slice(j * FF_CHUNK, (j + 1) * FF_CHUNK)
        h = jnp.square(jnp.maximum(jnp.dot(x1b, wu_ref[:, cols], preferred_element_type=F32), 0.0))
        ffn = ffn + jnp.dot(h.astype(BF16), wd_ref[cols, :], preferred_element_type=F32)
    y_ref[...] = _layernorm(ALPHA * x1 + ffn, g2_ref[...], b2_ref[...])


def _out_ffn(x, oatt, yret, wo_b, g1, b1, wu_b, wd_b, g2, b2, tm, name):
    M = x.shape[0]
    row = lambda i: (i, 0)
    const = lambda shape: pl.BlockSpec(shape, lambda i: (0, 0))
    return pl.pallas_call(
        _out_ffn_body,
        grid=(M // tm,),
        in_specs=[pl.BlockSpec((tm, D_MODEL), row), pl.BlockSpec((tm, W_ATT), row), pl.BlockSpec((tm, W_RET), row),
                  const((W_ATT + W_RET, D_MODEL)), const((1, D_MODEL)), const((1, D_MODEL)),
                  const((D_MODEL, D_FF)), const((D_FF, D_MODEL)), const((1, D_MODEL)), const((1, D_MODEL))],
        out_specs=pl.BlockSpec((tm, D_MODEL), row),
        out_shape=jax.ShapeDtypeStruct((M, D_MODEL), F32),
        compiler_params=_cparams(1),
        name=name,
    )(x, oatt, yret, wo_b, g1, b1, wu_b, wd_b, g2, b2)


PROMPT_ROWS = 512
RET_CHUNKS_PER_STEP = 4


def kernel(x_prompt, x_sample, cache_kv_win, state_ret, w_in, rel_bias, w_out,
           ln1_g, ln1_b, w_up, w_down, ln2_g, ln2_b):
    assert x_prompt.shape[0] == 1 and x_sample.shape[1] == 1 and w_in.shape[0] == DEPTH
    S = x_prompt.shape[1]
    B = x_sample.shape[0]
    w_in_b, wo_b = w_in[0].astype(BF16), w_out[0].astype(BF16)
    wu_b, wd_b = w_up[0].astype(BF16), w_down[0].astype(BF16)
    g1, b1, g2, b2 = ln1_g[0][None], ln1_b[0][None], ln2_g[0][None], ln2_b[0][None]
    tap_bias = _tap_bias(rel_bias)

    xp = x_prompt[0]
    *qkv_views, kvw_p, qr, kr, vr, g = _inproj_prompt(xp, w_in_b, PROMPT_ROWS)
    oatt = _attn_prompt(*qkv_views, _band_bias(tap_bias))
    yret, st_p = _ret_prompt(qr, kr, vr, g, RET_CHUNKS_PER_STEP)
    y_p = _out_ffn(xp, oatt, yret, wo_b, g1, b1, wu_b, wd_b, g2, b2, PROMPT_ROWS, "out_ffn_prompt")

    xs = x_sample[:, 0]
    qa_s, kvw_s, qr_s, kr_s, vr_s, g_s = _inproj_sample(xs, w_in_b)
    cache_t = jnp.transpose(cache_kv_win[0], (0, 2, 3, 4, 1))
    oatt_s, yret_s, st_s = _sample_mixer(qa_s, kvw_s, qr_s, kr_s, vr_s, g_s, cache_t, state_ret[0], tap_bias)
    y_s = _out_ffn(xs, oatt_s, yret_s, wo_b, g1, b1, wu_b, wd_b, g2, b2, B, "out_ffn_sample")

    win = kvw_p.shape[1]
    kv_win_p = jnp.transpose(kvw_p.reshape(2, H_ATT, HD_ATT, win), (3, 0, 1, 2))
    return (y_p[None], y_s[:, None],
            kv_win_p[None, None], kvw_s.reshape(1, B, 1, 2, H_ATT, HD_ATT),
            st_p[None, None], st_s[None])
```

```python
import functools
import math

import jax
import jax.numpy as jnp
from jax import lax
from jax.experimental import pallas as pl
from jax.experimental.pallas import tpu as pltpu

F32 = jnp.float32
BF16 = jnp.bfloat16

D_MODEL = 1024
DEPTH = 1
PAST_LEN = 16384
W_ATT = 512
HD_ATT = 64
H_ATT = 8
DILATIONS = ((128, 1), (512, 4), (2048, 16))
TAPS = 128
MAX_WINDOW = 2048
N_BUCKETS = 32
MAX_EXACT = N_BUCKETS // 2
W_RET = 512
H_RET = 4
DK_RET = 128
DV_RET = 128
RET_CHUNK = 128
ROPE_BASE = 10000.0
D_FF = 4096
N_SPLITS = 7
ALPHA = (2.0 * DEPTH) ** 0.25
LN_EPS = 1e-5
GN_EPS = 1e-6
LOG_GAMMA = tuple(math.log(1.0 - 2.0 ** (-5.0 - h)) for h in range(H_RET))

LANES = 128
PAIRS = W_ATT // LANES
NEG = -1e30

ATT_BLOCK = 2048
Q_ROWS = 128
VMEM_LIMIT = 56 * 1024 * 1024


def _cparams(n_axes):
    return pltpu.CompilerParams(dimension_semantics=("arbitrary",) * n_axes, vmem_limit_bytes=VMEM_LIMIT)


def _rotary(u, cos, sin_signed):
    outs = []
    for h in range(H_RET):
        xh = u[:, h * LANES:(h + 1) * LANES]
        outs.append(xh * cos + pltpu.roll(xh, LANES // 2, 1) * sin_signed)
    return jnp.concatenate(outs, axis=1)


def _proj(xb, w_ref, j):
    return jnp.dot(xb, w_ref[:, j * W_ATT:(j + 1) * W_ATT], preferred_element_type=F32)


def _store_views(val, nat_ref, by4_ref, by16_ref, rows_scr, by4_scr):
    tm = val.shape[0]
    for p in range(PAIRS):
        x = val[:, p * LANES:(p + 1) * LANES]
        nat_ref[p] = x.astype(BF16)
        rows_scr[p] = x
    for p in range(PAIRS):
        for r4 in range(4):
            a = rows_scr[p, pl.ds(r4, tm // 4, stride=4), :]
            by4_ref[p, :, r4 * LANES:(r4 + 1) * LANES] = a.astype(BF16)
            by4_scr[p * 4 + r4] = a
    for p in range(PAIRS):
        for r4 in range(4):
            for rr in range(4):
                r16 = r4 + 4 * rr
                b = by4_scr[p * 4 + r4, pl.ds(rr, tm // 16, stride=4), :]
                by16_ref[p, :, r16 * LANES:(r16 + 1) * LANES] = b.astype(BF16)


def _retention_inputs(xb, w_ref, cos, sin, qr_ref, kr_ref, vr_ref, g_ref):
    qr_ref[...] = _rotary(_proj(xb, w_ref, 3), cos, sin).astype(BF16)
    kr_ref[...] = (_rotary(_proj(xb, w_ref, 4), cos, sin) * DK_RET ** -0.5).astype(BF16)
    vr_ref[...] = _proj(xb, w_ref, 5).astype(BF16)
    g_ref[...] = _proj(xb, w_ref, 6)


def _inproj_prompt_body(x_ref, w_ref, cos_row_ref, sin_row_ref, cos_step_ref, sin_step_ref,
                        q1_ref, q4_ref, q16_ref, k1_ref, k4_ref, k16_ref, v1_ref, v4_ref, v16_ref, kvw_ref,
                        qr_ref, kr_ref, vr_ref, g_ref, rows_scr, by4_scr, *, win_first_step):
    xb = x_ref[...].astype(BF16)
    _store_views(_proj(xb, w_ref, 0) * HD_ATT ** -0.5, q1_ref, q4_ref, q16_ref, rows_scr, by4_scr)
    k = _proj(xb, w_ref, 1)
    v = _proj(xb, w_ref, 2)
    _store_views(k, k1_ref, k4_ref, k16_ref, rows_scr, by4_scr)
    _store_views(v, v1_ref, v4_ref, v16_ref, rows_scr, by4_scr)

    @pl.when(pl.program_id(0) >= win_first_step)
    def _():
        kvw_ref[:W_ATT, :] = k.T
        kvw_ref[W_ATT:, :] = v.T

    step = pl.ds(pl.program_id(0), 1)
    ca, sa = cos_step_ref[step, :], sin_step_ref[step, :]
    cb, sb = cos_row_ref[...], sin_row_ref[...]
    sign = jnp.where(lax.broadcasted_iota(jnp.int32, (1, LANES), 1) < LANES // 2, -1.0, 1.0)
    _retention_inputs(xb, w_ref, ca * cb - sa * sb, (sa * cb + ca * sb) * sign, qr_ref, kr_ref, vr_ref, g_ref)


def _inproj_sample_body(x_ref, w_ref, cos_ref, sin_ref, qa_ref, kvw_ref, qr_ref, kr_ref, vr_ref, g_ref):
    xb = x_ref[...].astype(BF16)
    qa_ref[...] = _proj(xb, w_ref, 0) * HD_ATT ** -0.5
    kvw_ref[:, :W_ATT] = _proj(xb, w_ref, 1)
    kvw_ref[:, W_ATT:] = _proj(xb, w_ref, 2)
    _retention_inputs(xb, w_ref, cos_ref[...], sin_ref[...], qr_ref, kr_ref, vr_ref, g_ref)


def _rope_tables(pos):
    half = DK_RET // 2
    inv_freq = 1.0 / (ROPE_BASE ** jnp.linspace(0.0, 1.0, half, dtype=F32))
    ang = pos.astype(F32)[:, None] * inv_freq[None, :]
    cos, sin = jnp.cos(ang), jnp.sin(ang)
    return jnp.concatenate([cos, cos], axis=1), jnp.concatenate([sin, sin], axis=1)


def _inproj_prompt(x, w_in_b, tm):
    S = x.shape[0]
    win = min(MAX_WINDOW, S)
    steps = S // tm
    win_first_step = steps - win // tm
    cos, sin = _rope_tables(jnp.concatenate([jnp.arange(tm, dtype=jnp.int32),
                                             jnp.arange(steps, dtype=jnp.int32) * tm]))
    const = lambda i: (0, 0)
    row = lambda i: (i, 0)
    view_specs = [pl.BlockSpec((PAIRS, tm // d, d * LANES), lambda i: (0, i, 0)) for d in (1, 4, 16)] * 3
    view_shapes = [jax.ShapeDtypeStruct((PAIRS, S // d, d * LANES), BF16) for d in (1, 4, 16)] * 3
    half_spec = pl.BlockSpec((tm, W_RET), row)
    return pl.pallas_call(
        functools.partial(_inproj_prompt_body, win_first_step=win_first_step),
        grid=(steps,),
        in_specs=[pl.BlockSpec((tm, D_MODEL), row),
                  pl.BlockSpec((D_MODEL, N_SPLITS * W_ATT), lambda i: (0, 0)),
                  pl.BlockSpec((tm, LANES), const), pl.BlockSpec((tm, LANES), const),
                  pl.BlockSpec((steps, LANES), const), pl.BlockSpec((steps, LANES), const)],
        out_specs=view_specs + [
            pl.BlockSpec((2 * W_ATT, tm), lambda i: (0, jnp.maximum(i - win_first_step, 0))),
            half_spec, half_spec, half_spec, half_spec],
        out_shape=view_shapes + [
            jax.ShapeDtypeStruct((2 * W_ATT, win), F32),
            jax.ShapeDtypeStruct((S, W_RET), BF16),
            jax.ShapeDtypeStruct((S, W_RET), BF16),
            jax.ShapeDtypeStruct((S, W_RET), BF16),
            jax.ShapeDtypeStruct((S, W_RET), F32)],
        scratch_shapes=[pltpu.VMEM((PAIRS, tm, LANES), F32), pltpu.VMEM((4 * PAIRS, tm // 4, LANES), F32)],
        compiler_params=_cparams(1),
        name="inproj_prompt",
    )(x, w_in_b, cos[:tm], sin[:tm], cos[tm:], sin[tm:])


def _inproj_sample(x, w_in_b):
    B = x.shape[0]
    cos, sin = _rope_tables(jnp.full((B,), PAST_LEN, dtype=jnp.int32))
    sin = sin * jnp.where(jnp.arange(LANES) < LANES // 2, -1.0, 1.0)
    full = lambda shape: pl.BlockSpec(shape, lambda i: (0, 0))
    return pl.pallas_call(
        _inproj_sample_body,
        grid=(1,),
        in_specs=[full((B, D_MODEL)), full((D_MODEL, N_SPLITS * W_ATT)), full((B, LANES)), full((B, LANES))],
        out_specs=[full((B, W_ATT)), full((B, 2 * W_ATT)), full((B, W_RET)), full((B, W_RET)),
                   full((B, W_RET)), full((B, W_RET))],
        out_shape=[jax.ShapeDtypeStruct((B, W_ATT), F32),
                   jax.ShapeDtypeStruct((B, 2 * W_ATT), F32),
                   jax.ShapeDtypeStruct((B, W_RET), BF16),
                   jax.ShapeDtypeStruct((B, W_RET), BF16),
                   jax.ShapeDtypeStruct((B, W_RET), BF16),
                   jax.ShapeDtypeStruct((B, W_RET), F32)],
        compiler_params=_cparams(1),
        name="inproj_sample",
    )(x, w_in_b, cos, sin)


def _t5_bucket(dist):
    is_small = dist < MAX_EXACT
    d_f = jnp.maximum(dist, 1).astype(F32)
    large = MAX_EXACT + (jnp.log(d_f / MAX_EXACT) / math.log(MAX_WINDOW / MAX_EXACT)
                         * (N_BUCKETS - MAX_EXACT)).astype(jnp.int32)
    large = jnp.minimum(large, N_BUCKETS - 1)
    return jnp.where(is_small, dist, large)


def _tap_bias(rel_bias):
    taps = jnp.arange(TAPS + 1)
    return jnp.stack([rel_bias[_t5_bucket(taps * d)].astype(F32).T for _, d in DILATIONS])


def _band_bias(tap_bias):
    n = Q_ROWS + 2 * Q_ROWS
    w = jnp.concatenate([tap_bias[..., ::-1], jnp.full(tap_bias.shape[:-1] + (n - TAPS - 1,), NEG, F32)], axis=-1)
    skew = jnp.tile(w, Q_ROWS)[..., :Q_ROWS * (n - 1)].reshape(tap_bias.shape[:-1] + (Q_ROWS, n - 1))
    return skew[..., :2 * Q_ROWS]


def _window_bias(tap_bias, window):
    rows = []
    for i, (_, d) in enumerate(DILATIONS):
        taps_rev = tap_bias[i, :, TAPS:0:-1]
        seg = jnp.concatenate([taps_rev[..., None], jnp.full((H_ATT, TAPS, d - 1), NEG, F32)], axis=-1)
        rows.append(jnp.concatenate([jnp.full((H_ATT, window - TAPS * d), NEG, F32),
                                     seg.reshape(H_ATT, TAPS * d)], axis=-1))
    return jnp.stack(rows)


def _attn_unit(q, k, v, bias2, head0_b, head0):
    qs = jnp.concatenate([q * head0_b, q * (1 - head0_b)], axis=0)
    s = lax.dot_general(qs, k, (((1,), (1,)), ((), ())), preferred_element_type=F32) + bias2
    m = jnp.max(s, axis=1, keepdims=True)
    p = jnp.exp(s - m)
    l = jnp.sum(p, axis=1, keepdims=True)
    o = jnp.dot(p.astype(BF16), v, preferred_element_type=F32) / l
    lse = m + jnp.log(l)
    o_pair = jnp.where(head0, o[:Q_ROWS], o[Q_ROWS:])
    lse_pair = jnp.where(head0, jnp.broadcast_to(lse[:Q_ROWS], (Q_ROWS, LANES)),
                         jnp.broadcast_to(lse[Q_ROWS:], (Q_ROWS, LANES)))
    return o_pair, lse_pair


def _attn_body(q1, q4, q16, k1c, k1p, k4c, k4p, k16c, k16p, v1c, v1p, v4c, v4p, v16c, v16p, bias_ref,
               out_ref, o16, l16, o4, l4):
    first = pl.program_id(0) == 0
    lane = lax.broadcasted_iota(jnp.int32, (Q_ROWS, LANES), 1)
    head0 = lane < HD_ATT
    head0_b = jnp.where(head0, 1.0, 0.0).astype(BF16)
    col = lax.broadcasted_iota(jnp.int32, (2 * Q_ROWS, 2 * Q_ROWS), 1)
    no_prev = jnp.where(jnp.logical_and(first, col < Q_ROWS), NEG, 0.0)

    def bias2(branch, masked_prev):
        b = jnp.concatenate([bias_ref[branch, 0], bias_ref[branch, 1]], axis=0)
        return b + no_prev if masked_prev else b

    def cat(a, b):
        return jnp.concatenate([a, b], axis=0)

    b16 = bias2(2, True)
    for r in range(16):
        sl = slice(r * LANES, (r + 1) * LANES)
        o, l = _attn_unit(q16[:, sl], cat(k16p[:, sl], k16c[:, sl]), cat(v16p[:, sl], v16c[:, sl]),
                          b16, head0_b, head0)
        o16[pl.ds(r, Q_ROWS, stride=16), :] = o
        l16[pl.ds(r, Q_ROWS, stride=16), :] = l

    b4_first = bias2(1, True)
    b4 = bias2(1, False)
    for r in range(4):
        sl = slice(r * LANES, (r + 1) * LANES)
        for b in range(4):
            rows = slice(b * Q_ROWS, (b + 1) * Q_ROWS)
            if b == 0:
                k = cat(k4p[:, sl], k4c[rows, sl])
                v = cat(v4p[:, sl], v4c[rows, sl])
            else:
                both = slice((b - 1) * Q_ROWS, (b + 1) * Q_ROWS)
                k, v = k4c[both, sl], v4c[both, sl]
            o, l = _attn_unit(q4[rows, sl], k, v, b4_first if b == 0 else b4, head0_b, head0)
            o4[pl.ds(b * 4 * Q_ROWS + r, Q_ROWS, stride=4), :] = o
            l4[pl.ds(b * 4 * Q_ROWS + r, Q_ROWS, stride=4), :] = l

    b1_first = bias2(0, True)
    b1 = bias2(0, False)
    for b in range(ATT_BLOCK // Q_ROWS):
        rows = slice(b * Q_ROWS, (b + 1) * Q_ROWS)
        if b == 0:
            k = cat(k1p[...], k1c[rows, :])
            v = cat(v1p[...], v1c[rows, :])
        else:
            both = slice((b - 1) * Q_ROWS, (b + 1) * Q_ROWS)
            k, v = k1c[both, :], v1c[both, :]
        oa, la = _attn_unit(q1[rows, :], k, v, b1_first if b == 0 else b1, head0_b, head0)
        ob, lb = o4[rows, :], l4[rows, :]
        oc, lc = o16[rows, :], l16[rows, :]
        top = jnp.maximum(jnp.maximum(la, lb), lc)
        ea, eb, ec = jnp.exp(la - top), jnp.exp(lb - top), jnp.exp(lc - top)
        out_ref[rows, :] = ((ea * oa + eb * ob + ec * oc) / (ea + eb + ec)).astype(BF16)


def _attn_prompt(q1, q4, q16, k1, k4, k16, v1, v4, v16, band_bias):
    S = q1.shape[1]
    assert S % ATT_BLOCK == 0
    steps = S // ATT_BLOCK

    def cur(d):
        return pl.BlockSpec((None, ATT_BLOCK // d, d * LANES), lambda c, p: (p, c, 0))

    def prev(d):
        per_step = ATT_BLOCK // d // Q_ROWS
        return pl.BlockSpec((None, Q_ROWS, d * LANES), lambda c, p: (p, jnp.maximum(c * per_step - 1, 0), 0))

    scratch = pltpu.VMEM((ATT_BLOCK, LANES), F32)
    return pl.pallas_call(
        _attn_body,
        grid=(steps, PAIRS),
        in_specs=[cur(1), cur(4), cur(16),
                  cur(1), prev(1), cur(4), prev(4), cur(16), prev(16),
                  cur(1), prev(1), cur(4), prev(4), cur(16), prev(16),
                  pl.BlockSpec((len(DILATIONS), 2, Q_ROWS, 2 * Q_ROWS), lambda c, p: (0, p, 0, 0))],
        out_specs=pl.BlockSpec((ATT_BLOCK, LANES), lambda c, p: (c, p)),
        out_shape=jax.ShapeDtypeStruct((S, W_ATT), BF16),
        scratch_shapes=[scratch, scratch, scratch, scratch],
        compiler_params=_cparams(2),
        name="attn_prompt",
    )(q1, q4, q16, k1, k1, k4, k4, k16, k16, v1, v1, v4, v4, v16, v16, band_bias)


def _group_norm_gate(o, g):
    mu = jnp.mean(o, axis=1, keepdims=True)
    var = jnp.mean(jnp.square(o - mu), axis=1, keepdims=True)
    return g * jax.nn.sigmoid(g) * ((o - mu) * lax.rsqrt(var + GN_EPS))


def _ret_prompt_body(q_ref, k_ref, v_ref, g_ref, y_ref, st_ref, state, *, chunks):
    C = RET_CHUNK

    @pl.when(pl.program_id(0) == 0)
    def _():
        state[...] = jnp.zeros_like(state)

    n = lax.broadcasted_iota(jnp.int32, (C, 1), 0).astype(F32)
    diff = (lax.broadcasted_iota(jnp.int32, (C, C), 0) - lax.broadcasted_iota(jnp.int32, (C, C), 1)).astype(F32)
    for h in range(H_RET):
        lg = LOG_GAMMA[h]
        decay = jnp.where(diff >= 0, jnp.exp(lg * jnp.maximum(diff, 0.0)), 0.0)
        q_decay = jnp.exp(lg * (n + 1.0))
        k_decay = jnp.exp(lg * (C - 1.0 - n))
        chunk_decay = math.exp(lg * C)
        cols = slice(h * DK_RET, (h + 1) * DK_RET)
        S = state[h]
        for c in range(chunks):
            rows = slice(c * C, (c + 1) * C)
            q, k, v = q_ref[rows, cols], k_ref[rows, cols], v_ref[rows, cols]
            scores = lax.dot_general(q, k, (((1,), (1,)), ((), ())), preferred_element_type=F32) * decay
            o = (jnp.dot(scores.astype(BF16), v, preferred_element_type=F32)
                 + jnp.dot(q, S.astype(BF16), preferred_element_type=F32) * q_decay)
            k_dec_t = (k.astype(F32) * k_decay).T.astype(BF16)
            S = chunk_decay * S + jnp.dot(k_dec_t, v, preferred_element_type=F32)
            y_ref[rows, cols] = _group_norm_gate(o, g_ref[rows, cols]).astype(BF16)
        state[h] = S
    st_ref[...] = state[...]


def _ret_prompt(qr, kr, vr, g, chunks):
    S = qr.shape[0]
    rows = chunks * RET_CHUNK
    spec = pl.BlockSpec((rows, W_RET), lambda i: (i, 0))
    return pl.pallas_call(
        functools.partial(_ret_prompt_body, chunks=chunks),
        grid=(S // rows,),
        in_specs=[spec, spec, spec, spec],
        out_specs=[spec, pl.BlockSpec((H_RET, DK_RET, DV_RET), lambda i: (0, 0, 0))],
        out_shape=[jax.ShapeDtypeStruct((S, W_RET), BF16),
                   jax.ShapeDtypeStruct((H_RET, DK_RET, DV_RET), F32)],
        scratch_shapes=[pltpu.VMEM((H_RET, DK_RET, DV_RET), F32)],
        compiler_params=_cparams(1),
        name="retention_prompt",
    )(qr, kr, vr, g)


def _as_column(row_pair):
    return jnp.broadcast_to(row_pair, (LANES, LANES)).T


def _sample_mixer_body(qa_ref, kvw_ref, qr_ref, kr_ref, vr_ref, g_ref, cache_ref, st_ref, bias_ref, bias0_ref,
                       oatt_ref, yret_ref, stout_ref):
    W = cache_ref.shape[-1]
    logit_rows, self_rows = [], []
    for p in range(PAIRS):
        lanes = slice(p * LANES, (p + 1) * LANES)
        q_pair = qa_ref[:, lanes]
        k_pair = kvw_ref[:, lanes]
        q_col = _as_column(q_pair)
        for hh in range(2):
            feat = slice(hh * HD_ATT, (hh + 1) * HD_ATT)
            logit_rows.append(jnp.sum(cache_ref[0, 2 * p + hh] * q_col[feat, 0:1], axis=0, keepdims=True))
            self_rows.append(jnp.sum(q_pair[:, feat] * k_pair[:, feat], axis=1, keepdims=True))
    logits = jnp.concatenate(logit_rows, axis=0)
    s_self = jnp.concatenate(self_rows, axis=0) + bias0_ref[:, 0:1]

    probs, p_selfs, denoms, lses = [], [], [], []
    for i, (_, d) in enumerate(DILATIONS):
        lo = W - TAPS * d
        s = logits[:, lo:] + bias_ref[i, :, lo:]
        m = jnp.maximum(jnp.max(s, axis=1, keepdims=True), s_self)
        pr = jnp.exp(s - m)
        p_self = jnp.exp(s_self - m)
        l = jnp.sum(pr, axis=1, keepdims=True) + p_self
        probs.append(pr)
        p_selfs.append(p_self)
        denoms.append(l)
        lses.append(m + jnp.log(l))
    top = jnp.maximum(jnp.maximum(lses[0], lses[1]), lses[2])
    e = [jnp.exp(x - top) for x in lses]
    e_sum = e[0] + e[1] + e[2]
    coef = [e[i] / (e_sum * denoms[i]) for i in range(3)]
    lo4, lo1 = W - TAPS * 4, W - TAPS
    p16, p4, p1 = coef[2] * probs[2], coef[1] * probs[1], coef[0] * probs[0]
    p_all = jnp.concatenate([p16[:, :lo4], p16[:, lo4:lo1] + p4[:, :lo1 - lo4],
                             p16[:, lo1:] + p4[:, lo1 - lo4:] + p1], axis=1)
    c_self = coef[0] * p_selfs[0] + coef[1] * p_selfs[1] + coef[2] * p_selfs[2]
    for p in range(PAIRS):
        lanes = slice(p * LANES, (p + 1) * LANES)
        v_col = _as_column(kvw_ref[:, W_ATT + p * LANES:W_ATT + (p + 1) * LANES])
        out_cols = []
        for hh in range(2):
            h = 2 * p + hh
            feat = slice(hh * HD_ATT, (hh + 1) * HD_ATT)
            pv = jnp.sum(cache_ref[1, h] * p_all[h:h + 1, :], axis=1, keepdims=True)
            out_cols.append(pv + c_self[h:h + 1, :] * v_col[feat, 0:1])
        col = jnp.broadcast_to(jnp.concatenate(out_cols, axis=0), (LANES, LANES))
        oatt_ref[:, lanes] = col.T[0:1, :].astype(BF16)

    for h in range(H_RET):
        cols = slice(h * DK_RET, (h + 1) * DK_RET)
        gamma = math.exp(LOG_GAMMA[h])
        qh = qr_ref[:, cols]
        kh = kr_ref[:, cols].astype(F32)
        vh = vr_ref[:, cols].astype(F32)
        S = st_ref[h]
        qk = jnp.sum(qh.astype(F32) * kh, axis=1, keepdims=True)
        cross = jnp.dot(jnp.broadcast_to(qh, (16, DK_RET)), S.astype(BF16), preferred_element_type=F32)[:1]
        o = qk * vh + cross * gamma
        stout_ref[h] = gamma * S + _as_column(kh) * vh
        yret_ref[:, cols] = _group_norm_gate(o, g_ref[:, cols]).astype(BF16)


def _sample_mixer(qa, kvw, qr, kr, vr, g, cache_t, state, tap_bias):
    B, _, _, _, W = cache_t.shape
    assert W == MAX_WINDOW and PAST_LEN >= MAX_WINDOW
    row3 = lambda a: a.reshape(B, 1, a.shape[-1])
    vec = lambda width: pl.BlockSpec((None, 1, width), lambda b: (b, 0, 0))
    bias0 = jnp.broadcast_to(tap_bias[0, :, 0:1], (H_ATT, LANES))
    st_spec = pl.BlockSpec((None, H_RET, DK_RET, DV_RET), lambda b: (b, 0, 0, 0))
    oatt, yret, st = pl.pallas_call(
        _sample_mixer_body,
        grid=(B,),
        in_specs=[vec(W_ATT), vec(2 * W_ATT), vec(W_RET), vec(W_RET), vec(W_RET), vec(W_RET),
                  pl.BlockSpec((None, 2, H_ATT, HD_ATT, W), lambda b: (b, 0, 0, 0, 0)),
                  st_spec,
                  pl.BlockSpec((len(DILATIONS), H_ATT, W), lambda b: (0, 0, 0)),
                  pl.BlockSpec((H_ATT, LANES), lambda b: (0, 0))],
        out_specs=[vec(W_ATT), vec(W_RET), st_spec],
        out_shape=[jax.ShapeDtypeStruct((B, 1, W_ATT), BF16),
                   jax.ShapeDtypeStruct((B, 1, W_RET), BF16),
                   jax.ShapeDtypeStruct(state.shape, F32)],
        compiler_params=_cparams(1),
        name="sample_mixer",
    )(row3(qa), row3(kvw), row3(qr), row3(kr), row3(vr), row3(g), cache_t, state,
      _window_bias(tap_bias, W), bias0)
    return oatt.reshape(B, W_ATT), yret.reshape(B, W_RET), st


FF_CHUNK = 1024


def _layernorm(z, g, b):
    mu = jnp.mean(z, axis=1, keepdims=True)
    var = jnp.mean(jnp.square(z - mu), axis=1, keepdims=True)
    return (z - mu) * lax.rsqrt(var + LN_EPS) * g + b


def _out_ffn_body(x_ref, oatt_ref, yret_ref, wo_ref, g1_ref, b1_ref, wu_ref, wd_ref, g2_ref, b2_ref, y_ref):
    mix = (jnp.dot(oatt_ref[...], wo_ref[:W_ATT, :], preferred_element_type=F32)
           + jnp.dot(yret_ref[...], wo_ref[W_ATT:, :], preferred_element_type=F32))
    x1 = _layernorm(ALPHA * x_ref[...] + mix, g1_ref[...], b1_ref[...])
    x1b = x1.astype(BF16)
    ffn = jnp.zeros_like(x1)
    for j in range(D_FF // FF_CHUNK):
        cols = slice(j * FF_CHUNK, (j + 1) * FF_CHUNK)
        h = jnp.square(jnp.maximum(jnp.dot(x1b, wu_ref[:, cols], preferred_element_type=F32), 0.0))
        ffn = ffn + jnp.dot(h.astype(BF16), wd_ref[cols, :], preferred_element_type=F32)
    y_ref[...] = _layernorm(ALPHA * x1 + ffn, g2_ref[...], b2_ref[...])


def _out_ffn(x, oatt, yret, wo_b, g1, b1, wu_b, wd_b, g2, b2, tm, name):
    M = x.shape[0]
    row = lambda i: (i, 0)
    const = lambda shape: pl.BlockSpec(shape, lambda i: (0, 0), pipeline_mode=pl.Buffered(1))
    return pl.pallas_call(
        _out_ffn_body,
        grid=(M // tm,),
        in_specs=[pl.BlockSpec((tm, D_MODEL), row), pl.BlockSpec((tm, W_ATT), row), pl.BlockSpec((tm, W_RET), row),
                  const((W_ATT + W_RET, D_MODEL)), const((1, D_MODEL)), const((1, D_MODEL)),
                  const((D_MODEL, D_FF)), const((D_FF, D_MODEL)), const((1, D_MODEL)), const((1, D_MODEL))],
        out_specs=pl.BlockSpec((tm, D_MODEL), row),
        out_shape=jax.ShapeDtypeStruct((M, D_MODEL), F32),
        compiler_params=_cparams(1),
        name=name,
    )(x, oatt, yret, wo_b, g1, b1, wu_b, wd_b, g2, b2)


PROMPT_ROWS = 512
FFN_ROWS = 512
RET_CHUNKS_PER_STEP = 16


def kernel(x_prompt, x_sample, cache_kv_win, state_ret, w_in, rel_bias, w_out,
           ln1_g, ln1_b, w_up, w_down, ln2_g, ln2_b):
    assert x_prompt.shape[0] == 1 and x_sample.shape[1] == 1 and w_in.shape[0] == DEPTH
    S = x_prompt.shape[1]
    B = x_sample.shape[0]
    w_in_b, wo_b = w_in[0].astype(BF16), w_out[0].astype(BF16)
    wu_b, wd_b = w_up[0].astype(BF16), w_down[0].astype(BF16)
    g1, b1, g2, b2 = ln1_g[0][None], ln1_b[0][None], ln2_g[0][None], ln2_b[0][None]
    tap_bias = _tap_bias(rel_bias)

    xp = x_prompt[0]
    *qkv_views, kvw_p, qr, kr, vr, g = _inproj_prompt(xp, w_in_b, PROMPT_ROWS)
    oatt = _attn_prompt(*qkv_views, _band_bias(tap_bias))
    yret, st_p = _ret_prompt(qr, kr, vr, g, RET_CHUNKS_PER_STEP)
    y_p = _out_ffn(xp, oatt, yret, wo_b, g1, b1, wu_b, wd_b, g2, b2, FFN_ROWS, "out_ffn_prompt")

    xs = x_sample[:, 0]
    qa_s, kvw_s, qr_s, kr_s, vr_s, g_s = _inproj_sample(xs, w_in_b)
    cache_t = jnp.transpose(cache_kv_win[0], (0, 2, 3, 4, 1))
    oatt_s, yret_s, st_s = _sample_mixer(qa_s, kvw_s, qr_s, kr_s, vr_s, g_s, cache_t, state_ret[0], tap_bias)
    y_s = _out_ffn(xs, oatt_s, yret_s, wo_b, g1, b1, wu_b, wd_b, g2, b2, B, "out_ffn_sample")

    win = kvw_p.shape[1]
    kv_win_p = jnp.transpose(kvw_p.reshape(2, H_ATT, HD_ATT, win), (3, 0, 1, 2))
    return (y_p[None], y_s[:, None],
            kv_win_p[None, None], kvw_s.reshape(1, B, 1, 2, H_ATT, HD_ATT),
            st_p[None, None], st_s[None])
```

```python
import functools
import math

import jax
import jax.numpy as jnp
from jax import lax
from jax.experimental import pallas as pl
from jax.experimental.pallas import tpu as pltpu

F32 = jnp.float32
BF16 = jnp.bfloat16

D_MODEL = 1024
DEPTH = 1
PAST_LEN = 16384
W_ATT = 512
HD_ATT = 64
H_ATT = 8
DILATIONS = ((128, 1), (512, 4), (2048, 16))
TAPS = 128
MAX_WINDOW = 2048
N_BUCKETS = 32
MAX_EXACT = N_BUCKETS // 2
W_RET = 512
H_RET = 4
DK_RET = 128
DV_RET = 128
RET_CHUNK = 128
ROPE_BASE = 10000.0
D_FF = 4096
N_SPLITS = 7
ALPHA = (2.0 * DEPTH) ** 0.25
LN_EPS = 1e-5
GN_EPS = 1e-6
LOG_GAMMA = tuple(math.log(1.0 - 2.0 ** (-5.0 - h)) for h in range(H_RET))

LANES = 128
PAIRS = W_ATT // LANES
NEG = -1e30

ATT_BLOCK = 2048
Q_ROWS = 128
VMEM_LIMIT = 56 * 1024 * 1024


def _cparams(n_axes):
    return pltpu.CompilerParams(dimension_semantics=("arbitrary",) * n_axes, vmem_limit_bytes=VMEM_LIMIT)


def _rotary(u, cos, sin_signed):
    outs = []
    for h in range(H_RET):
        xh = u[:, h * LANES:(h + 1) * LANES]
        outs.append(xh * cos + pltpu.roll(xh, LANES // 2, 1) * sin_signed)
    return jnp.concatenate(outs, axis=1)


def _proj(xb, w_ref, j):
    return jnp.dot(xb, w_ref[:, j * W_ATT:(j + 1) * W_ATT], preferred_element_type=F32)


def _store_views(val, nat_ref, by4_ref, by16_ref, rows_scr, by4_scr):
    tm = val.shape[0]
    for p in range(PAIRS):
        x = val[:, p * LANES:(p + 1) * LANES]
        nat_ref[p] = x.astype(BF16)
        rows_scr[p] = x
    for p in range(PAIRS):
        for r4 in range(4):
            a = rows_scr[p, pl.ds(r4, tm // 4, stride=4), :]
            by4_ref[p, :, r4 * LANES:(r4 + 1) * LANES] = a.astype(BF16)
            by4_scr[p * 4 + r4] = a
    for p in range(PAIRS):
        for r4 in range(4):
            for rr in range(4):
                r16 = r4 + 4 * rr
                b = by4_scr[p * 4 + r4, pl.ds(rr, tm // 16, stride=4), :]
                by16_ref[p, :, r16 * LANES:(r16 + 1) * LANES] = b.astype(BF16)


def _retention_qkvg(xb, w_ref, cos, sin):
    return (_rotary(_proj(xb, w_ref, 3), cos, sin), _rotary(_proj(xb, w_ref, 4), cos, sin) * DK_RET ** -0.5,
            _proj(xb, w_ref, 5), _proj(xb, w_ref, 6))


def _inproj_prompt_body(x_ref, w_ref, cos_row_ref, sin_row_ref, cos_step_ref, sin_step_ref,
                        q1_ref, q4_ref, q16_ref, k1_ref, k4_ref, k16_ref, v1_ref, v4_ref, v16_ref, kvw_ref,
                        yret_ref, st_ref, q_rows, q_by4, k_rows, k_by4, v_rows, v_by4, state, *, win_first_step):
    @pl.when(pl.program_id(0) == 0)
    def _():
        state[...] = jnp.zeros_like(state)

    xb = x_ref[...].astype(BF16)
    _store_views(_proj(xb, w_ref, 0) * HD_ATT ** -0.5, q1_ref, q4_ref, q16_ref, q_rows, q_by4)
    k = _proj(xb, w_ref, 1)
    v = _proj(xb, w_ref, 2)
    _store_views(k, k1_ref, k4_ref, k16_ref, k_rows, k_by4)
    _store_views(v, v1_ref, v4_ref, v16_ref, v_rows, v_by4)

    @pl.when(pl.program_id(0) >= win_first_step)
    def _():
        kvw_ref[:W_ATT, :] = k.T
        kvw_ref[W_ATT:, :] = v.T

    step = pl.ds(pl.program_id(0), 1)
    ca, sa = cos_step_ref[step, :], sin_step_ref[step, :]
    cb, sb = cos_row_ref[...], sin_row_ref[...]
    sign = jnp.where(lax.broadcasted_iota(jnp.int32, (1, LANES), 1) < LANES // 2, -1.0, 1.0)
    qr, kr, vr, g = _retention_qkvg(xb, w_ref, ca * cb - sa * sb, (sa * cb + ca * sb) * sign)
    _retention_tile(qr, kr, vr, g, yret_ref, state)
    st_ref[...] = state[...]


def _inproj_sample_body(x_ref, w_ref, cos_ref, sin_ref, qa_ref, kvw_ref, qr_ref, kr_ref, vr_ref, g_ref):
    xb = x_ref[...].astype(BF16)
    qa_ref[...] = _proj(xb, w_ref, 0) * HD_ATT ** -0.5
    kvw_ref[:, :W_ATT] = _proj(xb, w_ref, 1)
    kvw_ref[:, W_ATT:] = _proj(xb, w_ref, 2)
    qr, kr, vr, g = _retention_qkvg(xb, w_ref, cos_ref[...], sin_ref[...])
    qr_ref[...] = qr.astype(BF16)
    kr_ref[...] = kr.astype(BF16)
    vr_ref[...] = vr.astype(BF16)
    g_ref[...] = g


def _rope_tables(pos):
    half = DK_RET // 2
    inv_freq = 1.0 / (ROPE_BASE ** jnp.linspace(0.0, 1.0, half, dtype=F32))
    ang = pos.astype(F32)[:, None] * inv_freq[None, :]
    cos, sin = jnp.cos(ang), jnp.sin(ang)
    return jnp.concatenate([cos, cos], axis=1), jnp.concatenate([sin, sin], axis=1)


def _inproj_prompt(x, w_in_b, tm):
    S = x.shape[0]
    win = min(MAX_WINDOW, S)
    steps = S // tm
    win_first_step = steps - win // tm
    cos, sin = _rope_tables(jnp.concatenate([jnp.arange(tm, dtype=jnp.int32),
                                             jnp.arange(steps, dtype=jnp.int32) * tm]))
    const = lambda i: (0, 0)
    row = lambda i: (i, 0)
    view_specs = [pl.BlockSpec((PAIRS, tm // d, d * LANES), lambda i: (0, i, 0)) for d in (1, 4, 16)] * 3
    view_shapes = [jax.ShapeDtypeStruct((PAIRS, S // d, d * LANES), BF16) for d in (1, 4, 16)] * 3
    state_shape = (H_RET, DK_RET, DV_RET)
    gather_scratch = [pltpu.VMEM((PAIRS, tm, LANES), F32), pltpu.VMEM((4 * PAIRS, tm // 4, LANES), F32)]
    return pl.pallas_call(
        functools.partial(_inproj_prompt_body, win_first_step=win_first_step),
        grid=(steps,),
        in_specs=[pl.BlockSpec((tm, D_MODEL), row),
                  pl.BlockSpec((D_MODEL, N_SPLITS * W_ATT), lambda i: (0, 0)),
                  pl.BlockSpec((tm, LANES), const), pl.BlockSpec((tm, LANES), const),
                  pl.BlockSpec((steps, LANES), const), pl.BlockSpec((steps, LANES), const)],
        out_specs=view_specs + [
            pl.BlockSpec((2 * W_ATT, tm), lambda i: (0, jnp.maximum(i - win_first_step, 0))),
            pl.BlockSpec((tm, W_RET), row),
            pl.BlockSpec(state_shape, lambda i: (0, 0, 0))],
        out_shape=view_shapes + [
            jax.ShapeDtypeStruct((2 * W_ATT, win), F32),
            jax.ShapeDtypeStruct((S, W_RET), BF16),
            jax.ShapeDtypeStruct(state_shape, F32)],
        scratch_shapes=gather_scratch * 3 + [pltpu.VMEM(state_shape, F32)],
        compiler_params=_cparams(1),
        name="inproj_retention_prompt",
    )(x, w_in_b, cos[:tm], sin[:tm], cos[tm:], sin[tm:])


def _inproj_sample(x, w_in_b):
    B = x.shape[0]
    cos, sin = _rope_tables(jnp.full((B,), PAST_LEN, dtype=jnp.int32))
    sin = sin * jnp.where(jnp.arange(LANES) < LANES // 2, -1.0, 1.0)
    full = lambda shape: pl.BlockSpec(shape, lambda i: (0, 0))
    return pl.pallas_call(
        _inproj_sample_body,
        grid=(1,),
        in_specs=[full((B, D_MODEL)), full((D_MODEL, N_SPLITS * W_ATT)), full((B, LANES)), full((B, LANES))],
        out_specs=[full((B, W_ATT)), full((B, 2 * W_ATT)), full((B, W_RET)), full((B, W_RET)),
                   full((B, W_RET)), full((B, W_RET))],
        out_shape=[jax.ShapeDtypeStruct((B, W_ATT), F32),
                   jax.ShapeDtypeStruct((B, 2 * W_ATT), F32),
                   jax.ShapeDtypeStruct((B, W_RET), BF16),
                   jax.ShapeDtypeStruct((B, W_RET), BF16),
                   jax.ShapeDtypeStruct((B, W_RET), BF16),
                   jax.ShapeDtypeStruct((B, W_RET), F32)],
        compiler_params=_cparams(1),
        name="inproj_sample",
    )(x, w_in_b, cos, sin)


def _t5_bucket(dist):
    is_small = dist < MAX_EXACT
    d_f = jnp.maximum(dist, 1).astype(F32)
    large = MAX_EXACT + (jnp.log(d_f / MAX_EXACT) / math.log(MAX_WINDOW / MAX_EXACT)
                         * (N_BUCKETS - MAX_EXACT)).astype(jnp.int32)
    large = jnp.minimum(large, N_BUCKETS - 1)
    return jnp.where(is_small, dist, large)


def _tap_bias(rel_bias):
    taps = jnp.arange(TAPS + 1)
    return jnp.stack([rel_bias[_t5_bucket(taps * d)].astype(F32).T for _, d in DILATIONS])


def _band_bias(tap_bias):
    n = Q_ROWS + 2 * Q_ROWS
    w = jnp.concatenate([tap_bias[..., ::-1], jnp.full(tap_bias.shape[:-1] + (n - TAPS - 1,), NEG, F32)], axis=-1)
    skew = jnp.tile(w, Q_ROWS)[..., :Q_ROWS * (n - 1)].reshape(tap_bias.shape[:-1] + (Q_ROWS, n - 1))
    return skew[..., :2 * Q_ROWS]


def _window_bias(tap_bias, window):
    rows = []
    for i, (_, d) in enumerate(DILATIONS):
        taps_rev = tap_bias[i, :, TAPS:0:-1]
        seg = jnp.concatenate([taps_rev[..., None], jnp.full((H_ATT, TAPS, d - 1), NEG, F32)], axis=-1)
        rows.append(jnp.concatenate([jnp.full((H_ATT, window - TAPS * d), NEG, F32),
                                     seg.reshape(H_ATT, TAPS * d)], axis=-1))
    return jnp.stack(rows)


def _attn_unit(q, k, v, bias2, head0_b, head0):
    qs = jnp.concatenate([q * head0_b, q * (1 - head0_b)], axis=0)
    s = lax.dot_general(qs, k, (((1,), (1,)), ((), ())), preferred_element_type=F32) + bias2
    m = jnp.max(s, axis=1, keepdims=True)
    p = jnp.exp(s - m)
    l = jnp.sum(p, axis=1, keepdims=True)
    o = jnp.dot(p.astype(BF16), v, preferred_element_type=F32) / l
    lse = m + jnp.log(l)
    o_pair = jnp.where(head0, o[:Q_ROWS], o[Q_ROWS:])
    lse_pair = jnp.where(head0, jnp.broadcast_to(lse[:Q_ROWS], (Q_ROWS, LANES)),
                         jnp.broadcast_to(lse[Q_ROWS:], (Q_ROWS, LANES)))
    return o_pair, lse_pair


def _attn_body(q1, q4, q16, k1c, k1p, k4c, k4p, k16c, k16p, v1c, v1p, v4c, v4p, v16c, v16p, bias_ref,
               out_ref, o16, l16, o4, l4):
    first = pl.program_id(0) == 0
    lane = lax.broadcasted_iota(jnp.int32, (Q_ROWS, LANES), 1)
    head0 = lane < HD_ATT
    head0_b = jnp.where(head0, 1.0, 0.0).astype(BF16)
    col = lax.broadcasted_iota(jnp.int32, (2 * Q_ROWS, 2 * Q_ROWS), 1)
    no_prev = jnp.where(jnp.logical_and(first, col < Q_ROWS), NEG, 0.0)

    def bias2(branch, masked_prev):
        b = jnp.concatenate([bias_ref[branch, 0], bias_ref[branch, 1]], axis=0)
        return b + no_prev if masked_prev else b

    def cat(a, b):
        return jnp.concatenate([a, b], axis=0)

    b16 = bias2(2, True)
    for r in range(16):
        sl = slice(r * LANES, (r + 1) * LANES)
        o, l = _attn_unit(q16[:, sl], cat(k16p[:, sl], k16c[:, sl]), cat(v16p[:, sl], v16c[:, sl]),
                          b16, head0_b, head0)
        o16[pl.ds(r, Q_ROWS, stride=16), :] = o
        l16[pl.ds(r, Q_ROWS, stride=16), :] = l

    b4_first = bias2(1, True)
    b4 = bias2(1, False)
    for r in range(4):
        sl = slice(r * LANES, (r + 1) * LANES)
        for b in range(4):
            rows = slice(b * Q_ROWS, (b + 1) * Q_ROWS)
            if b == 0:
                k = cat(k4p[:, sl], k4c[rows, sl])
                v = cat(v4p[:, sl], v4c[rows, sl])
            else:
                both = slice((b - 1) * Q_ROWS, (b + 1) * Q_ROWS)
                k, v = k4c[both, sl], v4c[both, sl]
            o, l = _attn_unit(q4[rows, sl], k, v, b4_first if b == 0 else b4, head0_b, head0)
            o4[pl.ds(b * 4 * Q_ROWS + r, Q_ROWS, stride=4), :] = o
            l4[pl.ds(b * 4 * Q_ROWS + r, Q_ROWS, stride=4), :] = l

    b1_first = bias2(0, True)
    b1 = bias2(0, False)
    for b in range(ATT_BLOCK // Q_ROWS):
        rows = slice(b * Q_ROWS, (b + 1) * Q_ROWS)
        if b == 0:
            k = cat(k1p[...], k1c[rows, :])
            v = cat(v1p[...], v1c[rows, :])
        else:
            both = slice((b - 1) * Q_ROWS, (b + 1) * Q_ROWS)
            k, v = k1c[both, :], v1c[both, :]
        oa, la = _attn_unit(q1[rows, :], k, v, b1_first if b == 0 else b1, head0_b, head0)
        ob, lb = o4[rows, :], l4[rows, :]
        oc, lc = o16[rows, :], l16[rows, :]
        top = jnp.maximum(jnp.maximum(la, lb), lc)
        ea, eb, ec = jnp.exp(la - top), jnp.exp(lb - top), jnp.exp(lc - top)
        out_ref[rows, :] = ((ea * oa + eb * ob + ec * oc) / (ea + eb + ec)).astype(BF16)


def _attn_prompt(q1, q4, q16, k1, k4, k16, v1, v4, v16, band_bias):
    S = q1.shape[1]
    assert S % ATT_BLOCK == 0
    steps = S // ATT_BLOCK

    def cur(d):
        return pl.BlockSpec((None, ATT_BLOCK // d, d * LANES), lambda c, p: (p, c, 0))

    def prev(d):
        per_step = ATT_BLOCK // d // Q_ROWS
        return pl.BlockSpec((None, Q_ROWS, d * LANES), lambda c, p: (p, jnp.maximum(c * per_step - 1, 0), 0))

    scratch = pltpu.VMEM((ATT_BLOCK, LANES), F32)
    return pl.pallas_call(
        _attn_body,
        grid=(steps, PAIRS),
        in_specs=[cur(1), cur(4), cur(16),
                  cur(1), prev(1), cur(4), prev(4), cur(16), prev(16),
                  cur(1), prev(1), cur(4), prev(4), cur(16), prev(16),
                  pl.BlockSpec((len(DILATIONS), 2, Q_ROWS, 2 * Q_ROWS), lambda c, p: (0, p, 0, 0))],
        out_specs=pl.BlockSpec((ATT_BLOCK, LANES), lambda c, p: (c, p)),
        out_shape=jax.ShapeDtypeStruct((S, W_ATT), BF16),
        scratch_shapes=[scratch, scratch, scratch, scratch],
        compiler_params=_cparams(2),
        name="attn_prompt",
    )(q1, q4, q16, k1, k1, k4, k4, k16, k16, v1, v1, v4, v4, v16, v16, band_bias)


def _group_norm_gate(o, g):
    mu = jnp.mean(o, axis=1, keepdims=True)
    var = jnp.mean(jnp.square(o - mu), axis=1, keepdims=True)
    return g * jax.nn.sigmoid(g) * ((o - mu) * lax.rsqrt(var + GN_EPS))


def _retention_tile(q, k, v, g, y_ref, state):
    C = RET_CHUNK
    n = lax.broadcasted_iota(jnp.int32, (C, 1), 0).astype(F32)
    diff = (lax.broadcasted_iota(jnp.int32, (C, C), 0) - lax.broadcasted_iota(jnp.int32, (C, C), 1)).astype(F32)
    for h in range(H_RET):
        lg = LOG_GAMMA[h]
        decay = jnp.where(diff >= 0, jnp.exp(lg * jnp.maximum(diff, 0.0)), 0.0)
        q_decay = jnp.exp(lg * (n + 1.0))
        k_decay = jnp.exp(lg * (C - 1.0 - n))
        chunk_decay = math.exp(lg * C)
        cols = slice(h * DK_RET, (h + 1) * DK_RET)
        S = state[h]
        for c in range(q.shape[0] // C):
            rows = slice(c * C, (c + 1) * C)
            qb, kf, vb = q[rows, cols].astype(BF16), k[rows, cols], v[rows, cols].astype(BF16)
            scores = lax.dot_general(qb, kf.astype(BF16), (((1,), (1,)), ((), ())), preferred_element_type=F32) * decay
            o = (jnp.dot(scores.astype(BF16), vb, preferred_element_type=F32)
                 + jnp.dot(qb, S.astype(BF16), preferred_element_type=F32) * q_decay)
            k_dec_t = (kf * k_decay).T.astype(BF16)
            S = chunk_decay * S + jnp.dot(k_dec_t, vb, preferred_element_type=F32)
            y_ref[rows, cols] = _group_norm_gate(o, g[rows, cols]).astype(BF16)
        state[h] = S


def _as_column(row_pair):
    return jnp.broadcast_to(row_pair, (LANES, LANES)).T


def _sample_mixer_body(qa_ref, kvw_ref, qr_ref, kr_ref, vr_ref, g_ref, cache_ref, st_ref, bias_ref, bias0_ref,
                       oatt_ref, yret_ref, stout_ref):
    W = cache_ref.shape[-1]
    logit_rows, self_rows = [], []
    for p in range(PAIRS):
        lanes = slice(p * LANES, (p + 1) * LANES)
        q_pair = qa_ref[:, lanes]
        k_pair = kvw_ref[:, lanes]
        q_col = _as_column(q_pair)
        for hh in range(2):
            feat = slice(hh * HD_ATT, (hh + 1) * HD_ATT)
            logit_rows.append(jnp.sum(cache_ref[0, 2 * p + hh] * q_col[feat, 0:1], axis=0, keepdims=True))
            self_rows.append(jnp.sum(q_pair[:, feat] * k_pair[:, feat], axis=1, keepdims=True))
    logits = jnp.concatenate(logit_rows, axis=0)
    s_self = jnp.concatenate(self_rows, axis=0) + bias0_ref[:, 0:1]

    probs, p_selfs, denoms, lses = [], [], [], []
    for i, (_, d) in enumerate(DILATIONS):
        lo = W - TAPS * d
        s = logits[:, lo:] + bias_ref[i, :, lo:]
        m = jnp.maximum(jnp.max(s, axis=1, keepdims=True), s_self)
        pr = jnp.exp(s - m)
        p_self = jnp.exp(s_self - m)
        l = jnp.sum(pr, axis=1, keepdims=True) + p_self
        probs.append(pr)
        p_selfs.append(p_self)
        denoms.append(l)
        lses.append(m + jnp.log(l))
    top = jnp.maximum(jnp.maximum(lses[0], lses[1]), lses[2])
    e = [jnp.exp(x - top) for x in lses]
    e_sum = e[0] + e[1] + e[2]
    coef = [e[i] / (e_sum * denoms[i]) for i in range(3)]
    lo4, lo1 = W - TAPS * 4, W - TAPS
    p16, p4, p1 = coef[2] * probs[2], coef[1] * probs[1], coef[0] * probs[0]
    p_all = jnp.concatenate([p16[:, :lo4], p16[:, lo4:lo1] + p4[:, :lo1 - lo4],
                             p16[:, lo1:] + p4[:, lo1 - lo4:] + p1], axis=1)
    c_self = coef[0] * p_selfs[0] + coef[1] * p_selfs[1] + coef[2] * p_selfs[2]
    for p in range(PAIRS):
        lanes = slice(p * LANES, (p + 1) * LANES)
        v_col = _as_column(kvw_ref[:, W_ATT + p * LANES:W_ATT + (p + 1) * LANES])
        out_cols = []
        for hh in range(2):
            h = 2 * p + hh
            feat = slice(hh * HD_ATT, (hh + 1) * HD_ATT)
            pv = jnp.sum(cache_ref[1, h] * p_all[h:h + 1, :], axis=1, keepdims=True)
            out_cols.append(pv + c_self[h:h + 1, :] * v_col[feat, 0:1])
        col = jnp.broadcast_to(jnp.concatenate(out_cols, axis=0), (LANES, LANES))
        oatt_ref[:, lanes] = col.T[0:1, :].astype(BF16)

    for h in range(H_RET):
        cols = slice(h * DK_RET, (h + 1) * DK_RET)
        gamma = math.exp(LOG_GAMMA[h])
        qh = qr_ref[:, cols]
        kh = kr_ref[:, cols].astype(F32)
        vh = vr_ref[:, cols].astype(F32)
        S = st_ref[h]
        qk = jnp.sum(qh.astype(F32) * kh, axis=1, keepdims=True)
        cross = jnp.dot(jnp.broadcast_to(qh, (16, DK_RET)), S.astype(BF16), preferred_element_type=F32)[:1]
        o = qk * vh + cross * gamma
        stout_ref[h] = gamma * S + _as_column(kh) * vh
        yret_ref[:, cols] = _group_norm_gate(o, g_ref[:, cols]).astype(BF16)


def _sample_mixer(qa, kvw, qr, kr, vr, g, cache_t, state, tap_bias):
    B, _, _, _, W = cache_t.shape
    assert W == MAX_WINDOW and PAST_LEN >= MAX_WINDOW
    row3 = lambda a: a.reshape(B, 1, a.shape[-1])
    vec = lambda width: pl.BlockSpec((None, 1, width), lambda b: (b, 0, 0))
    bias0 = jnp.broadcast_to(tap_bias[0, :, 0:1], (H_ATT, LANES))
    st_spec = pl.BlockSpec((None, H_RET, DK_RET, DV_RET), lambda b: (b, 0, 0, 0))
    oatt, yret, st = pl.pallas_call(
        _sample_mixer_body,
        grid=(B,),
        in_specs=[vec(W_ATT), vec(2 * W_ATT), vec(W_RET), vec(W_RET), vec(W_RET), vec(W_RET),
                  pl.BlockSpec((None, 2, H_ATT, HD_ATT, W), lambda b: (b, 0, 0, 0, 0)),
                  st_spec,
                  pl.BlockSpec((len(DILATIONS), H_ATT, W), lambda b: (0, 0, 0)),
                  pl.BlockSpec((H_ATT, LANES), lambda b: (0, 0))],
        out_specs=[vec(W_ATT), vec(W_RET), st_spec],
        out_shape=[jax.ShapeDtypeStruct((B, 1, W_ATT), BF16),
                   jax.ShapeDtypeStruct((B, 1, W_RET), BF16),
                   jax.ShapeDtypeStruct(state.shape, F32)],
        compiler_params=_cparams(1),
        name="sample_mixer",
    )(row3(qa), row3(kvw), row3(qr), row3(kr), row3(vr), row3(g), cache_t, state,
      _window_bias(tap_bias, W), bias0)
    return oatt.reshape(B, W_ATT), yret.reshape(B, W_RET), st


FF_CHUNK = 1024


def _layernorm(z, g, b):
    mu = jnp.mean(z, axis=1, keepdims=True)
    var = jnp.mean(jnp.square(z - mu), axis=1, keepdims=True)
    return (z - mu) * lax.rsqrt(var + LN_EPS) * g + b


def _out_ffn_body(x_ref, oatt_ref, yret_ref, wo_ref, g1_ref, b1_ref, wu_ref, wd_ref, g2_ref, b2_ref, y_ref):
    mix = (jnp.dot(oatt_ref[...], wo_ref[:W_ATT, :], preferred_element_type=F32)
           + jnp.dot(yret_ref[...], wo_ref[W_ATT:, :], preferred_element_type=F32))
    x1 = _layernorm(ALPHA * x_ref[...] + mix, g1_ref[...], b1_ref[...])
    x1b = x1.astype(BF16)
    ffn = jnp.zeros_like(x1)
    for j in range(D_FF // FF_CHUNK):
        cols = slice(j * FF_CHUNK, (j + 1) * FF_CHUNK)
        h = jnp.square(jnp.maximum(jnp.dot(x1b, wu_ref[:, cols], preferred_element_type=F32), 0.0))
        ffn = ffn + jnp.dot(h.astype(BF16), wd_ref[cols, :], preferred_element_type=F32)
    y_ref[...] = _layernorm(ALPHA * x1 + ffn, g2_ref[...], b2_ref[...])


def _out_ffn(x, oatt, yret, wo_b, g1, b1, wu_b, wd_b, g2, b2, tm, name):
    M = x.shape[0]
    row = lambda i: (i, 0)
    const = lambda shape: pl.BlockSpec(shape, lambda i: (0, 0), pipeline_mode=pl.Buffered(1))
    return pl.pallas_call(
        _out_ffn_body,
        grid=(M // tm,),
        in_specs=[pl.BlockSpec((tm, D_MODEL), row), pl.BlockSpec((tm, W_ATT), row), pl.BlockSpec((tm, W_RET), row),
                  const((W_ATT + W_RET, D_MODEL)), const((1, D_MODEL)), const((1, D_MODEL)),
                  const((D_MODEL, D_FF)), const((D_FF, D_MODEL)), const((1, D_MODEL)), const((1, D_MODEL))],
        out_specs=pl.BlockSpec((tm, D_MODEL), row),
        out_shape=jax.ShapeDtypeStruct((M, D_MODEL), F32),
        compiler_params=_cparams(1),
        name=name,
    )(x, oatt, yret, wo_b, g1, b1, wu_b, wd_b, g2, b2)


PROMPT_ROWS = 512
FFN_ROWS = 512


def kernel(x_prompt, x_sample, cache_kv_win, state_ret, w_in, rel_bias, w_out,
           ln1_g, ln1_b, w_up, w_down, ln2_g, ln2_b):
    assert x_prompt.shape[0] == 1 and x_sample.shape[1] == 1 and w_in.shape[0] == DEPTH
    S = x_prompt.shape[1]
    B = x_sample.shape[0]
    w_in_b, wo_b = w_in[0].astype(BF16), w_out[0].astype(BF16)
    wu_b, wd_b = w_up[0].astype(BF16), w_down[0].astype(BF16)
    g1, b1, g2, b2 = ln1_g[0][None], ln1_b[0][None], ln2_g[0][None], ln2_b[0][None]
    tap_bias = _tap_bias(rel_bias)

    xp = x_prompt[0]
    *qkv_views, kvw_p, yret, st_p = _inproj_prompt(xp, w_in_b, PROMPT_ROWS)
    oatt = _attn_prompt(*qkv_views, _band_bias(tap_bias))
    y_p = _out_ffn(xp, oatt, yret, wo_b, g1, b1, wu_b, wd_b, g2, b2, FFN_ROWS, "out_ffn_prompt")

    xs = x_sample[:, 0]
    qa_s, kvw_s, qr_s, kr_s, vr_s, g_s = _inproj_sample(xs, w_in_b)
    cache_t = jnp.transpose(cache_kv_win[0], (0, 2, 3, 4, 1))
    oatt_s, yret_s, st_s = _sample_mixer(qa_s, kvw_s, qr_s, kr_s, vr_s, g_s, cache_t, state_ret[0], tap_bias)
    y_s = _out_ffn(xs, oatt_s, yret_s, wo_b, g1, b1, wu_b, wd_b, g2, b2, B, "out_ffn_sample")

    win = kvw_p.shape[1]
    kv_win_p = jnp.transpose(kvw_p.reshape(2, H_ATT, HD_ATT, win), (3, 0, 1, 2))
    return (y_p[None], y_s[:, None],
            kv_win_p[None, None], kvw_s.reshape(1, B, 1, 2, H_ATT, HD_ATT),
            st_p[None, None], st_s[None])
```

```python
import functools
import math

import jax
import jax.numpy as jnp
from jax import lax
from jax.experimental import pallas as pl
from jax.experimental.pallas import tpu as pltpu

F32 = jnp.float32
BF16 = jnp.bfloat16

D_MODEL = 1024
DEPTH = 1
PAST_LEN = 16384
W_ATT = 512
HD_ATT = 64
H_ATT = 8
DILATIONS = ((128, 1), (512, 4), (2048, 16))
TAPS = 128
MAX_WINDOW = 2048
N_BUCKETS = 32
MAX_EXACT = N_BUCKETS // 2
W_RET = 512
H_RET = 4
DK_RET = 128
DV_RET = 128
RET_CHUNK = 128
ROPE_BASE = 10000.0
D_FF = 4096
N_SPLITS = 7
ALPHA = (2.0 * DEPTH) ** 0.25
LN_EPS = 1e-5
GN_EPS = 1e-6
LOG_GAMMA = tuple(math.log(1.0 - 2.0 ** (-5.0 - h)) for h in range(H_RET))

LANES = 128
PAIRS = W_ATT // LANES
NEG = -1e30

ATT_BLOCK = 2048
Q_ROWS = 128
PROJ_BLOCK = 256
VMEM_LIMIT = 56 * 1024 * 1024


def _cparams(n_axes):
    return pltpu.CompilerParams(dimension_semantics=("arbitrary",) * n_axes, vmem_limit_bytes=VMEM_LIMIT)


def _rotary(u, cos, sin_signed):
    outs = []
    for h in range(H_RET):
        xh = u[:, h * LANES:(h + 1) * LANES]
        outs.append(xh * cos + pltpu.roll(xh, LANES // 2, 1) * sin_signed)
    return jnp.concatenate(outs, axis=1)


def _proj(xb, w_ref, j):
    return jnp.dot(xb, w_ref[:, j * W_ATT:(j + 1) * W_ATT], preferred_element_type=F32)


def _store_rows(val, rows, nat_ref, rows_scr):
    for p in range(PAIRS):
        x = val[:, p * LANES:(p + 1) * LANES]
        nat_ref[p, rows, :] = x.astype(BF16)
        rows_scr[p, rows, :] = x


def _store_gathered(by4_ref, by16_ref, rows_scr, by4_scr):
    tm = rows_scr.shape[1]
    for p in range(PAIRS):
        for r4 in range(4):
            a = rows_scr[p, pl.ds(r4, tm // 4, stride=4), :]
            by4_ref[p, :, r4 * LANES:(r4 + 1) * LANES] = a.astype(BF16)
            by4_scr[p * 4 + r4] = a
    for p in range(PAIRS):
        for r4 in range(4):
            for rr in range(4):
                r16 = r4 + 4 * rr
                b = by4_scr[p * 4 + r4, pl.ds(rr, tm // 16, stride=4), :]
                by16_ref[p, :, r16 * LANES:(r16 + 1) * LANES] = b.astype(BF16)


def _retention_qkvg(xb, w_ref, cos, sin):
    return (_rotary(_proj(xb, w_ref, 3), cos, sin), _rotary(_proj(xb, w_ref, 4), cos, sin) * DK_RET ** -0.5,
            _proj(xb, w_ref, 5), _proj(xb, w_ref, 6))


def _inproj_prompt_body(x_ref, w_ref, cos_row_ref, sin_row_ref, cos_step_ref, sin_step_ref,
                        q1_ref, q4_ref, q16_ref, k1_ref, k4_ref, k16_ref, v1_ref, v4_ref, v16_ref, kvw_ref,
                        yret_ref, st_ref, q_rows, q_by4, k_rows, k_by4, v_rows, v_by4, state, *, win_first_step):
    @pl.when(pl.program_id(0) == 0)
    def _():
        state[...] = jnp.zeros_like(state)

    xb = x_ref[...].astype(BF16)
    tm = xb.shape[0]
    half = tm // 2

    step = pl.ds(pl.program_id(0), 1)
    ca, sa = cos_step_ref[step, :], sin_step_ref[step, :]
    cb, sb = cos_row_ref[...], sin_row_ref[...]
    sign = jnp.where(lax.broadcasted_iota(jnp.int32, (1, LANES), 1) < LANES // 2, -1.0, 1.0)
    cos, sin = ca * cb - sa * sb, (sa * cb + ca * sb) * sign

    def retention_proj(j, rows):
        u = _proj(xb[rows], w_ref, 3 + j)
        if j < 2:
            u = _rotary(u, cos[rows], sin[rows])
        return u * DK_RET ** -0.5 if j == 1 else u

    attn_out = ((q1_ref, q4_ref, q16_ref, q_rows, q_by4), (k1_ref, k4_ref, k16_ref, k_rows, k_by4),
                (v1_ref, v4_ref, v16_ref, v_rows, v_by4))

    def attention_proj(j, rows, last):
        nat_ref, by4_ref, by16_ref, rows_scr, by4_scr = attn_out[j]
        val = _proj(xb[rows], w_ref, j)
        _store_rows(val * HD_ATT ** -0.5 if j == 0 else val, rows, nat_ref, rows_scr)
        if last:
            _store_gathered(by4_ref, by16_ref, rows_scr, by4_scr)

    halves = (slice(0, half), slice(half, tm))
    ret_in = [[retention_proj(j, halves[0]) for j in range(4)], [None] * 4]
    fills = [[("ret", 0), ("ret", 1), ("ret", 2), ("ret", 3), ("att", 0, 0)],
             [("att", 0, 1), ("att", 1, 0), ("att", 1, 1), ("att", 2, 0), ("att", 2, 1)]]
    decays = [_retention_decays(h) for h in range(H_RET)]
    S = [state[h] for h in range(H_RET)]
    units_per_half = (half // RET_CHUNK) * H_RET
    for ph in range(2):
        qr, kr, vr, g = ret_in[ph]
        emit_after = {((i + 1) * units_per_half) // len(fills[ph]) - 1: f for i, f in enumerate(fills[ph])}
        unit = 0
        for c in range(half // RET_CHUNK):
            rows = slice(c * RET_CHUNK, (c + 1) * RET_CHUNK)
            out_rows = slice(ph * half + c * RET_CHUNK, ph * half + (c + 1) * RET_CHUNK)
            for h in range(H_RET):
                cols = slice(h * DK_RET, (h + 1) * DK_RET)
                y, S[h] = _retention_chunk(qr[rows, cols], kr[rows, cols], vr[rows, cols], g[rows, cols],
                                           S[h], decays[h])
                yret_ref[out_rows, cols] = y.astype(BF16)
                fill = emit_after.get(unit)
                if fill is not None and fill[0] == "ret":
                    ret_in[1][fill[1]] = retention_proj(fill[1], halves[1])
                elif fill is not None:
                    attention_proj(fill[1], halves[fill[2]], last=fill[2] == 1)
                unit += 1
    for h in range(H_RET):
        state[h] = S[h]
        st_ref[h] = S[h]

    @pl.when(pl.program_id(0) >= win_first_step)
    def _():
        for p in range(PAIRS):
            kvw_ref[p * LANES:(p + 1) * LANES, :] = k_rows[p].T
            kvw_ref[W_ATT + p * LANES:W_ATT + (p + 1) * LANES, :] = v_rows[p].T


def _inproj_sample_body(x_ref, w_ref, cos_ref, sin_ref, qa_ref, kvw_ref, qr_ref, kr_ref, vr_ref, g_ref):
    xb = x_ref[...].astype(BF16)
    qa_ref[...] = _proj(xb, w_ref, 0) * HD_ATT ** -0.5
    kvw_ref[:, :W_ATT] = _proj(xb, w_ref, 1)
    kvw_ref[:, W_ATT:] = _proj(xb, w_ref, 2)
    qr, kr, vr, g = _retention_qkvg(xb, w_ref, cos_ref[...], sin_ref[...])
    qr_ref[...] = qr.astype(BF16)
    kr_ref[...] = kr.astype(BF16)
    vr_ref[...] = vr.astype(BF16)
    g_ref[...] = g


def _rope_tables(pos):
    half = DK_RET // 2
    inv_freq = 1.0 / (ROPE_BASE ** jnp.linspace(0.0, 1.0, half, dtype=F32))
    ang = pos.astype(F32)[:, None] * inv_freq[None, :]
    cos, sin = jnp.cos(ang), jnp.sin(ang)
    return jnp.concatenate([cos, cos], axis=1), jnp.concatenate([sin, sin], axis=1)


def _inproj_prompt(x, w_in_b, tm):
    S = x.shape[0]
    win = min(MAX_WINDOW, S)
    steps = S // tm
    win_first_step = steps - win // tm
    cos, sin = _rope_tables(jnp.concatenate([jnp.arange(tm, dtype=jnp.int32),
                                             jnp.arange(steps, dtype=jnp.int32) * tm]))
    const = lambda i: (0, 0)
    row = lambda i: (i, 0)
    view_specs = [pl.BlockSpec((PAIRS, tm // d, d * LANES), lambda i: (0, i, 0)) for d in (1, 4, 16)] * 3
    view_shapes = [jax.ShapeDtypeStruct((PAIRS, S // d, d * LANES), BF16) for d in (1, 4, 16)] * 3
    state_shape = (H_RET, DK_RET, DV_RET)
    gather_scratch = [pltpu.VMEM((PAIRS, tm, LANES), F32), pltpu.VMEM((4 * PAIRS, tm // 4, LANES), F32)]
    return pl.pallas_call(
        functools.partial(_inproj_prompt_body, win_first_step=win_first_step),
        grid=(steps,),
        in_specs=[pl.BlockSpec((tm, D_MODEL), row),
                  pl.BlockSpec((D_MODEL, N_SPLITS * W_ATT), lambda i: (0, 0)),
                  pl.BlockSpec((tm, LANES), const), pl.BlockSpec((tm, LANES), const),
                  pl.BlockSpec((steps, LANES), const), pl.BlockSpec((steps, LANES), const)],
        out_specs=view_specs + [
            pl.BlockSpec((2 * W_ATT, tm), lambda i: (0, jnp.maximum(i - win_first_step, 0))),
            pl.BlockSpec((tm, W_RET), row),
            pl.BlockSpec(state_shape, lambda i: (0, 0, 0))],
        out_shape=view_shapes + [
            jax.ShapeDtypeStruct((2 * W_ATT, win), F32),
            jax.ShapeDtypeStruct((S, W_RET), BF16),
            jax.ShapeDtypeStruct(state_shape, F32)],
        scratch_shapes=gather_scratch * 3 + [pltpu.VMEM(state_shape, F32)],
        compiler_params=_cparams(1),
        name="inproj_retention_prompt",
    )(x, w_in_b, cos[:tm], sin[:tm], cos[tm:], sin[tm:])


def _inproj_sample(x, w_in_b):
    B = x.shape[0]
    cos, sin = _rope_tables(jnp.full((B,), PAST_LEN, dtype=jnp.int32))
    sin = sin * jnp.where(jnp.arange(LANES) < LANES // 2, -1.0, 1.0)
    full = lambda shape: pl.BlockSpec(shape, lambda i: (0, 0))
    return pl.pallas_call(
        _inproj_sample_body,
        grid=(1,),
        in_specs=[full((B, D_MODEL)), full((D_MODEL, N_SPLITS * W_ATT)), full((B, LANES)), full((B, LANES))],
        out_specs=[full((B, W_ATT)), full((B, 2 * W_ATT)), full((B, W_RET)), full((B, W_RET)),
                   full((B, W_RET)), full((B, W_RET))],
        out_shape=[jax.ShapeDtypeStruct((B, W_ATT), F32),
                   jax.ShapeDtypeStruct((B, 2 * W_ATT), F32),
                   jax.ShapeDtypeStruct((B, W_RET), BF16),
                   jax.ShapeDtypeStruct((B, W_RET), BF16),
                   jax.ShapeDtypeStruct((B, W_RET), BF16),
                   jax.ShapeDtypeStruct((B, W_RET), F32)],
        compiler_params=_cparams(1),
        name="inproj_sample",
    )(x, w_in_b, cos, sin)


def _t5_bucket(dist):
    is_small = dist < MAX_EXACT
    d_f = jnp.maximum(dist, 1).astype(F32)
    large = MAX_EXACT + (jnp.log(d_f / MAX_EXACT) / math.log(MAX_WINDOW / MAX_EXACT)
                         * (N_BUCKETS - MAX_EXACT)).astype(jnp.int32)
    large = jnp.minimum(large, N_BUCKETS - 1)
    return jnp.where(is_small, dist, large)


def _tap_bias(rel_bias):
    taps = jnp.arange(TAPS + 1)
    return jnp.stack([rel_bias[_t5_bucket(taps * d)].astype(F32).T for _, d in DILATIONS])


def _band_bias(tap_bias):
    n = Q_ROWS + 2 * Q_ROWS
    w = jnp.concatenate([tap_bias[..., ::-1], jnp.full(tap_bias.shape[:-1] + (n - TAPS - 1,), NEG, F32)], axis=-1)
    skew = jnp.tile(w, Q_ROWS)[..., :Q_ROWS * (n - 1)].reshape(tap_bias.shape[:-1] + (Q_ROWS, n - 1))
    return skew[..., :2 * Q_ROWS]


def _window_bias(tap_bias, window):
    rows = []
    for i, (_, d) in enumerate(DILATIONS):
        taps_rev = tap_bias[i, :, TAPS:0:-1]
        seg = jnp.concatenate([taps_rev[..., None], jnp.full((H_ATT, TAPS, d - 1), NEG, F32)], axis=-1)
        rows.append(jnp.concatenate([jnp.full((H_ATT, window - TAPS * d), NEG, F32),
                                     seg.reshape(H_ATT, TAPS * d)], axis=-1))
    return jnp.stack(rows)


def _attn_unit(q, k, v, bias2, head0_b, head0):
    qs = jnp.concatenate([q * head0_b, q * (1 - head0_b)], axis=0)
    s = lax.dot_general(qs, k, (((1,), (1,)), ((), ())), preferred_element_type=F32) + bias2
    m = jnp.max(s, axis=1, keepdims=True)
    p = jnp.exp(s - m)
    l = jnp.sum(p, axis=1, keepdims=True)
    o = jnp.dot(p.astype(BF16), v, preferred_element_type=F32) / l
    lse = m + jnp.log(l)
    o_pair = jnp.where(head0, o[:Q_ROWS], o[Q_ROWS:])
    lse_pair = jnp.where(head0, jnp.broadcast_to(lse[:Q_ROWS], (Q_ROWS, LANES)),
                         jnp.broadcast_to(lse[Q_ROWS:], (Q_ROWS, LANES)))
    return o_pair, lse_pair


def _attn_body(q1, q4, q16, k1c, k1p, k4c, k4p, k16c, k16p, v1c, v1p, v4c, v4p, v16c, v16p, bias_ref,
               out_ref, o16, l16, o4, l4):
    first = pl.program_id(0) == 0
    lane = lax.broadcasted_iota(jnp.int32, (Q_ROWS, LANES), 1)
    head0 = lane < HD_ATT
    head0_b = jnp.where(head0, 1.0, 0.0).astype(BF16)
    col = lax.broadcasted_iota(jnp.int32, (2 * Q_ROWS, 2 * Q_ROWS), 1)
    no_prev = jnp.where(jnp.logical_and(first, col < Q_ROWS), NEG, 0.0)

    def bias2(branch, masked_prev):
        b = jnp.concatenate([bias_ref[branch, 0], bias_ref[branch, 1]], axis=0)
        return b + no_prev if masked_prev else b

    def cat(a, b):
        return jnp.concatenate([a, b], axis=0)

    b16 = bias2(2, True)
    for r in range(16):
        sl = slice(r * LANES, (r + 1) * LANES)
        o, l = _attn_unit(q16[:, sl], cat(k16p[:, sl], k16c[:, sl]), cat(v16p[:, sl], v16c[:, sl]),
                          b16, head0_b, head0)
        o16[pl.ds(r, Q_ROWS, stride=16), :] = o
        l16[pl.ds(r, Q_ROWS, stride=16), :] = l

    b4_first = bias2(1, True)
    b4 = bias2(1, False)
    for r in range(4):
        sl = slice(r * LANES, (r + 1) * LANES)
        for b in range(4):
            rows = slice(b * Q_ROWS, (b + 1) * Q_ROWS)
            if b == 0:
                k = cat(k4p[:, sl], k4c[rows, sl])
                v = cat(v4p[:, sl], v4c[rows, sl])
            else:
                both = slice((b - 1) * Q_ROWS, (b + 1) * Q_ROWS)
                k, v = k4c[both, sl], v4c[both, sl]
            o, l = _attn_unit(q4[rows, sl], k, v, b4_first if b == 0 else b4, head0_b, head0)
            o4[pl.ds(b * 4 * Q_ROWS + r, Q_ROWS, stride=4), :] = o
            l4[pl.ds(b * 4 * Q_ROWS + r, Q_ROWS, stride=4), :] = l

    b1_first = bias2(0, True)
    b1 = bias2(0, False)
    for b in range(ATT_BLOCK // Q_ROWS):
        rows = slice(b * Q_ROWS, (b + 1) * Q_ROWS)
        if b == 0:
            k = cat(k1p[...], k1c[rows, :])
            v = cat(v1p[...], v1c[rows, :])
        else:
            both = slice((b - 1) * Q_ROWS, (b + 1) * Q_ROWS)
            k, v = k1c[both, :], v1c[both, :]
        oa, la = _attn_unit(q1[rows, :], k, v, b1_first if b == 0 else b1, head0_b, head0)
        ob, lb = o4[rows, :], l4[rows, :]
        oc, lc = o16[rows, :], l16[rows, :]
        top = jnp.maximum(jnp.maximum(la, lb), lc)
        ea, eb, ec = jnp.exp(la - top), jnp.exp(lb - top), jnp.exp(lc - top)
        out_ref[rows, :] = ((ea * oa + eb * ob + ec * oc) / (ea + eb + ec)).astype(BF16)


def _attn_prompt(q1, q4, q16, k1, k4, k16, v1, v4, v16, band_bias):
    S = q1.shape[1]
    assert S % ATT_BLOCK == 0
    steps = S // ATT_BLOCK

    def cur(d):
        return pl.BlockSpec((None, ATT_BLOCK // d, d * LANES), lambda c, p: (p, c, 0))

    def prev(d):
        per_step = ATT_BLOCK // d // Q_ROWS
        return pl.BlockSpec((None, Q_ROWS, d * LANES), lambda c, p: (p, jnp.maximum(c * per_step - 1, 0), 0))

    scratch = pltpu.VMEM((ATT_BLOCK, LANES), F32)
    return pl.pallas_call(
        _attn_body,
        grid=(steps, PAIRS),
        in_specs=[cur(1), cur(4), cur(16),
                  cur(1), prev(1), cur(4), prev(4), cur(16), prev(16),
                  cur(1), prev(1), cur(4), prev(4), cur(16), prev(16),
                  pl.BlockSpec((len(DILATIONS), 2, Q_ROWS, 2 * Q_ROWS), lambda c, p: (0, p, 0, 0))],
        out_specs=pl.BlockSpec((ATT_BLOCK, LANES), lambda c, p: (c, p)),
        out_shape=jax.ShapeDtypeStruct((S, W_ATT), BF16),
        scratch_shapes=[scratch, scratch, scratch, scratch],
        compiler_params=_cparams(2),
        name="attn_prompt",
    )(q1, q4, q16, k1, k1, k4, k4, k16, k16, v1, v1, v4, v4, v16, v16, band_bias)


def _group_norm_gate(o, g):
    mu = jnp.mean(o, axis=1, keepdims=True)
    var = jnp.mean(jnp.square(o - mu), axis=1, keepdims=True)
    return g * jax.nn.sigmoid(g) * ((o - mu) * lax.rsqrt(var + GN_EPS))


def _retention_decays(h):
    C = RET_CHUNK
    lg = LOG_GAMMA[h]
    n = lax.broadcasted_iota(jnp.int32, (C, 1), 0).astype(F32)
    diff = (lax.broadcasted_iota(jnp.int32, (C, C), 0) - lax.broadcasted_iota(jnp.int32, (C, C), 1)).astype(F32)
    return (jnp.where(diff >= 0, jnp.exp(lg * jnp.maximum(diff, 0.0)), 0.0),
            jnp.exp(lg * (n + 1.0)), jnp.exp(lg * (C - 1.0 - n)), math.exp(lg * C))


def _retention_chunk(q, k, v, g, S, decays):
    decay, q_decay, k_decay, chunk_decay = decays
    qb, vb = q.astype(BF16), v.astype(BF16)
    scores = lax.dot_general(qb, k.astype(BF16), (((1,), (1,)), ((), ())), preferred_element_type=F32) * decay
    o = (jnp.dot(scores.astype(BF16), vb, preferred_element_type=F32)
         + jnp.dot(qb, S.astype(BF16), preferred_element_type=F32) * q_decay)
    k_dec_t = (k * k_decay).T.astype(BF16)
    return _group_norm_gate(o, g), chunk_decay * S + jnp.dot(k_dec_t, vb, preferred_element_type=F32)


def _as_column(row_pair):
    return jnp.broadcast_to(row_pair, (LANES, LANES)).T


def _sample_mixer_body(qa_ref, kvw_ref, qr_ref, kr_ref, vr_ref, g_ref, cache_ref, st_ref, bias_ref, bias0_ref,
                       oatt_ref, yret_ref, stout_ref):
    W = cache_ref.shape[-1]
    logit_rows, self_rows = [], []
    for p in range(PAIRS):
        lanes = slice(p * LANES, (p + 1) * LANES)
        q_pair = qa_ref[:, lanes]
        k_pair = kvw_ref[:, lanes]
        q_col = _as_column(q_pair)
        for hh in range(2):
            feat = slice(hh * HD_ATT, (hh + 1) * HD_ATT)
            logit_rows.append(jnp.sum(cache_ref[0, 2 * p + hh] * q_col[feat, 0:1], axis=0, keepdims=True))
            self_rows.append(jnp.sum(q_pair[:, feat] * k_pair[:, feat], axis=1, keepdims=True))
    logits = jnp.concatenate(logit_rows, axis=0)
    s_self = jnp.concatenate(self_rows, axis=0) + bias0_ref[:, 0:1]

    probs, p_selfs, denoms, lses = [], [], [], []
    for i, (_, d) in enumerate(DILATIONS):
        lo = W - TAPS * d
        s = logits[:, lo:] + bias_ref[i, :, lo:]
        m = jnp.maximum(jnp.max(s, axis=1, keepdims=True), s_self)
        pr = jnp.exp(s - m)
        p_self = jnp.exp(s_self - m)
        l = jnp.sum(pr, axis=1, keepdims=True) + p_self
        probs.append(pr)
        p_selfs.append(p_self)
        denoms.append(l)
        lses.append(m + jnp.log(l))
    top = jnp.maximum(jnp.maximum(lses[0], lses[1]), lses[2])
    e = [jnp.exp(x - top) for x in lses]
    e_sum = e[0] + e[1] + e[2]
    coef = [e[i] / (e_sum * denoms[i]) for i in range(3)]
    lo4, lo1 = W - TAPS * 4, W - TAPS
    p16, p4, p1 = coef[2] * probs[2], coef[1] * probs[1], coef[0] * probs[0]
    p_all = jnp.concatenate([p16[:, :lo4], p16[:, lo4:lo1] + p4[:, :lo1 - lo4],
                             p16[:, lo1:] + p4[:, lo1 - lo4:] + p1], axis=1)
    c_self = coef[0] * p_selfs[0] + coef[1] * p_selfs[1] + coef[2] * p_selfs[2]
    for p in range(PAIRS):
        lanes = slice(p * LANES, (p + 1) * LANES)
        v_col = _as_column(kvw_ref[:, W_ATT + p * LANES:W_ATT + (p + 1) * LANES])
        out_cols = []
        for hh in range(2):
            h = 2 * p + hh
            feat = slice(hh * HD_ATT, (hh + 1) * HD_ATT)
            pv = jnp.sum(cache_ref[1, h] * p_all[h:h + 1, :], axis=1, keepdims=True)
            out_cols.append(pv + c_self[h:h + 1, :] * v_col[feat, 0:1])
        col = jnp.broadcast_to(jnp.concatenate(out_cols, axis=0), (LANES, LANES))
        oatt_ref[:, lanes] = col.T[0:1, :].astype(BF16)

    for h in range(H_RET):
        cols = slice(h * DK_RET, (h + 1) * DK_RET)
        gamma = math.exp(LOG_GAMMA[h])
        qh = qr_ref[:, cols]
        kh = kr_ref[:, cols].astype(F32)
        vh = vr_ref[:, cols].astype(F32)
        S = st_ref[h]
        qk = jnp.sum(qh.astype(F32) * kh, axis=1, keepdims=True)
        cross = jnp.dot(jnp.broadcast_to(qh, (16, DK_RET)), S.astype(BF16), preferred_element_type=F32)[:1]
        o = qk * vh + cross * gamma
        stout_ref[h] = gamma * S + _as_column(kh) * vh
        yret_ref[:, cols] = _group_norm_gate(o, g_ref[:, cols]).astype(BF16)


def _sample_mixer(qa, kvw, qr, kr, vr, g, cache_t, state, tap_bias):
    B, _, _, _, W = cache_t.shape
    assert W == MAX_WINDOW and PAST_LEN >= MAX_WINDOW
    row3 = lambda a: a.reshape(B, 1, a.shape[-1])
    vec = lambda width: pl.BlockSpec((None, 1, width), lambda b: (b, 0, 0))
    bias0 = jnp.broadcast_to(tap_bias[0, :, 0:1], (H_ATT, LANES))
    st_spec = pl.BlockSpec((None, H_RET, DK_RET, DV_RET), lambda b: (b, 0, 0, 0))
    oatt, yret, st = pl.pallas_call(
        _sample_mixer_body,
        grid=(B,),
        in_specs=[vec(W_ATT), vec(2 * W_ATT), vec(W_RET), vec(W_RET), vec(W_RET), vec(W_RET),
                  pl.BlockSpec((None, 2, H_ATT, HD_ATT, W), lambda b: (b, 0, 0, 0, 0)),
                  st_spec,
                  pl.BlockSpec((len(DILATIONS), H_ATT, W), lambda b: (0, 0, 0)),
                  pl.BlockSpec((H_ATT, LANES), lambda b: (0, 0))],
        out_specs=[vec(W_ATT), vec(W_RET), st_spec],
        out_shape=[jax.ShapeDtypeStruct((B, 1, W_ATT), BF16),
                   jax.ShapeDtypeStruct((B, 1, W_RET), BF16),
                   jax.ShapeDtypeStruct(state.shape, F32)],
        compiler_params=_cparams(1),
        name="sample_mixer",
    )(row3(qa), row3(kvw), row3(qr), row3(kr), row3(vr), row3(g), cache_t, state,
      _window_bias(tap_bias, W), bias0)
    return oatt.reshape(B, W_ATT), yret.reshape(B, W_RET), st


FF_CHUNK = 1024


def _layernorm(z, g, b):
    mu = jnp.mean(z, axis=1, keepdims=True)
    var = jnp.mean(jnp.square(z - mu), axis=1, keepdims=True)
    return (z - mu) * lax.rsqrt(var + LN_EPS) * g + b


def _out_ffn_body(x_ref, oatt_ref, yret_ref, wo_ref, g1_ref, b1_ref, wu_ref, wd_ref, g2_ref, b2_ref, y_ref):
    mix = (jnp.dot(oatt_ref[...], wo_ref[:W_ATT, :], preferred_element_type=F32)
           + jnp.dot(yret_ref[...], wo_ref[W_ATT:, :], preferred_element_type=F32))
    x1 = _layernorm(ALPHA * x_ref[...] + mix, g1_ref[...], b1_ref[...])
    x1b = x1.astype(BF16)
    ffn = jnp.zeros_like(x1)
    for j in range(D_FF // FF_CHUNK):
        cols = slice(j * FF_CHUNK, (j + 1) * FF_CHUNK)
        h = jnp.square(jnp.maximum(jnp.dot(x1b, wu_ref[:, cols], preferred_element_type=F32), 0.0))
        ffn = ffn + jnp.dot(h.astype(BF16), wd_ref[cols, :], preferred_element_type=F32)
    y_ref[...] = _layernorm(ALPHA * x1 + ffn, g2_ref[...], b2_ref[...])


def _out_ffn(x, oatt, yret, wo_b, g1, b1, wu_b, wd_b, g2, b2, tm, name):
    M = x.shape[0]
    row = lambda i: (i, 0)
    const = lambda shape: pl.BlockSpec(shape, lambda i: (0, 0), pipeline_mode=pl.Buffered(1))
    return pl.pallas_call(
        _out_ffn_body,
        grid=(M // tm,),
        in_specs=[pl.BlockSpec((tm, D_MODEL), row), pl.BlockSpec((tm, W_ATT), row), pl.BlockSpec((tm, W_RET), row),
                  const((W_ATT + W_RET, D_MODEL)), const((1, D_MODEL)), const((1, D_MODEL)),
                  const((D_MODEL, D_FF)), const((D_FF, D_MODEL)), const((1, D_MODEL)), const((1, D_MODEL))],
        out_specs=pl.BlockSpec((tm, D_MODEL), row),
        out_shape=jax.ShapeDtypeStruct((M, D_MODEL), F32),
        compiler_params=_cparams(1),
        name=name,
    )(x, oatt, yret, wo_b, g1, b1, wu_b, wd_b, g2, b2)


PROMPT_ROWS = 512
FFN_ROWS = 512


def kernel(x_prompt, x_sample, cache_kv_win, state_ret, w_in, rel_bias, w_out,
           ln1_g, ln1_b, w_up, w_down, ln2_g, ln2_b):
    assert x_prompt.shape[0] == 1 and x_sample.shape[1] == 1 and w_in.shape[0] == DEPTH
    S = x_prompt.shape[1]
    B = x_sample.shape[0]
    w_in_b, wo_b = w_in[0].astype(BF16), w_out[0].astype(BF16)
    wu_b, wd_b = w_up[0].astype(BF16), w_down[0].astype(BF16)
    g1, b1, g2, b2 = ln1_g[0][None], ln1_b[0][None], ln2_g[0][None], ln2_b[0][None]
    tap_bias = _tap_bias(rel_bias)

    xp = x_prompt[0]
    *qkv_views, kvw_p, yret, st_p = _inproj_prompt(xp, w_in_b, PROMPT_ROWS)
    oatt = _attn_prompt(*qkv_views, _band_bias(tap_bias))
    y_p = _out_ffn(xp, oatt, yret, wo_b, g1, b1, wu_b, wd_b, g2, b2, FFN_ROWS, "out_ffn_prompt")

    xs = x_sample[:, 0]
    qa_s, kvw_s, qr_s, kr_s, vr_s, g_s = _inproj_sample(xs, w_in_b)
    cache_t = jnp.transpose(cache_kv_win[0], (0, 2, 3, 4, 1))
    oatt_s, yret_s, st_s = _sample_mixer(qa_s, kvw_s, qr_s, kr_s, vr_s, g_s, cache_t, state_ret[0], tap_bias)
    y_s = _out_ffn(xs, oatt_s, yret_s, wo_b, g1, b1, wu_b, wd_b, g2, b2, B, "out_ffn_sample")

    win = kvw_p.shape[1]
    kv_win_p = jnp.transpose(kvw_p.reshape(2, H_ATT, HD_ATT, win), (3, 0, 1, 2))
    return (y_p[None], y_s[:, None],
            kv_win_p[None, None], kvw_s.reshape(1, B, 1, 2, H_ATT, HD_ATT),
            st_p[None, None], st_s[None])
```

```python
import functools
import math

import jax
import jax.numpy as jnp
from jax import lax
from jax.experimental import pallas as pl
from jax.experimental.pallas import tpu as pltpu

F32 = jnp.float32
BF16 = jnp.bfloat16

D_MODEL = 1024
DEPTH = 1
PAST_LEN = 16384
W_ATT = 512
HD_ATT = 64
H_ATT = 8
DILATIONS = ((128, 1), (512, 4), (2048, 16))
TAPS = 128
MAX_WINDOW = 2048
N_BUCKETS = 32
MAX_EXACT = N_BUCKETS // 2
W_RET = 512
H_RET = 4
DK_RET = 128
DV_RET = 128
RET_CHUNK = 128
ROPE_BASE = 10000.0
D_FF = 4096
N_SPLITS = 7
ALPHA = (2.0 * DEPTH) ** 0.25
LN_EPS = 1e-5
GN_EPS = 1e-6
LOG_GAMMA = tuple(math.log(1.0 - 2.0 ** (-5.0 - h)) for h in range(H_RET))

LANES = 128
PAIRS = W_ATT // LANES
NEG = -1e30

ATT_BLOCK = 2048
Q_ROWS = 128
VMEM_LIMIT = 56 * 1024 * 1024


def _cparams(n_axes):
    return pltpu.CompilerParams(dimension_semantics=("arbitrary",) * n_axes, vmem_limit_bytes=VMEM_LIMIT)


def _rotary(u, cos, sin_signed):
    outs = []
    for h in range(H_RET):
        xh = u[:, h * LANES:(h + 1) * LANES]
        outs.append(xh * cos + pltpu.roll(xh, LANES // 2, 1) * sin_signed)
    return jnp.concatenate(outs, axis=1)


def _proj(xb, w_ref, j):
    return jnp.dot(xb, w_ref[:, j * W_ATT:(j + 1) * W_ATT], preferred_element_type=F32)


def _store_rows(val, rows, nat_ref, rows_scr):
    for p in range(PAIRS):
        x = val[:, p * LANES:(p + 1) * LANES]
        nat_ref[p, rows, :] = x.astype(BF16)
        rows_scr[p, rows, :] = x


def _store_gathered(by4_ref, by16_ref, rows_scr, by4_scr):
    tm = rows_scr.shape[1]
    for p in range(PAIRS):
        for r4 in range(4):
            a = rows_scr[p, pl.ds(r4, tm // 4, stride=4), :]
            by4_ref[p, :, r4 * LANES:(r4 + 1) * LANES] = a.astype(BF16)
            by4_scr[p * 4 + r4] = a
    for p in range(PAIRS):
        for r4 in range(4):
            for rr in range(4):
                r16 = r4 + 4 * rr
                b = by4_scr[p * 4 + r4, pl.ds(rr, tm // 16, stride=4), :]
                by16_ref[p, :, r16 * LANES:(r16 + 1) * LANES] = b.astype(BF16)


def _retention_qkvg(xb, w_ref, cos, sin):
    return (_rotary(_proj(xb, w_ref, 3), cos, sin), _rotary(_proj(xb, w_ref, 4), cos, sin) * DK_RET ** -0.5,
            _proj(xb, w_ref, 5), _proj(xb, w_ref, 6))


def _inproj_prompt_body(x_ref, w_ref, cos_row_ref, sin_row_ref, cos_step_ref, sin_step_ref,
                        q1_ref, q4_ref, q16_ref, k1_ref, k4_ref, k16_ref, v1_ref, v4_ref, v16_ref, kvw_ref,
                        yret_ref, st_ref, q_rows, q_by4, k_rows, k_by4, v_rows, v_by4, state, *, win_first_step):
    @pl.when(pl.program_id(0) == 0)
    def _():
        state[...] = jnp.zeros_like(state)

    xb = x_ref[...].astype(BF16)
    tm = xb.shape[0]
    half = tm // 2

    step = pl.ds(pl.program_id(0), 1)
    ca, sa = cos_step_ref[step, :], sin_step_ref[step, :]
    cb, sb = cos_row_ref[...], sin_row_ref[...]
    sign = jnp.where(lax.broadcasted_iota(jnp.int32, (1, LANES), 1) < LANES // 2, -1.0, 1.0)
    cos, sin = ca * cb - sa * sb, (sa * cb + ca * sb) * sign

    def retention_proj(j, rows):
        u = _proj(xb[rows], w_ref, 3 + j)
        if j < 2:
            u = _rotary(u, cos[rows], sin[rows])
        return u * DK_RET ** -0.5 if j == 1 else u

    attn_out = ((q1_ref, q4_ref, q16_ref, q_rows, q_by4), (k1_ref, k4_ref, k16_ref, k_rows, k_by4),
                (v1_ref, v4_ref, v16_ref, v_rows, v_by4))

    def attention_proj(j, rows, last):
        nat_ref, by4_ref, by16_ref, rows_scr, by4_scr = attn_out[j]
        val = _proj(xb[rows], w_ref, j)
        _store_rows(val * HD_ATT ** -0.5 if j == 0 else val, rows, nat_ref, rows_scr)
        if last:
            _store_gathered(by4_ref, by16_ref, rows_scr, by4_scr)

    halves = (slice(0, half), slice(half, tm))
    ret_in = [[retention_proj(j, halves[0]) for j in range(4)], [None] * 4]
    fills = [[("ret", 0), ("ret", 1), ("ret", 2), ("ret", 3), ("att", 0, 0)],
             [("att", 0, 1), ("att", 1, 0), ("att", 1, 1), ("att", 2, 0), ("att", 2, 1)]]
    decays = [_retention_decays(h) for h in range(H_RET)]
    S = [state[h] for h in range(H_RET)]
    units_per_half = (half // RET_CHUNK) * H_RET
    for ph in range(2):
        qr, kr, vr, g = ret_in[ph]
        emit_after = {((i + 1) * units_per_half) // len(fills[ph]) - 1: f for i, f in enumerate(fills[ph])}
        unit = 0
        for c in range(half // RET_CHUNK):
            rows = slice(c * RET_CHUNK, (c + 1) * RET_CHUNK)
            out_rows = slice(ph * half + c * RET_CHUNK, ph * half + (c + 1) * RET_CHUNK)
            for h in range(H_RET):
                cols = slice(h * DK_RET, (h + 1) * DK_RET)
                y, S[h] = _retention_chunk(qr[rows, cols], kr[rows, cols], vr[rows, cols], g[rows, cols],
                                           S[h], decays[h])
                yret_ref[out_rows, cols] = y.astype(BF16)
                fill = emit_after.get(unit)
                if fill is not None and fill[0] == "ret":
                    ret_in[1][fill[1]] = retention_proj(fill[1], halves[1])
                elif fill is not None:
                    attention_proj(fill[1], halves[fill[2]], last=fill[2] == 1)
                unit += 1
    for h in range(H_RET):
        state[h] = S[h]
        st_ref[h] = S[h]

    @pl.when(pl.program_id(0) >= win_first_step)
    def _():
        for p in range(PAIRS):
            kvw_ref[p * LANES:(p + 1) * LANES, :] = k_rows[p].T
            kvw_ref[W_ATT + p * LANES:W_ATT + (p + 1) * LANES, :] = v_rows[p].T


def _inproj_sample_body(x_ref, w_ref, cos_ref, sin_ref, qa_ref, kvw_ref, qr_ref, kr_ref, vr_ref, g_ref):
    xb = x_ref[...].astype(BF16)
    qa_ref[...] = _proj(xb, w_ref, 0) * HD_ATT ** -0.5
    kvw_ref[:, :W_ATT] = _proj(xb, w_ref, 1)
    kvw_ref[:, W_ATT:] = _proj(xb, w_ref, 2)
    qr, kr, vr, g = _retention_qkvg(xb, w_ref, cos_ref[...], sin_ref[...])
    qr_ref[...] = qr.astype(BF16)
    kr_ref[...] = kr.astype(BF16)
    vr_ref[...] = vr.astype(BF16)
    g_ref[...] = g


def _rope_tables(pos):
    half = DK_RET // 2
    inv_freq = 1.0 / (ROPE_BASE ** jnp.linspace(0.0, 1.0, half, dtype=F32))
    ang = pos.astype(F32)[:, None] * inv_freq[None, :]
    cos, sin = jnp.cos(ang), jnp.sin(ang)
    return jnp.concatenate([cos, cos], axis=1), jnp.concatenate([sin, sin], axis=1)


def _inproj_prompt(x, w_in_b, tm):
    S = x.shape[0]
    win = min(MAX_WINDOW, S)
    steps = S // tm
    win_first_step = steps - win // tm
    cos, sin = _rope_tables(jnp.concatenate([jnp.arange(tm, dtype=jnp.int32),
                                             jnp.arange(steps, dtype=jnp.int32) * tm]))
    const = lambda i: (0, 0)
    row = lambda i: (i, 0)
    view_specs = [pl.BlockSpec((PAIRS, tm // d, d * LANES), lambda i: (0, i, 0)) for d in (1, 4, 16)] * 3
    view_shapes = [jax.ShapeDtypeStruct((PAIRS, S // d, d * LANES), BF16) for d in (1, 4, 16)] * 3
    state_shape = (H_RET, DK_RET, DV_RET)
    gather_scratch = [pltpu.VMEM((PAIRS, tm, LANES), F32), pltpu.VMEM((4 * PAIRS, tm // 4, LANES), F32)]
    return pl.pallas_call(
        functools.partial(_inproj_prompt_body, win_first_step=win_first_step),
        grid=(steps,),
        in_specs=[pl.BlockSpec((tm, D_MODEL), row),
                  pl.BlockSpec((D_MODEL, N_SPLITS * W_ATT), lambda i: (0, 0)),
                  pl.BlockSpec((tm, LANES), const), pl.BlockSpec((tm, LANES), const),
                  pl.BlockSpec((steps, LANES), const), pl.BlockSpec((steps, LANES), const)],
        out_specs=view_specs + [
            pl.BlockSpec((2 * W_ATT, tm), lambda i: (0, jnp.maximum(i - win_first_step, 0))),
            pl.BlockSpec((tm, W_RET), row),
            pl.BlockSpec(state_shape, lambda i: (0, 0, 0))],
        out_shape=view_shapes + [
            jax.ShapeDtypeStruct((2 * W_ATT, win), F32),
            jax.ShapeDtypeStruct((S, W_RET), BF16),
            jax.ShapeDtypeStruct(state_shape, F32)],
        scratch_shapes=gather_scratch * 3 + [pltpu.VMEM(state_shape, F32)],
        compiler_params=_cparams(1),
        name="inproj_retention_prompt",
    )(x, w_in_b, cos[:tm], sin[:tm], cos[tm:], sin[tm:])


def _inproj_sample(x, w_in_b):
    B = x.shape[0]
    cos, sin = _rope_tables(jnp.full((B,), PAST_LEN, dtype=jnp.int32))
    sin = sin * jnp.where(jnp.arange(LANES) < LANES // 2, -1.0, 1.0)
    full = lambda shape: pl.BlockSpec(shape, lambda i: (0, 0))
    return pl.pallas_call(
        _inproj_sample_body,
        grid=(1,),
        in_specs=[full((B, D_MODEL)), full((D_MODEL, N_SPLITS * W_ATT)), full((B, LANES)), full((B, LANES))],
        out_specs=[full((B, W_ATT)), full((B, 2 * W_ATT)), full((B, W_RET)), full((B, W_RET)),
                   full((B, W_RET)), full((B, W_RET))],
        out_shape=[jax.ShapeDtypeStruct((B, W_ATT), F32),
                   jax.ShapeDtypeStruct((B, 2 * W_ATT), F32),
                   jax.ShapeDtypeStruct((B, W_RET), BF16),
                   jax.ShapeDtypeStruct((B, W_RET), BF16),
                   jax.ShapeDtypeStruct((B, W_RET), BF16),
                   jax.ShapeDtypeStruct((B, W_RET), F32)],
        compiler_params=_cparams(1),
        name="inproj_sample",
    )(x, w_in_b, cos, sin)


def _t5_bucket(dist):
    is_small = dist < MAX_EXACT
    d_f = jnp.maximum(dist, 1).astype(F32)
    large = MAX_EXACT + (jnp.log(d_f / MAX_EXACT) / math.log(MAX_WINDOW / MAX_EXACT)
                         * (N_BUCKETS - MAX_EXACT)).astype(jnp.int32)
    large = jnp.minimum(large, N_BUCKETS - 1)
    return jnp.where(is_small, dist, large)


def _tap_bias(rel_bias):
    taps = jnp.arange(TAPS + 1)
    return jnp.stack([rel_bias[_t5_bucket(taps * d)].astype(F32).T for _, d in DILATIONS])


def _band_bias(tap_bias):
    n = Q_ROWS + 2 * Q_ROWS
    w = jnp.concatenate([tap_bias[..., ::-1], jnp.full(tap_bias.shape[:-1] + (n - TAPS - 1,), NEG, F32)], axis=-1)
    skew = jnp.tile(w, Q_ROWS)[..., :Q_ROWS * (n - 1)].reshape(tap_bias.shape[:-1] + (Q_ROWS, n - 1))
    return skew[..., :2 * Q_ROWS]


def _window_bias(tap_bias, window):
    rows = []
    for i, (_, d) in enumerate(DILATIONS):
        taps_rev = tap_bias[i, :, TAPS:0:-1]
        seg = jnp.concatenate([taps_rev[..., None], jnp.full((H_ATT, TAPS, d - 1), NEG, F32)], axis=-1)
        rows.append(jnp.concatenate([jnp.full((H_ATT, window - TAPS * d), NEG, F32),
                                     seg.reshape(H_ATT, TAPS * d)], axis=-1))
    return jnp.stack(rows)


def _attn_unit(q, k, v, bias2, head0_b, head0):
    qs = jnp.concatenate([q * head0_b, q * (1 - head0_b)], axis=0)
    s = lax.dot_general(qs, k, (((1,), (1,)), ((), ())), preferred_element_type=F32) + bias2
    m = jnp.max(s, axis=1, keepdims=True)
    p = jnp.exp(s - m)
    l = jnp.sum(p, axis=1, keepdims=True)
    o = jnp.dot(p.astype(BF16), v, preferred_element_type=F32) / l
    lse = m + jnp.log(l)
    o_pair = jnp.where(head0, o[:Q_ROWS], o[Q_ROWS:])
    lse_pair = jnp.where(head0, jnp.broadcast_to(lse[:Q_ROWS], (Q_ROWS, LANES)),
                         jnp.broadcast_to(lse[Q_ROWS:], (Q_ROWS, LANES)))
    return o_pair, lse_pair


def _attn_body(q1, q4, q16, k1c, k1p, k4c, k4p, k16c, k16p, v1c, v1p, v4c, v4p, v16c, v16p, bias_ref,
               out_ref, o16, l16, o4, l4):
    first = pl.program_id(0) == 0
    lane = lax.broadcasted_iota(jnp.int32, (Q_ROWS, LANES), 1)
    head0 = lane < HD_ATT
    head0_b = jnp.where(head0, 1.0, 0.0).astype(BF16)
    col = lax.broadcasted_iota(jnp.int32, (2 * Q_ROWS, 2 * Q_ROWS), 1)
    no_prev = jnp.where(jnp.logical_and(first, col < Q_ROWS), NEG, 0.0)

    def bias2(branch, masked_prev):
        b = jnp.concatenate([bias_ref[branch, 0], bias_ref[branch, 1]], axis=0)
        return b + no_prev if masked_prev else b

    def cat(a, b):
        return jnp.concatenate([a, b], axis=0)

    b16 = bias2(2, True)
    for r in range(16):
        sl = slice(r * LANES, (r + 1) * LANES)
        o, l = _attn_unit(q16[:, sl], cat(k16p[:, sl], k16c[:, sl]), cat(v16p[:, sl], v16c[:, sl]),
                          b16, head0_b, head0)
        o16[pl.ds(r, Q_ROWS, stride=16), :] = o
        l16[pl.ds(r, Q_ROWS, stride=16), :] = l

    b4_first = bias2(1, True)
    b4 = bias2(1, False)
    for r in range(4):
        sl = slice(r * LANES, (r + 1) * LANES)
        for b in range(4):
            rows = slice(b * Q_ROWS, (b + 1) * Q_ROWS)
            if b == 0:
                k = cat(k4p[:, sl], k4c[rows, sl])
                v = cat(v4p[:, sl], v4c[rows, sl])
            else:
                both = slice((b - 1) * Q_ROWS, (b + 1) * Q_ROWS)
                k, v = k4c[both, sl], v4c[both, sl]
            o, l = _attn_unit(q4[rows, sl], k, v, b4_first if b == 0 else b4, head0_b, head0)
            o4[pl.ds(b * 4 * Q_ROWS + r, Q_ROWS, stride=4), :] = o
            l4[pl.ds(b * 4 * Q_ROWS + r, Q_ROWS, stride=4), :] = l

    b1_first = bias2(0, True)
    b1 = bias2(0, False)
    for b in range(ATT_BLOCK // Q_ROWS):
        rows = slice(b * Q_ROWS, (b + 1) * Q_ROWS)
        if b == 0:
            k = cat(k1p[...], k1c[rows, :])
            v = cat(v1p[...], v1c[rows, :])
        else:
            both = slice((b - 1) * Q_ROWS, (b + 1) * Q_ROWS)
            k, v = k1c[both, :], v1c[both, :]
        oa, la = _attn_unit(q1[rows, :], k, v, b1_first if b == 0 else b1, head0_b, head0)
        ob, lb = o4[rows, :], l4[rows, :]
        oc, lc = o16[rows, :], l16[rows, :]
        top = jnp.maximum(jnp.maximum(la, lb), lc)
        ea, eb, ec = jnp.exp(la - top), jnp.exp(lb - top), jnp.exp(lc - top)
        out_ref[rows, :] = ((ea * oa + eb * ob + ec * oc) / (ea + eb + ec)).astype(BF16)


def _attn_prompt(q1, q4, q16, k1, k4, k16, v1, v4, v16, band_bias):
    S = q1.shape[1]
    assert S % ATT_BLOCK == 0
    steps = S // ATT_BLOCK

    def cur(d):
        return pl.BlockSpec((None, ATT_BLOCK // d, d * LANES), lambda c, p: (p, c, 0))

    def prev(d):
        per_step = ATT_BLOCK // d // Q_ROWS
        return pl.BlockSpec((None, Q_ROWS, d * LANES), lambda c, p: (p, jnp.maximum(c * per_step - 1, 0), 0))

    scratch = pltpu.VMEM((ATT_BLOCK, LANES), F32)
    return pl.pallas_call(
        _attn_body,
        grid=(steps, PAIRS),
        in_specs=[cur(1), cur(4), cur(16),
                  cur(1), prev(1), cur(4), prev(4), cur(16), prev(16),
                  cur(1), prev(1), cur(4), prev(4), cur(16), prev(16),
                  pl.BlockSpec((len(DILATIONS), 2, Q_ROWS, 2 * Q_ROWS), lambda c, p: (0, p, 0, 0))],
        out_specs=pl.BlockSpec((ATT_BLOCK, LANES), lambda c, p: (c, p)),
        out_shape=jax.ShapeDtypeStruct((S, W_ATT), BF16),
        scratch_shapes=[scratch, scratch, scratch, scratch],
        compiler_params=_cparams(2),
        name="attn_prompt",
    )(q1, q4, q16, k1, k1, k4, k4, k16, k16, v1, v1, v4, v4, v16, v16, band_bias)


def _group_norm_gate(o, g):
    mu = jnp.mean(o, axis=1, keepdims=True)
    var = jnp.mean(jnp.square(o - mu), axis=1, keepdims=True)
    return g * jax.nn.sigmoid(g) * ((o - mu) * lax.rsqrt(var + GN_EPS))


def _retention_decays(h):
    C = RET_CHUNK
    lg = LOG_GAMMA[h]
    n = lax.broadcasted_iota(jnp.int32, (C, 1), 0).astype(F32)
    diff = (lax.broadcasted_iota(jnp.int32, (C, C), 0) - lax.broadcasted_iota(jnp.int32, (C, C), 1)).astype(F32)
    return (jnp.where(diff >= 0, jnp.exp(lg * jnp.maximum(diff, 0.0)), 0.0),
            jnp.exp(lg * (n + 1.0)), jnp.exp(lg * (C - 1.0 - n)), math.exp(lg * C))


def _retention_chunk(q, k, v, g, S, decays):
    decay, q_decay, k_decay, chunk_decay = decays
    qb, vb = q.astype(BF16), v.astype(BF16)
    scores = lax.dot_general(qb, k.astype(BF16), (((1,), (1,)), ((), ())), preferred_element_type=F32) * decay
    o = (jnp.dot(scores.astype(BF16), vb, preferred_element_type=F32)
         + jnp.dot(qb, S.astype(BF16), preferred_element_type=F32) * q_decay)
    k_dec_t = (k * k_decay).T.astype(BF16)
    return _group_norm_gate(o, g), chunk_decay * S + jnp.dot(k_dec_t, vb, preferred_element_type=F32)


def _as_column(row_pair):
    return jnp.broadcast_to(row_pair, (LANES, LANES)).T


def _sample_mixer_body(qa_ref, kvw_ref, qr_ref, kr_ref, vr_ref, g_ref, cache_ref, st_ref, bias_ref, bias0_ref,
                       oatt_ref, yret_ref, stout_ref):
    W = cache_ref.shape[-1]
    logit_rows, self_rows = [], []
    for p in range(PAIRS):
        lanes = slice(p * LANES, (p + 1) * LANES)
        q_pair = qa_ref[:, lanes]
        k_pair = kvw_ref[:, lanes]
        q_col = _as_column(q_pair)
        for hh in range(2):
            feat = slice(hh * HD_ATT, (hh + 1) * HD_ATT)
            logit_rows.append(jnp.sum(cache_ref[0, 2 * p + hh] * q_col[feat, 0:1], axis=0, keepdims=True))
            self_rows.append(jnp.sum(q_pair[:, feat] * k_pair[:, feat], axis=1, keepdims=True))
    logits = jnp.concatenate(logit_rows, axis=0)
    s_self = jnp.concatenate(self_rows, axis=0) + bias0_ref[:, 0:1]

    probs, p_selfs, denoms, lses = [], [], [], []
    for i, (_, d) in enumerate(DILATIONS):
        lo = W - TAPS * d
        s = logits[:, lo:] + bias_ref[i, :, lo:]
        m = jnp.maximum(jnp.max(s, axis=1, keepdims=True), s_self)
        pr = jnp.exp(s - m)
        p_self = jnp.exp(s_self - m)
        l = jnp.sum(pr, axis=1, keepdims=True) + p_self
        probs.append(pr)
        p_selfs.append(p_self)
        denoms.append(l)
        lses.append(m + jnp.log(l))
    top = jnp.maximum(jnp.maximum(lses[0], lses[1]), lses[2])
    e = [jnp.exp(x - top) for x in lses]
    e_sum = e[0] + e[1] + e[2]
    coef = [e[i] / (e_sum * denoms[i]) for i in range(3)]
    lo4, lo1 = W - TAPS * 4, W - TAPS
    p16, p4, p1 = coef[2] * probs[2], coef[1] * probs[1], coef[0] * probs[0]
    p_all = jnp.concatenate([p16[:, :lo4], p16[:, lo4:lo1] + p4[:, :lo1 - lo4],
                             p16[:, lo1:] + p4[:, lo1 - lo4:] + p1], axis=1)
    c_self = coef[0] * p_selfs[0] + coef[1] * p_selfs[1] + coef[2] * p_selfs[2]
    for p in range(PAIRS):
        lanes = slice(p * LANES, (p + 1) * LANES)
        v_col = _as_column(kvw_ref[:, W_ATT + p * LANES:W_ATT + (p + 1) * LANES])
        out_cols = []
        for hh in range(2):
            h = 2 * p + hh
            feat = slice(hh * HD_ATT, (hh + 1) * HD_ATT)
            pv = jnp.sum(cache_ref[1, h] * p_all[h:h + 1, :], axis=1, keepdims=True)
            out_cols.append(pv + c_self[h:h + 1, :] * v_col[feat, 0:1])
        col = jnp.broadcast_to(jnp.concatenate(out_cols, axis=0), (LANES, LANES))
        oatt_ref[:, lanes] = col.T[0:1, :].astype(BF16)

    for h in range(H_RET):
        cols = slice(h * DK_RET, (h + 1) * DK_RET)
        gamma = math.exp(LOG_GAMMA[h])
        qh = qr_ref[:, cols]
        kh = kr_ref[:, cols].astype(F32)
        vh = vr_ref[:, cols].astype(F32)
        S = st_ref[h]
        qk = jnp.sum(qh.astype(F32) * kh, axis=1, keepdims=True)
        cross = jnp.dot(jnp.broadcast_to(qh, (16, DK_RET)), S.astype(BF16), preferred_element_type=F32)[:1]
        o = qk * vh + cross * gamma
        stout_ref[h] = gamma * S + _as_column(kh) * vh
        yret_ref[:, cols] = _group_norm_gate(o, g_ref[:, cols]).astype(BF16)


def _mixer_operands(qa, kvw, qr, kr, vr, g, cache_t, state, tap_bias):
    B, _, _, _, W = cache_t.shape
    assert W == MAX_WINDOW and PAST_LEN >= MAX_WINDOW
    row3 = lambda a: a.reshape(B, 1, a.shape[-1])
    vec = lambda width: pl.BlockSpec((None, 1, width), lambda b: (b, 0, 0))
    bias0 = jnp.broadcast_to(tap_bias[0, :, 0:1], (H_ATT, LANES))
    st_spec = pl.BlockSpec((None, H_RET, DK_RET, DV_RET), lambda b: (b, 0, 0, 0))
    in_specs = [vec(W_ATT), vec(2 * W_ATT), vec(W_RET), vec(W_RET), vec(W_RET), vec(W_RET),
                pl.BlockSpec((None, 2, H_ATT, HD_ATT, W), lambda b: (b, 0, 0, 0, 0)),
                st_spec,
                pl.BlockSpec((len(DILATIONS), H_ATT, W), lambda b: (0, 0, 0)),
                pl.BlockSpec((H_ATT, LANES), lambda b: (0, 0))]
    args = (row3(qa), row3(kvw), row3(qr), row3(kr), row3(vr), row3(g), cache_t, state,
            _window_bias(tap_bias, W), bias0)
    out_specs = [vec(W_ATT), vec(W_RET), st_spec]
    out_shape = [jax.ShapeDtypeStruct((B, 1, W_ATT), BF16), jax.ShapeDtypeStruct((B, 1, W_RET), BF16),
                 jax.ShapeDtypeStruct(state.shape, F32)]
    return in_specs, args, out_specs, out_shape


FF_CHUNK = 1024


def _layernorm(z, g, b):
    mu = jnp.mean(z, axis=1, keepdims=True)
    var = jnp.mean(jnp.square(z - mu), axis=1, keepdims=True)
    return (z - mu) * lax.rsqrt(var + LN_EPS) * g + b


def _out_ffn_body(x_ref, oatt_ref, yret_ref, wo_ref, g1_ref, b1_ref, wu_ref, wd_ref, g2_ref, b2_ref, y_ref):
    mix = (jnp.dot(oatt_ref[...], wo_ref[:W_ATT, :], preferred_element_type=F32)
           + jnp.dot(yret_ref[...], wo_ref[W_ATT:, :], preferred_element_type=F32))
    x1 = _layernorm(ALPHA * x_ref[...] + mix, g1_ref[...], b1_ref[...])
    x1b = x1.astype(BF16)
    ffn = jnp.zeros_like(x1)
    for j in range(D_FF // FF_CHUNK):
        cols = slice(j * FF_CHUNK, (j + 1) * FF_CHUNK)
        h = jnp.square(jnp.maximum(jnp.dot(x1b, wu_ref[:, cols], preferred_element_type=F32), 0.0))
        ffn = ffn + jnp.dot(h.astype(BF16), wd_ref[cols, :], preferred_element_type=F32)
    y_ref[...] = _layernorm(ALPHA * x1 + ffn, g2_ref[...], b2_ref[...])


def _ffn_operands(x, oatt, yret, wo_b, g1, b1, wu_b, wd_b, g2, b2, tm):
    row = lambda i: (i, 0)
    const = lambda shape: pl.BlockSpec(shape, lambda i: (0, 0), pipeline_mode=pl.Buffered(1))
    in_specs = [pl.BlockSpec((tm, D_MODEL), row), pl.BlockSpec((tm, W_ATT), row), pl.BlockSpec((tm, W_RET), row),
                const((W_ATT + W_RET, D_MODEL)), const((1, D_MODEL)), const((1, D_MODEL)),
                const((D_MODEL, D_FF)), const((D_FF, D_MODEL)), const((1, D_MODEL)), const((1, D_MODEL))]
    args = (x, oatt, yret, wo_b, g1, b1, wu_b, wd_b, g2, b2)
    return in_specs, args, pl.BlockSpec((tm, D_MODEL), row), jax.ShapeDtypeStruct((x.shape[0], D_MODEL), F32)


def _out_ffn(ffn_args, tm, name):
    in_specs, args, out_spec, out_shape = _ffn_operands(*ffn_args, tm)
    return pl.pallas_call(
        _out_ffn_body, grid=(args[0].shape[0] // tm,), in_specs=in_specs, out_specs=out_spec, out_shape=out_shape,
        compiler_params=_cparams(1), name=name,
    )(*args)


def _ffn_and_mixer_body(*refs, n_ffn_in, n_mix_in):
    ffn_in, mix_in = refs[:n_ffn_in], refs[n_ffn_in:n_ffn_in + n_mix_in]
    y_ref, *mix_out = refs[n_ffn_in + n_mix_in:]
    _sample_mixer_body(*mix_in, *mix_out)
    _out_ffn_body(*ffn_in, y_ref)


def _out_ffn_prompt_and_sample_mixer(ffn_args, mixer_args):
    B = mixer_args[0].shape[0]
    tm = ffn_args[0].shape[0] // B
    f_in, f_args, f_out, f_shape = _ffn_operands(*ffn_args, tm)
    m_in, m_args, m_out, m_shape = _mixer_operands(*mixer_args)
    y, oatt, yret, st = pl.pallas_call(
        functools.partial(_ffn_and_mixer_body, n_ffn_in=len(f_in), n_mix_in=len(m_in)),
        grid=(B,), in_specs=f_in + m_in, out_specs=[f_out] + m_out, out_shape=[f_shape] + m_shape,
        compiler_params=_cparams(1), name="out_ffn_prompt_sample_mixer",
    )(*f_args, *m_args)
    return y, oatt.reshape(B, W_ATT), yret.reshape(B, W_RET), st


PROMPT_ROWS = 512


def kernel(x_prompt, x_sample, cache_kv_win, state_ret, w_in, rel_bias, w_out,
           ln1_g, ln1_b, w_up, w_down, ln2_g, ln2_b):
    assert x_prompt.shape[0] == 1 and x_sample.shape[1] == 1 and w_in.shape[0] == DEPTH
    S = x_prompt.shape[1]
    B = x_sample.shape[0]
    w_in_b, wo_b = w_in[0].astype(BF16), w_out[0].astype(BF16)
    wu_b, wd_b = w_up[0].astype(BF16), w_down[0].astype(BF16)
    g1, b1, g2, b2 = ln1_g[0][None], ln1_b[0][None], ln2_g[0][None], ln2_b[0][None]
    tap_bias = _tap_bias(rel_bias)

    xp = x_prompt[0]
    *qkv_views, kvw_p, yret, st_p = _inproj_prompt(xp, w_in_b, PROMPT_ROWS)
    oatt = _attn_prompt(*qkv_views, _band_bias(tap_bias))

    xs = x_sample[:, 0]
    qa_s, kvw_s, qr_s, kr_s, vr_s, g_s = _inproj_sample(xs, w_in_b)
    cache_t = jnp.transpose(cache_kv_win[0], (0, 2, 3, 4, 1))
    ffn_weights = (wo_b, g1, b1, wu_b, wd_b, g2, b2)
    y_p, oatt_s, yret_s, st_s = _out_ffn_prompt_and_sample_mixer(
        (xp, oatt, yret) + ffn_weights, (qa_s, kvw_s, qr_s, kr_s, vr_s, g_s, cache_t, state_ret[0], tap_bias))
    y_s = _out_ffn((xs, oatt_s, yret_s) + ffn_weights, B, "out_ffn_sample")

    win = kvw_p.shape[1]
    kv_win_p = jnp.transpose(kvw_p.reshape(2, H_ATT, HD_ATT, win), (3, 0, 1, 2))
    return (y_p[None], y_s[:, None],
            kv_win_p[None, None], kvw_s.reshape(1, B, 1, 2, H_ATT, HD_ATT),
            st_p[None, None], st_s[None])
```

```python
import functools
import math

import jax
import jax.numpy as jnp
from jax import lax
from jax.experimental import pallas as pl
from jax.experimental.pallas import tpu as pltpu

F32 = jnp.float32
BF16 = jnp.bfloat16

D_MODEL = 1024
DEPTH = 1
PAST_LEN = 16384
W_ATT = 512
HD_ATT = 64
H_ATT = 8
DILATIONS = ((128, 1), (512, 4), (2048, 16))
TAPS = 128
MAX_WINDOW = 2048
N_BUCKETS = 32
MAX_EXACT = N_BUCKETS // 2
W_RET = 512
H_RET = 4
DK_RET = 128
DV_RET = 128
RET_CHUNK = 128
ROPE_BASE = 10000.0
D_FF = 4096
N_SPLITS = 7
ALPHA = (2.0 * DEPTH) ** 0.25
LN_EPS = 1e-5
GN_EPS = 1e-6
LOG_GAMMA = tuple(math.log(1.0 - 2.0 ** (-5.0 - h)) for h in range(H_RET))

LANES = 128
PAIRS = W_ATT // LANES
NEG = -1e30

ATT_BLOCK = 2048
Q_ROWS = 128
VMEM_LIMIT = 56 * 1024 * 1024


def _cparams(n_axes):
    return pltpu.CompilerParams(dimension_semantics=("arbitrary",) * n_axes, vmem_limit_bytes=VMEM_LIMIT)


def _rotary(u, cos, sin_signed):
    outs = []
    for h in range(H_RET):
        xh = u[:, h * LANES:(h + 1) * LANES]
        outs.append(xh * cos + pltpu.roll(xh, LANES // 2, 1) * sin_signed)
    return jnp.concatenate(outs, axis=1)


def _proj(xb, w_ref, j):
    return jnp.dot(xb, w_ref[:, j * W_ATT:(j + 1) * W_ATT], preferred_element_type=F32)


def _store_rows(val, rows, nat_ref, rows_scr):
    for p in range(PAIRS):
        x = val[:, p * LANES:(p + 1) * LANES]
        nat_ref[p, rows, :] = x.astype(BF16)
        rows_scr[p, rows, :] = x


def _store_gathered(by4_ref, by16_ref, rows_scr, by4_scr):
    tm = rows_scr.shape[1]
    for p in range(PAIRS):
        for r4 in range(4):
            a = rows_scr[p, pl.ds(r4, tm // 4, stride=4), :]
            by4_ref[p, :, r4 * LANES:(r4 + 1) * LANES] = a.astype(BF16)
            by4_scr[p * 4 + r4] = a
    for p in range(PAIRS):
        for r4 in range(4):
            for rr in range(4):
                r16 = r4 + 4 * rr
                b = by4_scr[p * 4 + r4, pl.ds(rr, tm // 16, stride=4), :]
                by16_ref[p, :, r16 * LANES:(r16 + 1) * LANES] = b.astype(BF16)


def _retention_qkvg(xb, w_ref, cos, sin):
    return (_rotary(_proj(xb, w_ref, 3), cos, sin), _rotary(_proj(xb, w_ref, 4), cos, sin) * DK_RET ** -0.5,
            _proj(xb, w_ref, 5), _proj(xb, w_ref, 6))


def _inproj_prompt_body(x_ref, w_ref, cos_row_ref, sin_row_ref, cos_step_ref, sin_step_ref,
                        q1_ref, q4_ref, q16_ref, k1_ref, k4_ref, k16_ref, v1_ref, v4_ref, v16_ref, kvw_ref,
                        yret_ref, st_ref, q_rows, q_by4, k_rows, k_by4, v_rows, v_by4, state, *, win_first_step):
    @pl.when(pl.program_id(0) == 0)
    def _():
        state[...] = jnp.zeros_like(state)

    xb = x_ref[...].astype(BF16)
    tm = xb.shape[0]
    half = tm // 2

    step = pl.ds(pl.program_id(0), 1)
    ca, sa = cos_step_ref[step, :], sin_step_ref[step, :]
    cb, sb = cos_row_ref[...], sin_row_ref[...]
    sign = jnp.where(lax.broadcasted_iota(jnp.int32, (1, LANES), 1) < LANES // 2, -1.0, 1.0)
    cos, sin = ca * cb - sa * sb, (sa * cb + ca * sb) * sign

    def retention_proj(j, rows):
        u = _proj(xb[rows], w_ref, 3 + j)
        if j < 2:
            u = _rotary(u, cos[rows], sin[rows])
        return u * DK_RET ** -0.5 if j == 1 else u

    attn_out = ((q1_ref, q4_ref, q16_ref, q_rows, q_by4), (k1_ref, k4_ref, k16_ref, k_rows, k_by4),
                (v1_ref, v4_ref, v16_ref, v_rows, v_by4))

    def attention_proj(j, rows, last):
        nat_ref, by4_ref, by16_ref, rows_scr, by4_scr = attn_out[j]
        val = _proj(xb[rows], w_ref, j)
        _store_rows(val * HD_ATT ** -0.5 if j == 0 else val, rows, nat_ref, rows_scr)
        if last:
            _store_gathered(by4_ref, by16_ref, rows_scr, by4_scr)

    halves = (slice(0, half), slice(half, tm))
    ret_in = [[retention_proj(j, halves[0]) for j in range(4)], [None] * 4]
    fills = [[("ret", 0), ("ret", 1), ("ret", 2), ("ret", 3), ("att", 0, 0)],
             [("att", 0, 1), ("att", 1, 0), ("att", 1, 1), ("att", 2, 0), ("att", 2, 1)]]
    decays = [_retention_decays(h) for h in range(H_RET)]
    S = [state[h] for h in range(H_RET)]

    def emit(fill):
        if fill[0] == "ret":
            ret_in[1][fill[1]] = retention_proj(fill[1], halves[1])
        else:
            attention_proj(fill[1], halves[fill[2]], last=fill[2] == 1)

    for ph in range(2):
        qr, kr, vr, g = ret_in[ph]
        pending = list(fills[ph])
        for c in range(half // RET_CHUNK):
            rows = slice(c * RET_CHUNK, (c + 1) * RET_CHUNK)
            out_rows = slice(ph * half + c * RET_CHUNK, ph * half + (c + 1) * RET_CHUNK)
            staged = []
            for h in range(H_RET):
                cols = slice(h * DK_RET, (h + 1) * DK_RET)
                scores, cross, vb, S[h] = _retention_chunk_matmuls(qr[rows, cols], kr[rows, cols], vr[rows, cols],
                                                                  S[h], decays[h])
                staged.append((scores, cross, vb))
            emit(pending.pop(0))
            for h in range(H_RET):
                cols = slice(h * DK_RET, (h + 1) * DK_RET)
                yret_ref[out_rows, cols] = _retention_chunk_output(*staged[h], g[rows, cols]).astype(BF16)
            emit(pending.pop(0))
        while pending:
            emit(pending.pop(0))
    for h in range(H_RET):
        state[h] = S[h]
        st_ref[h] = S[h]

    @pl.when(pl.program_id(0) >= win_first_step)
    def _():
        for p in range(PAIRS):
            kvw_ref[p * LANES:(p + 1) * LANES, :] = k_rows[p].T
            kvw_ref[W_ATT + p * LANES:W_ATT + (p + 1) * LANES, :] = v_rows[p].T


def _inproj_sample_body(x_ref, w_ref, cos_ref, sin_ref, qa_ref, kvw_ref, qr_ref, kr_ref, vr_ref, g_ref):
    xb = x_ref[...].astype(BF16)
    qa_ref[...] = _proj(xb, w_ref, 0) * HD_ATT ** -0.5
    kvw_ref[:, :W_ATT] = _proj(xb, w_ref, 1)
    kvw_ref[:, W_ATT:] = _proj(xb, w_ref, 2)
    qr, kr, vr, g = _retention_qkvg(xb, w_ref, cos_ref[...], sin_ref[...])
    qr_ref[...] = qr.astype(BF16)
    kr_ref[...] = kr.astype(BF16)
    vr_ref[...] = vr.astype(BF16)
    g_ref[...] = g


def _rope_tables(pos):
    half = DK_RET // 2
    inv_freq = 1.0 / (ROPE_BASE ** jnp.linspace(0.0, 1.0, half, dtype=F32))
    ang = pos.astype(F32)[:, None] * inv_freq[None, :]
    cos, sin = jnp.cos(ang), jnp.sin(ang)
    return jnp.concatenate([cos, cos], axis=1), jnp.concatenate([sin, sin], axis=1)


def _inproj_prompt(x, w_in_b, tm):
    S = x.shape[0]
    win = min(MAX_WINDOW, S)
    steps = S // tm
    win_first_step = steps - win // tm
    cos, sin = _rope_tables(jnp.concatenate([jnp.arange(tm, dtype=jnp.int32),
                                             jnp.arange(steps, dtype=jnp.int32) * tm]))
    const = lambda i: (0, 0)
    row = lambda i: (i, 0)
    view_specs = [pl.BlockSpec((PAIRS, tm // d, d * LANES), lambda i: (0, i, 0)) for d in (1, 4, 16)] * 3
    view_shapes = [jax.ShapeDtypeStruct((PAIRS, S // d, d * LANES), BF16) for d in (1, 4, 16)] * 3
    state_shape = (H_RET, DK_RET, DV_RET)
    gather_scratch = [pltpu.VMEM((PAIRS, tm, LANES), F32), pltpu.VMEM((4 * PAIRS, tm // 4, LANES), F32)]
    return pl.pallas_call(
        functools.partial(_inproj_prompt_body, win_first_step=win_first_step),
        grid=(steps,),
        in_specs=[pl.BlockSpec((tm, D_MODEL), row),
                  pl.BlockSpec((D_MODEL, N_SPLITS * W_ATT), lambda i: (0, 0)),
                  pl.BlockSpec((tm, LANES), const), pl.BlockSpec((tm, LANES), const),
                  pl.BlockSpec((steps, LANES), const), pl.BlockSpec((steps, LANES), const)],
        out_specs=view_specs + [
            pl.BlockSpec((2 * W_ATT, tm), lambda i: (0, jnp.maximum(i - win_first_step, 0))),
            pl.BlockSpec((tm, W_RET), row),
            pl.BlockSpec(state_shape, lambda i: (0, 0, 0))],
        out_shape=view_shapes + [
            jax.ShapeDtypeStruct((2 * W_ATT, win), F32),
            jax.ShapeDtypeStruct((S, W_RET), BF16),
            jax.ShapeDtypeStruct(state_shape, F32)],
        scratch_shapes=gather_scratch * 3 + [pltpu.VMEM(state_shape, F32)],
        compiler_params=_cparams(1),
        name="inproj_retention_prompt",
    )(x, w_in_b, cos[:tm], sin[:tm], cos[tm:], sin[tm:])


def _inproj_sample(x, w_in_b):
    B = x.shape[0]
    cos, sin = _rope_tables(jnp.full((B,), PAST_LEN, dtype=jnp.int32))
    sin = sin * jnp.where(jnp.arange(LANES) < LANES // 2, -1.0, 1.0)
    full = lambda shape: pl.BlockSpec(shape, lambda i: (0, 0))
    return pl.pallas_call(
        _inproj_sample_body,
        grid=(1,),
        in_specs=[full((B, D_MODEL)), full((D_MODEL, N_SPLITS * W_ATT)), full((B, LANES)), full((B, LANES))],
        out_specs=[full((B, W_ATT)), full((B, 2 * W_ATT)), full((B, W_RET)), full((B, W_RET)),
                   full((B, W_RET)), full((B, W_RET))],
        out_shape=[jax.ShapeDtypeStruct((B, W_ATT), F32),
                   jax.ShapeDtypeStruct((B, 2 * W_ATT), F32),
                   jax.ShapeDtypeStruct((B, W_RET), BF16),
                   jax.ShapeDtypeStruct((B, W_RET), BF16),
                   jax.ShapeDtypeStruct((B, W_RET), BF16),
                   jax.ShapeDtypeStruct((B, W_RET), F32)],
        compiler_params=_cparams(1),
        name="inproj_sample",
    )(x, w_in_b, cos, sin)


def _t5_bucket(dist):
    is_small = dist < MAX_EXACT
    d_f = jnp.maximum(dist, 1).astype(F32)
    large = MAX_EXACT + (jnp.log(d_f / MAX_EXACT) / math.log(MAX_WINDOW / MAX_EXACT)
                         * (N_BUCKETS - MAX_EXACT)).astype(jnp.int32)
    large = jnp.minimum(large, N_BUCKETS - 1)
    return jnp.where(is_small, dist, large)


def _tap_bias(rel_bias):
    taps = jnp.arange(TAPS + 1)
    return jnp.stack([rel_bias[_t5_bucket(taps * d)].astype(F32).T for _, d in DILATIONS])


def _band_bias(tap_bias):
    n = Q_ROWS + 2 * Q_ROWS
    w = jnp.concatenate([tap_bias[..., ::-1], jnp.full(tap_bias.shape[:-1] + (n - TAPS - 1,), NEG, F32)], axis=-1)
    skew = jnp.tile(w, Q_ROWS)[..., :Q_ROWS * (n - 1)].reshape(tap_bias.shape[:-1] + (Q_ROWS, n - 1))
    return skew[..., :2 * Q_ROWS]


def _window_bias(tap_bias, window):
    rows = []
    for i, (_, d) in enumerate(DILATIONS):
        taps_rev = tap_bias[i, :, TAPS:0:-1]
        seg = jnp.concatenate([taps_rev[..., None], jnp.full((H_ATT, TAPS, d - 1), NEG, F32)], axis=-1)
        rows.append(jnp.concatenate([jnp.full((H_ATT, window - TAPS * d), NEG, F32),
                                     seg.reshape(H_ATT, TAPS * d)], axis=-1))
    return jnp.stack(rows)


def _attn_unit(q, k, v, bias2, head0_b, head0):
    qs = jnp.concatenate([q * head0_b, q * (1 - head0_b)], axis=0)
    s = lax.dot_general(qs, k, (((1,), (1,)), ((), ())), preferred_element_type=F32) + bias2
    m = jnp.max(s, axis=1, keepdims=True)
    p = jnp.exp(s - m)
    l = jnp.sum(p, axis=1, keepdims=True)
    o = jnp.dot(p.astype(BF16), v, preferred_element_type=F32) / l
    lse = m + jnp.log(l)
    o_pair = jnp.where(head0, o[:Q_ROWS], o[Q_ROWS:])
    lse_pair = jnp.where(head0, jnp.broadcast_to(lse[:Q_ROWS], (Q_ROWS, LANES)),
                         jnp.broadcast_to(lse[Q_ROWS:], (Q_ROWS, LANES)))
    return o_pair, lse_pair


def _attn_body(q1, q4, q16, k1c, k1p, k4c, k4p, k16c, k16p, v1c, v1p, v4c, v4p, v16c, v16p, bias_ref,
               out_ref, o16, l16, o4, l4):
    first = pl.program_id(0) == 0
    lane = lax.broadcasted_iota(jnp.int32, (Q_ROWS, LANES), 1)
    head0 = lane < HD_ATT
    head0_b = jnp.where(head0, 1.0, 0.0).astype(BF16)
    col = lax.broadcasted_iota(jnp.int32, (2 * Q_ROWS, 2 * Q_ROWS), 1)
    no_prev = jnp.where(jnp.logical_and(first, col < Q_ROWS), NEG, 0.0)

    def bias2(branch, masked_prev):
        b = jnp.concatenate([bias_ref[branch, 0], bias_ref[branch, 1]], axis=0)
        return b + no_prev if masked_prev else b

    def cat(a, b):
        return jnp.concatenate([a, b], axis=0)

    b16 = bias2(2, True)
    for r in range(16):
        sl = slice(r * LANES, (r + 1) * LANES)
        o, l = _attn_unit(q16[:, sl], cat(k16p[:, sl], k16c[:, sl]), cat(v16p[:, sl], v16c[:, sl]),
                          b16, head0_b, head0)
        o16[pl.ds(r, Q_ROWS, stride=16), :] = o
        l16[pl.ds(r, Q_ROWS, stride=16), :] = l

    b4_first = bias2(1, True)
    b4 = bias2(1, False)
    for r in range(4):
        sl = slice(r * LANES, (r + 1) * LANES)
        for b in range(4):
            rows = slice(b * Q_ROWS, (b + 1) * Q_ROWS)
            if b == 0:
                k = cat(k4p[:, sl], k4c[rows, sl])
                v = cat(v4p[:, sl], v4c[rows, sl])
            else:
                both = slice((b - 1) * Q_ROWS, (b + 1) * Q_ROWS)
                k, v = k4c[both, sl], v4c[both, sl]
            o, l = _attn_unit(q4[rows, sl], k, v, b4_first if b == 0 else b4, head0_b, head0)
            o4[pl.ds(b * 4 * Q_ROWS + r, Q_ROWS, stride=4), :] = o
            l4[pl.ds(b * 4 * Q_ROWS + r, Q_ROWS, stride=4), :] = l

    b1_first = bias2(0, True)
    b1 = bias2(0, False)
    for b in range(ATT_BLOCK // Q_ROWS):
        rows = slice(b * Q_ROWS, (b + 1) * Q_ROWS)
        if b == 0:
            k = cat(k1p[...], k1c[rows, :])
            v = cat(v1p[...], v1c[rows, :])
        else:
            both = slice((b - 1) * Q_ROWS, (b + 1) * Q_ROWS)
            k, v = k1c[both, :], v1c[both, :]
        oa, la = _attn_unit(q1[rows, :], k, v, b1_first if b == 0 else b1, head0_b, head0)
        ob, lb = o4[rows, :], l4[rows, :]
        oc, lc = o16[rows, :], l16[rows, :]
        top = jnp.maximum(jnp.maximum(la, lb), lc)
        ea, eb, ec = jnp.exp(la - top), jnp.exp(lb - top), jnp.exp(lc - top)
        out_ref[rows, :] = ((ea * oa + eb * ob + ec * oc) / (ea + eb + ec)).astype(BF16)


def _attn_prompt(q1, q4, q16, k1, k4, k16, v1, v4, v16, band_bias):
    S = q1.shape[1]
    assert S % ATT_BLOCK == 0
    steps = S // ATT_BLOCK

    def cur(d):
        return pl.BlockSpec((None, ATT_BLOCK // d, d * LANES), lambda c, p: (p, c, 0))

    def prev(d):
        per_step = ATT_BLOCK // d // Q_ROWS
        return pl.BlockSpec((None, Q_ROWS, d * LANES), lambda c, p: (p, jnp.maximum(c * per_step - 1, 0), 0))

    scratch = pltpu.VMEM((ATT_BLOCK, LANES), F32)
    return pl.pallas_call(
        _attn_body,
        grid=(steps, PAIRS),
        in_specs=[cur(1), cur(4), cur(16),
                  cur(1), prev(1), cur(4), prev(4), cur(16), prev(16),
                  cur(1), prev(1), cur(4), prev(4), cur(16), prev(16),
                  pl.BlockSpec((len(DILATIONS), 2, Q_ROWS, 2 * Q_ROWS), lambda c, p: (0, p, 0, 0))],
        out_specs=pl.BlockSpec((ATT_BLOCK, LANES), lambda c, p: (c, p)),
        out_shape=jax.ShapeDtypeStruct((S, W_ATT), BF16),
        scratch_shapes=[scratch, scratch, scratch, scratch],
        compiler_params=_cparams(2),
        name="attn_prompt",
    )(q1, q4, q16, k1, k1, k4, k4, k16, k16, v1, v1, v4, v4, v16, v16, band_bias)


def _group_norm_gate(o, g):
    mu = jnp.mean(o, axis=1, keepdims=True)
    var = jnp.mean(jnp.square(o - mu), axis=1, keepdims=True)
    return g * jax.nn.sigmoid(g) * ((o - mu) * lax.rsqrt(var + GN_EPS))


def _retention_decays(h):
    C = RET_CHUNK
    lg = LOG_GAMMA[h]
    n = lax.broadcasted_iota(jnp.int32, (C, 1), 0).astype(F32)
    diff = (lax.broadcasted_iota(jnp.int32, (C, C), 0) - lax.broadcasted_iota(jnp.int32, (C, C), 1)).astype(F32)
    return (jnp.where(diff >= 0, jnp.exp(lg * jnp.maximum(diff, 0.0)), 0.0),
            jnp.exp(lg * (n + 1.0)), jnp.exp(lg * (C - 1.0 - n)), math.exp(lg * C))


def _retention_chunk_matmuls(q, k, v, S, decays):
    decay, q_decay, k_decay, chunk_decay = decays
    qb, vb = q.astype(BF16), v.astype(BF16)
    scores = lax.dot_general(qb, k.astype(BF16), (((1,), (1,)), ((), ())), preferred_element_type=F32) * decay
    cross = jnp.dot(qb, S.astype(BF16), preferred_element_type=F32) * q_decay
    k_dec_t = (k * k_decay).T.astype(BF16)
    return scores.astype(BF16), cross, vb, chunk_decay * S + jnp.dot(k_dec_t, vb, preferred_element_type=F32)


def _retention_chunk_output(scores, cross, vb, g):
    return _group_norm_gate(jnp.dot(scores, vb, preferred_element_type=F32) + cross, g)


def _as_column(row_pair):
    return jnp.broadcast_to(row_pair, (LANES, LANES)).T


def _sample_mixer_body(qa_ref, kvw_ref, qr_ref, kr_ref, vr_ref, g_ref, cache_ref, st_ref, bias_ref, bias0_ref,
                       oatt_ref, yret_ref, stout_ref):
    W = cache_ref.shape[-1]
    logit_rows, self_rows = [], []
    for p in range(PAIRS):
        lanes = slice(p * LANES, (p + 1) * LANES)
        q_pair = qa_ref[:, lanes]
        k_pair = kvw_ref[:, lanes]
        q_col = _as_column(q_pair)
        for hh in range(2):
            feat = slice(hh * HD_ATT, (hh + 1) * HD_ATT)
            logit_rows.append(jnp.sum(cache_ref[0, 2 * p + hh] * q_col[feat, 0:1], axis=0, keepdims=True))
            self_rows.append(jnp.sum(q_pair[:, feat] * k_pair[:, feat], axis=1, keepdims=True))
    logits = jnp.concatenate(logit_rows, axis=0)
    s_self = jnp.concatenate(self_rows, axis=0) + bias0_ref[:, 0:1]

    probs, p_selfs, denoms, lses = [], [], [], []
    for i, (_, d) in enumerate(DILATIONS):
        lo = W - TAPS * d
        s = logits[:, lo:] + bias_ref[i, :, lo:]
        m = jnp.maximum(jnp.max(s, axis=1, keepdims=True), s_self)
        pr = jnp.exp(s - m)
        p_self = jnp.exp(s_self - m)
        l = jnp.sum(pr, axis=1, keepdims=True) + p_self
        probs.append(pr)
        p_selfs.append(p_self)
        denoms.append(l)
        lses.append(m + jnp.log(l))
    top = jnp.maximum(jnp.maximum(lses[0], lses[1]), lses[2])
    e = [jnp.exp(x - top) for x in lses]
    e_sum = e[0] + e[1] + e[2]
    coef = [e[i] / (e_sum * denoms[i]) for i in range(3)]
    lo4, lo1 = W - TAPS * 4, W - TAPS
    p16, p4, p1 = coef[2] * probs[2], coef[1] * probs[1], coef[0] * probs[0]
    p_all = jnp.concatenate([p16[:, :lo4], p16[:, lo4:lo1] + p4[:, :lo1 - lo4],
                             p16[:, lo1:] + p4[:, lo1 - lo4:] + p1], axis=1)
    c_self = coef[0] * p_selfs[0] + coef[1] * p_selfs[1] + coef[2] * p_selfs[2]
    for p in range(PAIRS):
        lanes = slice(p * LANES, (p + 1) * LANES)
        v_col = _as_column(kvw_ref[:, W_ATT + p * LANES:W_ATT + (p + 1) * LANES])
        out_cols = []
        for hh in range(2):
            h = 2 * p + hh
            feat = slice(hh * HD_ATT, (hh + 1) * HD_ATT)
            pv = jnp.sum(cache_ref[1, h] * p_all[h:h + 1, :], axis=1, keepdims=True)
            out_cols.append(pv + c_self[h:h + 1, :] * v_col[feat, 0:1])
        col = jnp.broadcast_to(jnp.concatenate(out_cols, axis=0), (LANES, LANES))
        oatt_ref[:, lanes] = col.T[0:1, :].astype(BF16)

    for h in range(H_RET):
        cols = slice(h * DK_RET, (h + 1) * DK_RET)
        gamma = math.exp(LOG_GAMMA[h])
        qh = qr_ref[:, cols]
        kh = kr_ref[:, cols].astype(F32)
        vh = vr_ref[:, cols].astype(F32)
        S = st_ref[h]
        qk = jnp.sum(qh.astype(F32) * kh, axis=1, keepdims=True)
        cross = jnp.dot(jnp.broadcast_to(qh, (16, DK_RET)), S.astype(BF16), preferred_element_type=F32)[:1]
        o = qk * vh + cross * gamma
        stout_ref[h] = gamma * S + _as_column(kh) * vh
        yret_ref[:, cols] = _group_norm_gate(o, g_ref[:, cols]).astype(BF16)


def _mixer_operands(qa, kvw, qr, kr, vr, g, cache_t, state, tap_bias):
    B, _, _, _, W = cache_t.shape
    assert W == MAX_WINDOW and PAST_LEN >= MAX_WINDOW
    row3 = lambda a: a.reshape(B, 1, a.shape[-1])
    vec = lambda width: pl.BlockSpec((None, 1, width), lambda b: (b, 0, 0))
    bias0 = jnp.broadcast_to(tap_bias[0, :, 0:1], (H_ATT, LANES))
    st_spec = pl.BlockSpec((None, H_RET, DK_RET, DV_RET), lambda b: (b, 0, 0, 0))
    in_specs = [vec(W_ATT), vec(2 * W_ATT), vec(W_RET), vec(W_RET), vec(W_RET), vec(W_RET),
                pl.BlockSpec((None, 2, H_ATT, HD_ATT, W), lambda b: (b, 0, 0, 0, 0)),
                st_spec,
                pl.BlockSpec((len(DILATIONS), H_ATT, W), lambda b: (0, 0, 0)),
                pl.BlockSpec((H_ATT, LANES), lambda b: (0, 0))]
    args = (row3(qa), row3(kvw), row3(qr), row3(kr), row3(vr), row3(g), cache_t, state,
            _window_bias(tap_bias, W), bias0)
    out_specs = [vec(W_ATT), vec(W_RET), st_spec]
    out_shape = [jax.ShapeDtypeStruct((B, 1, W_ATT), BF16), jax.ShapeDtypeStruct((B, 1, W_RET), BF16),
                 jax.ShapeDtypeStruct(state.shape, F32)]
    return in_specs, args, out_specs, out_shape


FF_CHUNK = 1024


def _layernorm(z, g, b):
    mu = jnp.mean(z, axis=1, keepdims=True)
    var = jnp.mean(jnp.square(z - mu), axis=1, keepdims=True)
    return (z - mu) * lax.rsqrt(var + LN_EPS) * g + b


def _out_ffn_body(x_ref, oatt_ref, yret_ref, wo_ref, g1_ref, b1_ref, wu_ref, wd_ref, g2_ref, b2_ref, y_ref):
    mix = (jnp.dot(oatt_ref[...], wo_ref[:W_ATT, :], preferred_element_type=F32)
           + jnp.dot(yret_ref[...], wo_ref[W_ATT:, :], preferred_element_type=F32))
    x1 = _layernorm(ALPHA * x_ref[...] + mix, g1_ref[...], b1_ref[...])
    x1b = x1.astype(BF16)
    ffn = jnp.zeros_like(x1)
    for j in range(D_FF // FF_CHUNK):
        cols = slice(j * FF_CHUNK, (j + 1) * FF_CHUNK)
        h = jnp.square(jnp.maximum(jnp.dot(x1b, wu_ref[:, cols], preferred_element_type=F32), 0.0))
        ffn = ffn + jnp.dot(h.astype(BF16), wd_ref[cols, :], preferred_element_type=F32)
    y_ref[...] = _layernorm(ALPHA * x1 + ffn, g2_ref[...], b2_ref[...])


def _ffn_operands(x, oatt, yret, wo_b, g1, b1, wu_b, wd_b, g2, b2, tm):
    row = lambda i: (i, 0)
    const = lambda shape: pl.BlockSpec(shape, lambda i: (0, 0), pipeline_mode=pl.Buffered(1))
    in_specs = [pl.BlockSpec((tm, D_MODEL), row), pl.BlockSpec((tm, W_ATT), row), pl.BlockSpec((tm, W_RET), row),
                const((W_ATT + W_RET, D_MODEL)), const((1, D_MODEL)), const((1, D_MODEL)),
                const((D_MODEL, D_FF)), const((D_FF, D_MODEL)), const((1, D_MODEL)), const((1, D_MODEL))]
    args = (x, oatt, yret, wo_b, g1, b1, wu_b, wd_b, g2, b2)
    return in_specs, args, pl.BlockSpec((tm, D_MODEL), row), jax.ShapeDtypeStruct((x.shape[0], D_MODEL), F32)


def _out_ffn(ffn_args, tm, name):
    in_specs, args, out_spec, out_shape = _ffn_operands(*ffn_args, tm)
    return pl.pallas_call(
        _out_ffn_body, grid=(args[0].shape[0] // tm,), in_specs=in_specs, out_specs=out_spec, out_shape=out_shape,
        compiler_params=_cparams(1), name=name,
    )(*args)


def _ffn_and_mixer_body(*refs, n_ffn_in, n_mix_in):
    ffn_in, mix_in = refs[:n_ffn_in], refs[n_ffn_in:n_ffn_in + n_mix_in]
    y_ref, *mix_out = refs[n_ffn_in + n_mix_in:]
    _sample_mixer_body(*mix_in, *mix_out)
    _out_ffn_body(*ffn_in, y_ref)


def _out_ffn_prompt_and_sample_mixer(ffn_args, mixer_args):
    B = mixer_args[0].shape[0]
    tm = ffn_args[0].shape[0] // B
    f_in, f_args, f_out, f_shape = _ffn_operands(*ffn_args, tm)
    m_in, m_args, m_out, m_shape = _mixer_operands(*mixer_args)
    y, oatt, yret, st = pl.pallas_call(
        functools.partial(_ffn_and_mixer_body, n_ffn_in=len(f_in), n_mix_in=len(m_in)),
        grid=(B,), in_specs=f_in + m_in, out_specs=[f_out] + m_out, out_shape=[f_shape] + m_shape,
        compiler_params=_cparams(1), name="out_ffn_prompt_sample_mixer",
    )(*f_args, *m_args)
    return y, oatt.reshape(B, W_ATT), yret.reshape(B, W_RET), st


PROMPT_ROWS = 512


def kernel(x_prompt, x_sample, cache_kv_win, state_ret, w_in, rel_bias, w_out,
           ln1_g, ln1_b, w_up, w_down, ln2_g, ln2_b):
    assert x_prompt.shape[0] == 1 and x_sample.shape[1] == 1 and w_in.shape[0] == DEPTH
    S = x_prompt.shape[1]
    B = x_sample.shape[0]
    w_in_b, wo_b = w_in[0].astype(BF16), w_out[0].astype(BF16)
    wu_b, wd_b = w_up[0].astype(BF16), w_down[0].astype(BF16)
    g1, b1, g2, b2 = ln1_g[0][None], ln1_b[0][None], ln2_g[0][None], ln2_b[0][None]
    tap_bias = _tap_bias(rel_bias)

    xp = x_prompt[0]
    *qkv_views, kvw_p, yret, st_p = _inproj_prompt(xp, w_in_b, PROMPT_ROWS)
    oatt = _attn_prompt(*qkv_views, _band_bias(tap_bias))

    xs = x_sample[:, 0]
    qa_s, kvw_s, qr_s, kr_s, vr_s, g_s = _inproj_sample(xs, w_in_b)
    cache_t = jnp.transpose(cache_kv_win[0], (0, 2, 3, 4, 1))
    ffn_weights = (wo_b, g1, b1, wu_b, wd_b, g2, b2)
    y_p, oatt_s, yret_s, st_s = _out_ffn_prompt_and_sample_mixer(
        (xp, oatt, yret) + ffn_weights, (qa_s, kvw_s, qr_s, kr_s, vr_s, g_s, cache_t, state_ret[0], tap_bias))
    y_s = _out_ffn((xs, oatt_s, yret_s) + ffn_weights, B, "out_ffn_sample")

    win = kvw_p.shape[1]
    kv_win_p = jnp.transpose(kvw_p.reshape(2, H_ATT, HD_ATT, win), (3, 0, 1, 2))
    return (y_p[None], y_s[:, None],
            kv_win_p[None, None], kvw_s.reshape(1, B, 1, 2, H_ATT, HD_ATT),
            st_p[None, None], st_s[None])
```

```python
import functools
import math

import jax
import jax.numpy as jnp
from jax import lax
from jax.experimental import pallas as pl
from jax.experimental.pallas import tpu as pltpu

F32 = jnp.float32
BF16 = jnp.bfloat16

D_MODEL = 1024
DEPTH = 1
PAST_LEN = 16384
W_ATT = 512
HD_ATT = 64
H_ATT = 8
DILATIONS = ((128, 1), (512, 4), (2048, 16))
TAPS = 128
MAX_WINDOW = 2048
N_BUCKETS = 32
MAX_EXACT = N_BUCKETS // 2
W_RET = 512
H_RET = 4
DK_RET = 128
DV_RET = 128
RET_CHUNK = 128
ROPE_BASE = 10000.0
D_FF = 4096
N_SPLITS = 7
ALPHA = (2.0 * DEPTH) ** 0.25
LN_EPS = 1e-5
GN_EPS = 1e-6
LOG_GAMMA = tuple(math.log(1.0 - 2.0 ** (-5.0 - h)) for h in range(H_RET))

LANES = 128
PAIRS = W_ATT // LANES
NEG = -1e30

ATT_BLOCK = 2048
Q_ROWS = 128
VMEM_LIMIT = 56 * 1024 * 1024


def _cparams(n_axes):
    return pltpu.CompilerParams(dimension_semantics=("arbitrary",) * n_axes, vmem_limit_bytes=VMEM_LIMIT)


def _rotary(u, cos, sin_signed):
    outs = []
    for h in range(H_RET):
        xh = u[:, h * LANES:(h + 1) * LANES]
        outs.append(xh * cos + pltpu.roll(xh, LANES // 2, 1) * sin_signed)
    return jnp.concatenate(outs, axis=1)


def _proj(xb, w_ref, j):
    return jnp.dot(xb, w_ref[:, j * W_ATT:(j + 1) * W_ATT], preferred_element_type=F32)


def _store_rows(val, rows, nat_ref, rows_scr):
    for p in range(PAIRS):
        x = val[:, p * LANES:(p + 1) * LANES]
        nat_ref[p, rows, :] = x.astype(BF16)
        rows_scr[p, rows, :] = x


def _store_gathered(by4_ref, by16_ref, rows_scr, by4_scr):
    tm = rows_scr.shape[1]
    for p in range(PAIRS):
        for r4 in range(4):
            a = rows_scr[p, pl.ds(r4, tm // 4, stride=4), :]
            by4_ref[p, :, r4 * LANES:(r4 + 1) * LANES] = a.astype(BF16)
            by4_scr[p * 4 + r4] = a
    for p in range(PAIRS):
        for r4 in range(4):
            for rr in range(4):
                r16 = r4 + 4 * rr
                b = by4_scr[p * 4 + r4, pl.ds(rr, tm // 16, stride=4), :]
                by16_ref[p, :, r16 * LANES:(r16 + 1) * LANES] = b.astype(BF16)


def _retention_qkvg(xb, w_ref, cos, sin):
    return (_rotary(_proj(xb, w_ref, 3), cos, sin), _rotary(_proj(xb, w_ref, 4), cos, sin) * DK_RET ** -0.5,
            _proj(xb, w_ref, 5), _proj(xb, w_ref, 6))


def _inproj_prompt_body(x_ref, w_ref, cos_row_ref, sin_row_ref, cos_step_ref, sin_step_ref,
                        q1_ref, q4_ref, q16_ref, k1_ref, k4_ref, k16_ref, v1_ref, v4_ref, v16_ref, kvw_ref,
                        yret_ref, st_ref, q_rows, q_by4, k_rows, k_by4, v_rows, v_by4, state, *, win_first_step):
    @pl.when(pl.program_id(0) == 0)
    def _():
        state[...] = jnp.zeros_like(state)

    xb = x_ref[...].astype(BF16)
    tm = xb.shape[0]
    half = tm // 2

    step = pl.ds(pl.program_id(0), 1)
    ca, sa = cos_step_ref[step, :], sin_step_ref[step, :]
    cb, sb = cos_row_ref[...], sin_row_ref[...]
    sign = jnp.where(lax.broadcasted_iota(jnp.int32, (1, LANES), 1) < LANES // 2, -1.0, 1.0)
    cos, sin = ca * cb - sa * sb, (sa * cb + ca * sb) * sign

    def retention_proj(j, rows):
        u = _proj(xb[rows], w_ref, 3 + j)
        if j < 2:
            u = _rotary(u, cos[rows], sin[rows])
        return u * DK_RET ** -0.5 if j == 1 else u

    attn_out = ((q1_ref, q4_ref, q16_ref, q_rows, q_by4), (k1_ref, k4_ref, k16_ref, k_rows, k_by4),
                (v1_ref, v4_ref, v16_ref, v_rows, v_by4))

    def attention_proj(j, rows, last):
        nat_ref, by4_ref, by16_ref, rows_scr, by4_scr = attn_out[j]
        val = _proj(xb[rows], w_ref, j)
        _store_rows(val * HD_ATT ** -0.5 if j == 0 else val, rows, nat_ref, rows_scr)
        if last:
            _store_gathered(by4_ref, by16_ref, rows_scr, by4_scr)

    halves = (slice(0, half), slice(half, tm))
    ret_in = [[retention_proj(j, halves[0]) for j in range(4)], [None] * 4]
    fills = [[("ret", 0), ("ret", 1), ("ret", 2), ("ret", 3), ("att", 0, 0)],
             [("att", 0, 1), ("att", 1, 0), ("att", 1, 1), ("att", 2, 0), ("att", 2, 1)]]
    decays = [_retention_decays(h) for h in range(H_RET)]
    S = [state[h] for h in range(H_RET)]

    def emit(fill):
        if fill[0] == "ret":
            ret_in[1][fill[1]] = retention_proj(fill[1], halves[1])
        else:
            attention_proj(fill[1], halves[fill[2]], last=fill[2] == 1)

    for ph in range(2):
        qr, kr, vr, g = ret_in[ph]
        pending = list(fills[ph])
        for c in range(half // RET_CHUNK):
            rows = slice(c * RET_CHUNK, (c + 1) * RET_CHUNK)
            out_rows = slice(ph * half + c * RET_CHUNK, ph * half + (c + 1) * RET_CHUNK)
            staged = []
            for h in range(H_RET):
                cols = slice(h * DK_RET, (h + 1) * DK_RET)
                scores, cross, vb, S[h] = _retention_chunk_matmuls(qr[rows, cols], kr[rows, cols], vr[rows, cols],
                                                                  S[h], decays[h])
                staged.append((scores, cross, vb))
            emit(pending.pop(0))
            for h in range(H_RET):
                cols = slice(h * DK_RET, (h + 1) * DK_RET)
                yret_ref[out_rows, cols] = _retention_chunk_output(*staged[h], g[rows, cols]).astype(BF16)
            emit(pending.pop(0))
        while pending:
            emit(pending.pop(0))
    for h in range(H_RET):
        state[h] = S[h]
        st_ref[h] = S[h]

    @pl.when(pl.program_id(0) >= win_first_step)
    def _():
        for p in range(PAIRS):
            kvw_ref[p * LANES:(p + 1) * LANES, :] = k_rows[p].T
            kvw_ref[W_ATT + p * LANES:W_ATT + (p + 1) * LANES, :] = v_rows[p].T


def _inproj_sample_body(x_ref, w_ref, cos_ref, sin_ref, qa_ref, kvw_ref, qr_ref, kr_ref, vr_ref, g_ref):
    xb = x_ref[...].astype(BF16)
    qa_ref[...] = _proj(xb, w_ref, 0) * HD_ATT ** -0.5
    kvw_ref[:, :W_ATT] = _proj(xb, w_ref, 1)
    kvw_ref[:, W_ATT:] = _proj(xb, w_ref, 2)
    qr, kr, vr, g = _retention_qkvg(xb, w_ref, cos_ref[...], sin_ref[...])
    qr_ref[...] = qr.astype(BF16)
    kr_ref[...] = kr.astype(BF16)
    vr_ref[...] = vr.astype(BF16)
    g_ref[...] = g


def _rope_tables(pos):
    half = DK_RET // 2
    inv_freq = 1.0 / (ROPE_BASE ** jnp.linspace(0.0, 1.0, half, dtype=F32))
    ang = pos.astype(F32)[:, None] * inv_freq[None, :]
    cos, sin = jnp.cos(ang), jnp.sin(ang)
    return jnp.concatenate([cos, cos], axis=1), jnp.concatenate([sin, sin], axis=1)


def _inproj_prompt(x, w_in_b, tm):
    S = x.shape[0]
    win = min(MAX_WINDOW, S)
    steps = S // tm
    win_first_step = steps - win // tm
    cos, sin = _rope_tables(jnp.concatenate([jnp.arange(tm, dtype=jnp.int32),
                                             jnp.arange(steps, dtype=jnp.int32) * tm]))
    const = lambda i: (0, 0)
    row = lambda i: (i, 0)
    view_specs = [pl.BlockSpec((PAIRS, tm // d, d * LANES), lambda i: (0, i, 0)) for d in (1, 4, 16)] * 3
    view_shapes = [jax.ShapeDtypeStruct((PAIRS, S // d, d * LANES), BF16) for d in (1, 4, 16)] * 3
    state_shape = (H_RET, DK_RET, DV_RET)
    gather_scratch = [pltpu.VMEM((PAIRS, tm, LANES), F32), pltpu.VMEM((4 * PAIRS, tm // 4, LANES), F32)]
    return pl.pallas_call(
        functools.partial(_inproj_prompt_body, win_first_step=win_first_step),
        grid=(steps,),
        in_specs=[pl.BlockSpec((tm, D_MODEL), row),
                  pl.BlockSpec((D_MODEL, N_SPLITS * W_ATT), lambda i: (0, 0)),
                  pl.BlockSpec((tm, LANES), const), pl.BlockSpec((tm, LANES), const),
                  pl.BlockSpec((steps, LANES), const), pl.BlockSpec((steps, LANES), const)],
        out_specs=view_specs + [
            pl.BlockSpec((2 * W_ATT, tm), lambda i: (0, jnp.maximum(i - win_first_step, 0))),
            pl.BlockSpec((tm, W_RET), row),
            pl.BlockSpec(state_shape, lambda i: (0, 0, 0))],
        out_shape=view_shapes + [
            jax.ShapeDtypeStruct((2 * W_ATT, win), F32),
            jax.ShapeDtypeStruct((S, W_RET), BF16),
            jax.ShapeDtypeStruct(state_shape, F32)],
        scratch_shapes=gather_scratch * 3 + [pltpu.VMEM(state_shape, F32)],
        compiler_params=_cparams(1),
        name="inproj_retention_prompt",
    )(x, w_in_b, cos[:tm], sin[:tm], cos[tm:], sin[tm:])


def _inproj_sample(x, w_in_b):
    B = x.shape[0]
    cos, sin = _rope_tables(jnp.full((B,), PAST_LEN, dtype=jnp.int32))
    sin = sin * jnp.where(jnp.arange(LANES) < LANES // 2, -1.0, 1.0)
    full = lambda shape: pl.BlockSpec(shape, lambda i: (0, 0))
    return pl.pallas_call(
        _inproj_sample_body,
        grid=(1,),
        in_specs=[full((B, D_MODEL)), full((D_MODEL, N_SPLITS * W_ATT)), full((B, LANES)), full((B, LANES))],
        out_specs=[full((B, W_ATT)), full((B, 2 * W_ATT)), full((B, W_RET)), full((B, W_RET)),
                   full((B, W_RET)), full((B, W_RET))],
        out_shape=[jax.ShapeDtypeStruct((B, W_ATT), F32),
                   jax.ShapeDtypeStruct((B, 2 * W_ATT), F32),
                   jax.ShapeDtypeStruct((B, W_RET), BF16),
                   jax.ShapeDtypeStruct((B, W_RET), BF16),
                   jax.ShapeDtypeStruct((B, W_RET), BF16),
                   jax.ShapeDtypeStruct((B, W_RET), F32)],
        compiler_params=_cparams(1),
        name="inproj_sample",
    )(x, w_in_b, cos, sin)


def _t5_bucket(dist):
    is_small = dist < MAX_EXACT
    d_f = jnp.maximum(dist, 1).astype(F32)
    large = MAX_EXACT + (jnp.log(d_f / MAX_EXACT) / math.log(MAX_WINDOW / MAX_EXACT)
                         * (N_BUCKETS - MAX_EXACT)).astype(jnp.int32)
    large = jnp.minimum(large, N_BUCKETS - 1)
    return jnp.where(is_small, dist, large)


def _tap_bias(rel_bias):
    taps = jnp.arange(TAPS + 1)
    return jnp.stack([rel_bias[_t5_bucket(taps * d)].astype(F32).T for _, d in DILATIONS])


def _band_bias(tap_bias):
    n = Q_ROWS + 2 * Q_ROWS
    w = jnp.concatenate([tap_bias[..., ::-1], jnp.full(tap_bias.shape[:-1] + (n - TAPS - 1,), NEG, F32)], axis=-1)
    skew = jnp.tile(w, Q_ROWS)[..., :Q_ROWS * (n - 1)].reshape(tap_bias.shape[:-1] + (Q_ROWS, n - 1))
    return skew[..., :2 * Q_ROWS]


def _window_bias(tap_bias, window):
    rows = []
    for i, (_, d) in enumerate(DILATIONS):
        taps_rev = tap_bias[i, :, TAPS:0:-1]
        seg = jnp.concatenate([taps_rev[..., None], jnp.full((H_ATT, TAPS, d - 1), NEG, F32)], axis=-1)
        rows.append(jnp.concatenate([jnp.full((H_ATT, window - TAPS * d), NEG, F32),
                                     seg.reshape(H_ATT, TAPS * d)], axis=-1))
    return jnp.stack(rows)


def _attn_probs(q, k, bias2, head0_b):
    qs = jnp.concatenate([q * head0_b, q * (1 - head0_b)], axis=0)
    s = lax.dot_general(qs, k, (((1,), (1,)), ((), ())), preferred_element_type=F32) + bias2
    m = jnp.max(s, axis=1, keepdims=True)
    p = jnp.exp(s - m)
    return p.astype(BF16), jnp.sum(p, axis=1, keepdims=True), m


def _attn_output(p, l, m, v, head0):
    o = jnp.dot(p, v, preferred_element_type=F32) / l
    lse = m + jnp.log(l)
    o_pair = jnp.where(head0, o[:Q_ROWS], o[Q_ROWS:])
    lse_pair = jnp.where(head0, jnp.broadcast_to(lse[:Q_ROWS], (Q_ROWS, LANES)),
                         jnp.broadcast_to(lse[Q_ROWS:], (Q_ROWS, LANES)))
    return o_pair, lse_pair


def _attn_body(q1, q4, q16, k1c, k1p, k4c, k4p, k16c, k16p, v1c, v1p, v4c, v4p, v16c, v16p, bias_ref,
               out_ref, o16, l16, o4, l4):
    first = pl.program_id(0) == 0
    lane = lax.broadcasted_iota(jnp.int32, (Q_ROWS, LANES), 1)
    head0 = lane < HD_ATT
    head0_b = jnp.where(head0, 1.0, 0.0).astype(BF16)
    col = lax.broadcasted_iota(jnp.int32, (2 * Q_ROWS, 2 * Q_ROWS), 1)
    no_prev = jnp.where(jnp.logical_and(first, col < Q_ROWS), NEG, 0.0)

    def bias2(branch, masked_prev):
        b = jnp.concatenate([bias_ref[branch, 0], bias_ref[branch, 1]], axis=0)
        return b + no_prev if masked_prev else b

    def cat(a, b):
        return jnp.concatenate([a, b], axis=0)

    def window(cur, prev, b, sl):
        if b == 0:
            return cat(prev[:, sl], cur[0:Q_ROWS, sl])
        return cur[(b - 1) * Q_ROWS:(b + 1) * Q_ROWS, sl]

    units = []
    b16 = bias2(2, True)
    for r in range(16):
        sl = slice(r * LANES, (r + 1) * LANES)

        def store16(o, l, r=r):
            o16[pl.ds(r, Q_ROWS, stride=16), :] = o
            l16[pl.ds(r, Q_ROWS, stride=16), :] = l

        units.append((lambda sl=sl: q16[:, sl], lambda sl=sl: window(k16c, k16p, 0, sl),
                      lambda sl=sl: window(v16c, v16p, 0, sl), b16, store16))

    b4 = (bias2(1, True), bias2(1, False))
    for r in range(4):
        sl = slice(r * LANES, (r + 1) * LANES)
        for b in range(4):
            rows = slice(b * Q_ROWS, (b + 1) * Q_ROWS)

            def store4(o, l, r=r, b=b):
                o4[pl.ds(b * 4 * Q_ROWS + r, Q_ROWS, stride=4), :] = o
                l4[pl.ds(b * 4 * Q_ROWS + r, Q_ROWS, stride=4), :] = l

            units.append((lambda rows=rows, sl=sl: q4[rows, sl], lambda b=b, sl=sl: window(k4c, k4p, b, sl),
                          lambda b=b, sl=sl: window(v4c, v4p, b, sl), b4[min(b, 1)], store4))

    b1 = (bias2(0, True), bias2(0, False))
    everything = slice(None)
    for b in range(ATT_BLOCK // Q_ROWS):
        rows = slice(b * Q_ROWS, (b + 1) * Q_ROWS)

        def mix(oa, la, rows=rows):
            ob, lb = o4[rows, :], l4[rows, :]
            oc, lc = o16[rows, :], l16[rows, :]
            top = jnp.maximum(jnp.maximum(la, lb), lc)
            ea, eb, ec = jnp.exp(la - top), jnp.exp(lb - top), jnp.exp(lc - top)
            out_ref[rows, :] = ((ea * oa + eb * ob + ec * oc) / (ea + eb + ec)).astype(BF16)

        units.append((lambda rows=rows: q1[rows, :], lambda b=b: window(k1c, k1p, b, everything),
                      lambda b=b: window(v1c, v1p, b, everything), b1[min(b, 1)], mix))

    pending = None
    for q, k, v, bias, finish in units:
        probs = _attn_probs(q(), k(), bias, head0_b)
        if pending is not None:
            p_probs, p_v, p_finish = pending
            p_finish(*_attn_output(*p_probs, p_v(), head0))
        pending = (probs, v, finish)
    p_probs, p_v, p_finish = pending
    p_finish(*_attn_output(*p_probs, p_v(), head0))


def _attn_prompt(q1, q4, q16, k1, k4, k16, v1, v4, v16, band_bias):
    S = q1.shape[1]
    assert S % ATT_BLOCK == 0
    steps = S // ATT_BLOCK

    def cur(d):
        return pl.BlockSpec((None, ATT_BLOCK // d, d * LANES), lambda c, p: (p, c, 0))

    def prev(d):
        per_step = ATT_BLOCK // d // Q_ROWS
        return pl.BlockSpec((None, Q_ROWS, d * LANES), lambda c, p: (p, jnp.maximum(c * per_step - 1, 0), 0))

    scratch = pltpu.VMEM((ATT_BLOCK, LANES), F32)
    return pl.pallas_call(
        _attn_body,
        grid=(steps, PAIRS),
        in_specs=[cur(1), cur(4), cur(16),
                  cur(1), prev(1), cur(4), prev(4), cur(16), prev(16),
                  cur(1), prev(1), cur(4), prev(4), cur(16), prev(16),
                  pl.BlockSpec((len(DILATIONS), 2, Q_ROWS, 2 * Q_ROWS), lambda c, p: (0, p, 0, 0))],
        out_specs=pl.BlockSpec((ATT_BLOCK, LANES), lambda c, p: (c, p)),
        out_shape=jax.ShapeDtypeStruct((S, W_ATT), BF16),
        scratch_shapes=[scratch, scratch, scratch, scratch],
        compiler_params=_cparams(2),
        name="attn_prompt",
    )(q1, q4, q16, k1, k1, k4, k4, k16, k16, v1, v1, v4, v4, v16, v16, band_bias)


def _group_norm_gate(o, g):
    mu = jnp.mean(o, axis=1, keepdims=True)
    var = jnp.mean(jnp.square(o - mu), axis=1, keepdims=True)
    return g * jax.nn.sigmoid(g) * ((o - mu) * lax.rsqrt(var + GN_EPS))


def _retention_decays(h):
    C = RET_CHUNK
    lg = LOG_GAMMA[h]
    n = lax.broadcasted_iota(jnp.int32, (C, 1), 0).astype(F32)
    diff = (lax.broadcasted_iota(jnp.int32, (C, C), 0) - lax.broadcasted_iota(jnp.int32, (C, C), 1)).astype(F32)
    return (jnp.where(diff >= 0, jnp.exp(lg * jnp.maximum(diff, 0.0)), 0.0),
            jnp.exp(lg * (n + 1.0)), jnp.exp(lg * (C - 1.0 - n)), math.exp(lg * C))


def _retention_chunk_matmuls(q, k, v, S, decays):
    decay, q_decay, k_decay, chunk_decay = decays
    qb, vb = q.astype(BF16), v.astype(BF16)
    scores = lax.dot_general(qb, k.astype(BF16), (((1,), (1,)), ((), ())), preferred_element_type=F32) * decay
    cross = jnp.dot(qb, S.astype(BF16), preferred_element_type=F32) * q_decay
    k_dec_t = (k * k_decay).T.astype(BF16)
    return scores.astype(BF16), cross, vb, chunk_decay * S + jnp.dot(k_dec_t, vb, preferred_element_type=F32)


def _retention_chunk_output(scores, cross, vb, g):
    return _group_norm_gate(jnp.dot(scores, vb, preferred_element_type=F32) + cross, g)


def _as_column(row_pair):
    return jnp.broadcast_to(row_pair, (LANES, LANES)).T


def _sample_mixer_body(qa_ref, kvw_ref, qr_ref, kr_ref, vr_ref, g_ref, cache_ref, st_ref, bias_ref, bias0_ref,
                       oatt_ref, yret_ref, stout_ref):
    W = cache_ref.shape[-1]
    logit_rows, self_rows = [], []
    for p in range(PAIRS):
        lanes = slice(p * LANES, (p + 1) * LANES)
        q_pair = qa_ref[:, lanes]
        k_pair = kvw_ref[:, lanes]
        q_col = _as_column(q_pair)
        for hh in range(2):
            feat = slice(hh * HD_ATT, (hh + 1) * HD_ATT)
            logit_rows.append(jnp.sum(cache_ref[0, 2 * p + hh] * q_col[feat, 0:1], axis=0, keepdims=True))
            self_rows.append(jnp.sum(q_pair[:, feat] * k_pair[:, feat], axis=1, keepdims=True))
    logits = jnp.concatenate(logit_rows, axis=0)
    s_self = jnp.concatenate(self_rows, axis=0) + bias0_ref[:, 0:1]

    probs, p_selfs, denoms, lses = [], [], [], []
    for i, (_, d) in enumerate(DILATIONS):
        lo = W - TAPS * d
        s = logits[:, lo:] + bias_ref[i, :, lo:]
        m = jnp.maximum(jnp.max(s, axis=1, keepdims=True), s_self)
        pr = jnp.exp(s - m)
        p_self = jnp.exp(s_self - m)
        l = jnp.sum(pr, axis=1, keepdims=True) + p_self
        probs.append(pr)
        p_selfs.append(p_self)
        denoms.append(l)
        lses.append(m + jnp.log(l))
    top = jnp.maximum(jnp.maximum(lses[0], lses[1]), lses[2])
    e = [jnp.exp(x - top) for x in lses]
    e_sum = e[0] + e[1] + e[2]
    coef = [e[i] / (e_sum * denoms[i]) for i in range(3)]
    lo4, lo1 = W - TAPS * 4, W - TAPS
    p16, p4, p1 = coef[2] * probs[2], coef[1] * probs[1], coef[0] * probs[0]
    p_all = jnp.concatenate([p16[:, :lo4], p16[:, lo4:lo1] + p4[:, :lo1 - lo4],
                             p16[:, lo1:] + p4[:, lo1 - lo4:] + p1], axis=1)
    c_self = coef[0] * p_selfs[0] + coef[1] * p_selfs[1] + coef[2] * p_selfs[2]
    for p in range(PAIRS):
        lanes = slice(p * LANES, (p + 1) * LANES)
        v_col = _as_column(kvw_ref[:, W_ATT + p * LANES:W_ATT + (p + 1) * LANES])
        out_cols = []
        for hh in range(2):
            h = 2 * p + hh
            feat = slice(hh * HD_ATT, (hh + 1) * HD_ATT)
            pv = jnp.sum(cache_ref[1, h] * p_all[h:h + 1, :], axis=1, keepdims=True)
            out_cols.append(pv + c_self[h:h + 1, :] * v_col[feat, 0:1])
        col = jnp.broadcast_to(jnp.concatenate(out_cols, axis=0), (LANES, LANES))
        oatt_ref[:, lanes] = col.T[0:1, :].astype(BF16)

    for h in range(H_RET):
        cols = slice(h * DK_RET, (h + 1) * DK_RET)
        gamma = math.exp(LOG_GAMMA[h])
        qh = qr_ref[:, cols]
        kh = kr_ref[:, cols].astype(F32)
        vh = vr_ref[:, cols].astype(F32)
        S = st_ref[h]
        qk = jnp.sum(qh.astype(F32) * kh, axis=1, keepdims=True)
        cross = jnp.dot(jnp.broadcast_to(qh, (16, DK_RET)), S.astype(BF16), preferred_element_type=F32)[:1]
        o = qk * vh + cross * gamma
        stout_ref[h] = gamma * S + _as_column(kh) * vh
        yret_ref[:, cols] = _group_norm_gate(o, g_ref[:, cols]).astype(BF16)


def _mixer_operands(qa, kvw, qr, kr, vr, g, cache_t, state, tap_bias):
    B, _, _, _, W = cache_t.shape
    assert W == MAX_WINDOW and PAST_LEN >= MAX_WINDOW
    row3 = lambda a: a.reshape(B, 1, a.shape[-1])
    vec = lambda width: pl.BlockSpec((None, 1, width), lambda b: (b, 0, 0))
    bias0 = jnp.broadcast_to(tap_bias[0, :, 0:1], (H_ATT, LANES))
    st_spec = pl.BlockSpec((None, H_RET, DK_RET, DV_RET), lambda b: (b, 0, 0, 0))
    in_specs = [vec(W_ATT), vec(2 * W_ATT), vec(W_RET), vec(W_RET), vec(W_RET), vec(W_RET),
                pl.BlockSpec((None, 2, H_ATT, HD_ATT, W), lambda b: (b, 0, 0, 0, 0)),
                st_spec,
                pl.BlockSpec((len(DILATIONS), H_ATT, W), lambda b: (0, 0, 0)),
                pl.BlockSpec((H_ATT, LANES), lambda b: (0, 0))]
    args = (row3(qa), row3(kvw), row3(qr), row3(kr), row3(vr), row3(g), cache_t, state,
            _window_bias(tap_bias, W), bias0)
    out_specs = [vec(W_ATT), vec(W_RET), st_spec]
    out_shape = [jax.ShapeDtypeStruct((B, 1, W_ATT), BF16), jax.ShapeDtypeStruct((B, 1, W_RET), BF16),
                 jax.ShapeDtypeStruct(state.shape, F32)]
    return in_specs, args, out_specs, out_shape


FF_CHUNK = 1024


def _layernorm(z, g, b):
    mu = jnp.mean(z, axis=1, keepdims=True)
    var = jnp.mean(jnp.square(z - mu), axis=1, keepdims=True)
    return (z - mu) * lax.rsqrt(var + LN_EPS) * g + b


def _out_ffn_body(x_ref, oatt_ref, yret_ref, wo_ref, g1_ref, b1_ref, wu_ref, wd_ref, g2_ref, b2_ref, y_ref):
    mix = (jnp.dot(oatt_ref[...], wo_ref[:W_ATT, :], preferred_element_type=F32)
           + jnp.dot(yret_ref[...], wo_ref[W_ATT:, :], preferred_element_type=F32))
    x1 = _layernorm(ALPHA * x_ref[...] + mix, g1_ref[...], b1_ref[...])
    x1b = x1.astype(BF16)

    def up(j):
        cols = slice(j * FF_CHUNK, (j + 1) * FF_CHUNK)
        return jnp.square(jnp.maximum(jnp.dot(x1b, wu_ref[:, cols], preferred_element_type=F32), 0.0)).astype(BF16)

    n_chunks = D_FF // FF_CHUNK
    ffn = jnp.zeros_like(x1)
    h = up(0)
    for j in range(n_chunks):
        h_next = up(j + 1) if j + 1 < n_chunks else None
        ffn = ffn + jnp.dot(h, wd_ref[j * FF_CHUNK:(j + 1) * FF_CHUNK, :], preferred_element_type=F32)
        h = h_next
    y_ref[...] = _layernorm(ALPHA * x1 + ffn, g2_ref[...], b2_ref[...])


def _ffn_operands(x, oatt, yret, wo_b, g1, b1, wu_b, wd_b, g2, b2, tm):
    row = lambda i: (i, 0)
    const = lambda shape: pl.BlockSpec(shape, lambda i: (0, 0), pipeline_mode=pl.Buffered(1))
    in_specs = [pl.BlockSpec((tm, D_MODEL), row), pl.BlockSpec((tm, W_ATT), row), pl.BlockSpec((tm, W_RET), row),
                const((W_ATT + W_RET, D_MODEL)), const((1, D_MODEL)), const((1, D_MODEL)),
                const((D_MODEL, D_FF)), const((D_FF, D_MODEL)), const((1, D_MODEL)), const((1, D_MODEL))]
    args = (x, oatt, yret, wo_b, g1, b1, wu_b, wd_b, g2, b2)
    return in_specs, args, pl.BlockSpec((tm, D_MODEL), row), jax.ShapeDtypeStruct((x.shape[0], D_MODEL), F32)


def _out_ffn(ffn_args, tm, name):
    in_specs, args, out_spec, out_shape = _ffn_operands(*ffn_args, tm)
    return pl.pallas_call(
        _out_ffn_body, grid=(args[0].shape[0] // tm,), in_specs=in_specs, out_specs=out_spec, out_shape=out_shape,
        compiler_params=_cparams(1), name=name,
    )(*args)


def _ffn_and_mixer_body(*refs, n_ffn_in, n_mix_in):
    ffn_in, mix_in = refs[:n_ffn_in], refs[n_ffn_in:n_ffn_in + n_mix_in]
    y_ref, *mix_out = refs[n_ffn_in + n_mix_in:]
    _sample_mixer_body(*mix_in, *mix_out)
    _out_ffn_body(*ffn_in, y_ref)


def _out_ffn_prompt_and_sample_mixer(ffn_args, mixer_args):
    B = mixer_args[0].shape[0]
    tm = ffn_args[0].shape[0] // B
    f_in, f_args, f_out, f_shape = _ffn_operands(*ffn_args, tm)
    m_in, m_args, m_out, m_shape = _mixer_operands(*mixer_args)
    y, oatt, yret, st = pl.pallas_call(
        functools.partial(_ffn_and_mixer_body, n_ffn_in=len(f_in), n_mix_in=len(m_in)),
        grid=(B,), in_specs=f_in + m_in, out_specs=[f_out] + m_out, out_shape=[f_shape] + m_shape,
        compiler_params=_cparams(1), name="out_ffn_prompt_sample_mixer",
    )(*f_args, *m_args)
    return y, oatt.reshape(B, W_ATT), yret.reshape(B, W_RET), st


PROMPT_ROWS = 512


def kernel(x_prompt, x_sample, cache_kv_win, state_ret, w_in, rel_bias, w_out,
           ln1_g, ln1_b, w_up, w_down, ln2_g, ln2_b):
    assert x_prompt.shape[0] == 1 and x_sample.shape[1] == 1 and w_in.shape[0] == DEPTH
    S = x_prompt.shape[1]
    B = x_sample.shape[0]
    w_in_b, wo_b = w_in[0].astype(BF16), w_out[0].astype(BF16)
    wu_b, wd_b = w_up[0].astype(BF16), w_down[0].astype(BF16)
    g1, b1, g2, b2 = ln1_g[0][None], ln1_b[0][None], ln2_g[0][None], ln2_b[0][None]
    tap_bias = _tap_bias(rel_bias)

    xp = x_prompt[0]
    *qkv_views, kvw_p, yret, st_p = _inproj_prompt(xp, w_in_b, PROMPT_ROWS)
    oatt = _attn_prompt(*qkv_views, _band_bias(tap_bias))

    xs = x_sample[:, 0]
    qa_s, kvw_s, qr_s, kr_s, vr_s, g_s = _inproj_sample(xs, w_in_b)
    cache_t = jnp.transpose(cache_kv_win[0], (0, 2, 3, 4, 1))
    ffn_weights = (wo_b, g1, b1, wu_b, wd_b, g2, b2)
    y_p, oatt_s, yret_s, st_s = _out_ffn_prompt_and_sample_mixer(
        (xp, oatt, yret) + ffn_weights, (qa_s, kvw_s, qr_s, kr_s, vr_s, g_s, cache_t, state_ret[0], tap_bias))
    y_s = _out_ffn((xs, oatt_s, yret_s) + ffn_weights, B, "out_ffn_sample")

    win = kvw_p.shape[1]
    kv_win_p = jnp.transpose(kvw_p.reshape(2, H_ATT, HD_ATT, win), (3, 0, 1, 2))
    return (y_p[None], y_s[:, None],
            kv_win_p[None, None], kvw_s.reshape(1, B, 1, 2, H_ATT, HD_ATT),
            st_p[None, None], st_s[None])
```

```python
import functools
import math

import jax
import jax.numpy as jnp
from jax import lax
from jax.experimental import pallas as pl
from jax.experimental.pallas import tpu as pltpu

F32 = jnp.float32
BF16 = jnp.bfloat16

D_MODEL = 1024
DEPTH = 1
PAST_LEN = 16384
W_ATT = 512
HD_ATT = 64
H_ATT = 8
DILATIONS = ((128, 1), (512, 4), (2048, 16))
TAPS = 128
MAX_WINDOW = 2048
N_BUCKETS = 32
MAX_EXACT = N_BUCKETS // 2
W_RET = 512
H_RET = 4
DK_RET = 128
DV_RET = 128
RET_CHUNK = 128
ROPE_BASE = 10000.0
D_FF = 4096
N_SPLITS = 7
ALPHA = (2.0 * DEPTH) ** 0.25
LN_EPS = 1e-5
GN_EPS = 1e-6
LOG_GAMMA = tuple(math.log(1.0 - 2.0 ** (-5.0 - h)) for h in range(H_RET))

LANES = 128
PAIRS = W_ATT // LANES
NEG = -1e30

ATT_BLOCK = 2048
Q_ROWS = 128
VMEM_LIMIT = 56 * 1024 * 1024


def _cparams(n_axes):
    return pltpu.CompilerParams(dimension_semantics=("arbitrary",) * n_axes, vmem_limit_bytes=VMEM_LIMIT)


def _rotary(u, cos, sin_signed):
    outs = []
    for h in range(H_RET):
        xh = u[:, h * LANES:(h + 1) * LANES]
        outs.append(xh * cos + pltpu.roll(xh, LANES // 2, 1) * sin_signed)
    return jnp.concatenate(outs, axis=1)


def _proj(xb, w_ref, j):
    return jnp.dot(xb, w_ref[:, j * W_ATT:(j + 1) * W_ATT], preferred_element_type=F32)


def _store_rows(val, rows, nat_ref, rows_scr):
    for p in range(PAIRS):
        x = val[:, p * LANES:(p + 1) * LANES]
        nat_ref[p, rows, :] = x.astype(BF16)
        rows_scr[p, rows, :] = x


def _store_gathered(by4_ref, by16_ref, rows_scr, by4_scr):
    tm = rows_scr.shape[1]
    for p in range(PAIRS):
        for r4 in range(4):
            a = rows_scr[p, pl.ds(r4, tm // 4, stride=4), :]
            by4_ref[p, :, r4 * LANES:(r4 + 1) * LANES] = a.astype(BF16)
            by4_scr[p * 4 + r4] = a
    for p in range(PAIRS):
        for r4 in range(4):
            for rr in range(4):
                r16 = r4 + 4 * rr
                b = by4_scr[p * 4 + r4, pl.ds(rr, tm // 16, stride=4), :]
                by16_ref[p, :, r16 * LANES:(r16 + 1) * LANES] = b.astype(BF16)


def _retention_qkvg(xb, w_ref, cos, sin):
    return (_rotary(_proj(xb, w_ref, 3), cos, sin), _rotary(_proj(xb, w_ref, 4), cos, sin) * DK_RET ** -0.5,
            _proj(xb, w_ref, 5), _proj(xb, w_ref, 6))


def _inproj_prompt_body(*refs, win_first_step, n_cast):
    x_ref, w_ref, cos_row_ref, sin_row_ref, cos_step_ref, sin_step_ref = refs[:6]
    cast_in, outs = refs[6:6 + n_cast], refs[6 + n_cast:]
    q1_ref, q4_ref, q16_ref, k1_ref, k4_ref, k16_ref, v1_ref, v4_ref, v16_ref, kvw_ref, yret_ref, st_ref = outs[:12]
    cast_out = outs[12:12 + n_cast]
    q_rows, q_by4, k_rows, k_by4, v_rows, v_by4, state = outs[12 + n_cast:]

    @pl.when(pl.program_id(0) == 0)
    def _():
        state[...] = jnp.zeros_like(state)

    for src, dst in zip(cast_in, cast_out):
        dst[...] = src[...].astype(BF16)

    xb = x_ref[...].astype(BF16)
    tm = xb.shape[0]
    half = tm // 2

    step = pl.ds(pl.program_id(0), 1)
    ca, sa = cos_step_ref[step, :], sin_step_ref[step, :]
    cb, sb = cos_row_ref[...], sin_row_ref[...]
    sign = jnp.where(lax.broadcasted_iota(jnp.int32, (1, LANES), 1) < LANES // 2, -1.0, 1.0)
    cos, sin = ca * cb - sa * sb, (sa * cb + ca * sb) * sign

    def retention_proj(j, rows):
        u = _proj(xb[rows], w_ref, 3 + j)
        if j < 2:
            u = _rotary(u, cos[rows], sin[rows])
        return u * DK_RET ** -0.5 if j == 1 else u

    attn_out = ((q1_ref, q4_ref, q16_ref, q_rows, q_by4), (k1_ref, k4_ref, k16_ref, k_rows, k_by4),
                (v1_ref, v4_ref, v16_ref, v_rows, v_by4))

    def attention_proj(j, rows, last):
        nat_ref, by4_ref, by16_ref, rows_scr, by4_scr = attn_out[j]
        val = _proj(xb[rows], w_ref, j)
        _store_rows(val * HD_ATT ** -0.5 if j == 0 else val, rows, nat_ref, rows_scr)
        if last:
            _store_gathered(by4_ref, by16_ref, rows_scr, by4_scr)

    halves = (slice(0, half), slice(half, tm))
    ret_in = [[retention_proj(j, halves[0]) for j in range(4)], [None] * 4]
    fills = [[("ret", 0), ("ret", 1), ("ret", 2), ("ret", 3), ("att", 0, 0)],
             [("att", 0, 1), ("att", 1, 0), ("att", 1, 1), ("att", 2, 0), ("att", 2, 1)]]
    decays = [_retention_decays(h) for h in range(H_RET)]
    S = [state[h] for h in range(H_RET)]

    def emit(fill):
        if fill[0] == "ret":
            ret_in[1][fill[1]] = retention_proj(fill[1], halves[1])
        else:
            attention_proj(fill[1], halves[fill[2]], last=fill[2] == 1)

    for ph in range(2):
        qr, kr, vr, g = ret_in[ph]
        pending = list(fills[ph])
        for c in range(half // RET_CHUNK):
            rows = slice(c * RET_CHUNK, (c + 1) * RET_CHUNK)
            out_rows = slice(ph * half + c * RET_CHUNK, ph * half + (c + 1) * RET_CHUNK)
            staged = []
            for h in range(H_RET):
                cols = slice(h * DK_RET, (h + 1) * DK_RET)
                scores, cross, vb, S[h] = _retention_chunk_matmuls(qr[rows, cols], kr[rows, cols], vr[rows, cols],
                                                                  S[h], decays[h])
                staged.append((scores, cross, vb))
            emit(pending.pop(0))
            for h in range(H_RET):
                cols = slice(h * DK_RET, (h + 1) * DK_RET)
                yret_ref[out_rows, cols] = _retention_chunk_output(*staged[h], g[rows, cols]).astype(BF16)
            emit(pending.pop(0))
        while pending:
            emit(pending.pop(0))
    for h in range(H_RET):
        state[h] = S[h]
        st_ref[h] = S[h]

    @pl.when(pl.program_id(0) >= win_first_step)
    def _():
        for p in range(PAIRS):
            kvw_ref[p * LANES:(p + 1) * LANES, :] = k_rows[p].T
            kvw_ref[W_ATT + p * LANES:W_ATT + (p + 1) * LANES, :] = v_rows[p].T


def _inproj_sample_body(x_ref, w_ref, cos_ref, sin_ref, qa_ref, kvw_ref, qr_ref, kr_ref, vr_ref, g_ref):
    xb = x_ref[...].astype(BF16)
    qa_ref[...] = _proj(xb, w_ref, 0) * HD_ATT ** -0.5
    kvw_ref[:, :W_ATT] = _proj(xb, w_ref, 1)
    kvw_ref[:, W_ATT:] = _proj(xb, w_ref, 2)
    qr, kr, vr, g = _retention_qkvg(xb, w_ref, cos_ref[...], sin_ref[...])
    qr_ref[...] = qr
    kr_ref[...] = kr
    vr_ref[...] = vr
    g_ref[...] = g


def _rope_tables(pos):
    half = DK_RET // 2
    inv_freq = 1.0 / (ROPE_BASE ** jnp.linspace(0.0, 1.0, half, dtype=F32))
    ang = pos.astype(F32)[:, None] * inv_freq[None, :]
    cos, sin = jnp.cos(ang), jnp.sin(ang)
    return jnp.concatenate([cos, cos], axis=1), jnp.concatenate([sin, sin], axis=1)


def _inproj_prompt(x, w_in_b, later_weights, tm):
    S = x.shape[0]
    win = min(MAX_WINDOW, S)
    steps = S // tm
    win_first_step = steps - win // tm
    cos, sin = _rope_tables(jnp.concatenate([jnp.arange(tm, dtype=jnp.int32),
                                             jnp.arange(steps, dtype=jnp.int32) * tm]))
    const = lambda i: (0, 0)
    row = lambda i: (i, 0)
    view_specs = [pl.BlockSpec((PAIRS, tm // d, d * LANES), lambda i: (0, i, 0)) for d in (1, 4, 16)] * 3
    view_shapes = [jax.ShapeDtypeStruct((PAIRS, S // d, d * LANES), BF16) for d in (1, 4, 16)] * 3
    state_shape = (H_RET, DK_RET, DV_RET)
    gather_scratch = [pltpu.VMEM((PAIRS, tm, LANES), F32), pltpu.VMEM((4 * PAIRS, tm // 4, LANES), F32)]
    slab_specs = [pl.BlockSpec((w.shape[0] // steps, w.shape[1]), row) for w in later_weights]
    return pl.pallas_call(
        functools.partial(_inproj_prompt_body, win_first_step=win_first_step, n_cast=len(later_weights)),
        grid=(steps,),
        in_specs=[pl.BlockSpec((tm, D_MODEL), row),
                  pl.BlockSpec((D_MODEL, N_SPLITS * W_ATT), lambda i: (0, 0)),
                  pl.BlockSpec((tm, LANES), const), pl.BlockSpec((tm, LANES), const),
                  pl.BlockSpec((steps, LANES), const), pl.BlockSpec((steps, LANES), const)] + slab_specs,
        out_specs=view_specs + [
            pl.BlockSpec((2 * W_ATT, tm), lambda i: (0, jnp.maximum(i - win_first_step, 0))),
            pl.BlockSpec((tm, W_RET), row),
            pl.BlockSpec(state_shape, lambda i: (0, 0, 0))] + slab_specs,
        out_shape=view_shapes + [
            jax.ShapeDtypeStruct((2 * W_ATT, win), F32),
            jax.ShapeDtypeStruct((S, W_RET), BF16),
            jax.ShapeDtypeStruct(state_shape, F32)] + [jax.ShapeDtypeStruct(w.shape, BF16) for w in later_weights],
        scratch_shapes=gather_scratch * 3 + [pltpu.VMEM(state_shape, F32)],
        compiler_params=_cparams(1),
        name="inproj_retention_prompt",
    )(x, w_in_b, cos[:tm], sin[:tm], cos[tm:], sin[tm:], *later_weights)


def _inproj_sample(x, w_in_b):
    B = x.shape[0]
    cos, sin = _rope_tables(jnp.full((B,), PAST_LEN, dtype=jnp.int32))
    sin = sin * jnp.where(jnp.arange(LANES) < LANES // 2, -1.0, 1.0)
    full = lambda shape: pl.BlockSpec(shape, lambda i: (0, 0))
    return pl.pallas_call(
        _inproj_sample_body,
        grid=(1,),
        in_specs=[full((B, D_MODEL)), full((D_MODEL, N_SPLITS * W_ATT)), full((B, LANES)), full((B, LANES))],
        out_specs=[full((B, W_ATT)), full((B, 2 * W_ATT)), full((B, W_RET)), full((B, W_RET)),
                   full((B, W_RET)), full((B, W_RET))],
        out_shape=[jax.ShapeDtypeStruct((B, W_ATT), F32),
                   jax.ShapeDtypeStruct((B, 2 * W_ATT), F32),
                   jax.ShapeDtypeStruct((B, W_RET), F32),
                   jax.ShapeDtypeStruct((B, W_RET), F32),
                   jax.ShapeDtypeStruct((B, W_RET), F32),
                   jax.ShapeDtypeStruct((B, W_RET), F32)],
        compiler_params=_cparams(1),
        name="inproj_sample",
    )(x, w_in_b, cos, sin)


def _t5_bucket(dist):
    is_small = dist < MAX_EXACT
    d_f = jnp.maximum(dist, 1).astype(F32)
    large = MAX_EXACT + (jnp.log(d_f / MAX_EXACT) / math.log(MAX_WINDOW / MAX_EXACT)
                         * (N_BUCKETS - MAX_EXACT)).astype(jnp.int32)
    large = jnp.minimum(large, N_BUCKETS - 1)
    return jnp.where(is_small, dist, large)


def _tap_bias(rel_bias):
    taps = jnp.arange(TAPS + 1)
    return jnp.stack([rel_bias[_t5_bucket(taps * d)].astype(F32).T for _, d in DILATIONS])


def _band_bias_rows(tap_bias):
    rows = jnp.concatenate([tap_bias[..., ::-1], jnp.full(tap_bias.shape[:-1] + (2 * Q_ROWS - TAPS - 1,), NEG, F32)],
                           axis=-1)
    return rows.reshape(len(DILATIONS), PAIRS, 2, 2 * Q_ROWS)


def _window_bias(tap_bias, window):
    rows = []
    for i, (_, d) in enumerate(DILATIONS):
        taps_rev = tap_bias[i, :, TAPS:0:-1]
        seg = jnp.concatenate([taps_rev[..., None], jnp.full((H_ATT, TAPS, d - 1), NEG, F32)], axis=-1)
        rows.append(jnp.concatenate([jnp.full((H_ATT, window - TAPS * d), NEG, F32),
                                     seg.reshape(H_ATT, TAPS * d)], axis=-1))
    return jnp.stack(rows)


def _attn_probs(q, k, bias2, head0_b):
    qs = jnp.concatenate([q * head0_b, q * (1 - head0_b)], axis=0)
    s = lax.dot_general(qs, k, (((1,), (1,)), ((), ())), preferred_element_type=F32) + bias2
    m = jnp.max(s, axis=1, keepdims=True)
    p = jnp.exp(s - m)
    return p.astype(BF16), jnp.sum(p, axis=1, keepdims=True), m


def _attn_output(p, l, m, v, head0):
    o = jnp.dot(p, v, preferred_element_type=F32) / l
    lse = m + jnp.log(l)
    o_pair = jnp.where(head0, o[:Q_ROWS], o[Q_ROWS:])
    lse_pair = jnp.where(head0, jnp.broadcast_to(lse[:Q_ROWS], (Q_ROWS, LANES)),
                         jnp.broadcast_to(lse[Q_ROWS:], (Q_ROWS, LANES)))
    return o_pair, lse_pair


def _attn_body(q1, q4, q16, k1c, k1p, k4c, k4p, k16c, k16p, v1c, v1p, v4c, v4p, v16c, v16p, bias_ref,
               out_ref, o16, l16, o4, l4, band):
    first = pl.program_id(1) == 0
    lane = lax.broadcasted_iota(jnp.int32, (Q_ROWS, LANES), 1)
    head0 = lane < HD_ATT
    head0_b = jnp.where(head0, 1.0, 0.0).astype(BF16)
    col = lax.broadcasted_iota(jnp.int32, (2 * Q_ROWS, 2 * Q_ROWS), 1)
    no_prev = jnp.where(jnp.logical_and(first, col < Q_ROWS), NEG, 0.0)

    @pl.when(first)
    def _():
        for branch in range(len(DILATIONS)):
            for hh in range(2):
                row = jnp.broadcast_to(bias_ref[branch, hh:hh + 1, :], (Q_ROWS, 2 * Q_ROWS))
                band[branch, hh * Q_ROWS:(hh + 1) * Q_ROWS, :] = pltpu.roll(row, 0, 1, stride=1, stride_axis=0)

    def bias2(branch, masked_prev):
        return band[branch] + no_prev if masked_prev else band[branch]

    def cat(a, b):
        return jnp.concatenate([a, b], axis=0)

    def window(cur, prev, b, sl):
        if b == 0:
            return cat(prev[:, sl], cur[0:Q_ROWS, sl])
        return cur[(b - 1) * Q_ROWS:(b + 1) * Q_ROWS, sl]

    units = []
    b16 = bias2(2, True)
    for r in range(16):
        sl = slice(r * LANES, (r + 1) * LANES)

        def store16(o, l, r=r):
            o16[pl.ds(r, Q_ROWS, stride=16), :] = o
            l16[pl.ds(r, Q_ROWS, stride=16), :] = l

        units.append((lambda sl=sl: q16[:, sl], lambda sl=sl: window(k16c, k16p, 0, sl),
                      lambda sl=sl: window(v16c, v16p, 0, sl), b16, store16))

    b4 = (bias2(1, True), bias2(1, False))
    for r in range(4):
        sl = slice(r * LANES, (r + 1) * LANES)
        for b in range(4):
            rows = slice(b * Q_ROWS, (b + 1) * Q_ROWS)

            def store4(o, l, r=r, b=b):
                o4[pl.ds(b * 4 * Q_ROWS + r, Q_ROWS, stride=4), :] = o
                l4[pl.ds(b * 4 * Q_ROWS + r, Q_ROWS, stride=4), :] = l

            units.append((lambda rows=rows, sl=sl: q4[rows, sl], lambda b=b, sl=sl: window(k4c, k4p, b, sl),
                          lambda b=b, sl=sl: window(v4c, v4p, b, sl), b4[min(b, 1)], store4))

    b1 = (bias2(0, True), bias2(0, False))
    everything = slice(None)
    for b in range(ATT_BLOCK // Q_ROWS):
        rows = slice(b * Q_ROWS, (b + 1) * Q_ROWS)

        def mix(oa, la, rows=rows):
            ob, lb = o4[rows, :], l4[rows, :]
            oc, lc = o16[rows, :], l16[rows, :]
            top = jnp.maximum(jnp.maximum(la, lb), lc)
            ea, eb, ec = jnp.exp(la - top), jnp.exp(lb - top), jnp.exp(lc - top)
            out_ref[rows, :] = ((ea * oa + eb * ob + ec * oc) / (ea + eb + ec)).astype(BF16)

        units.append((lambda rows=rows: q1[rows, :], lambda b=b: window(k1c, k1p, b, everything),
                      lambda b=b: window(v1c, v1p, b, everything), b1[min(b, 1)], mix))

    pending = None
    for q, k, v, bias, finish in units:
        probs = _attn_probs(q(), k(), bias, head0_b)
        if pending is not None:
            p_probs, p_v, p_finish = pending
            p_finish(*_attn_output(*p_probs, p_v(), head0))
        pending = (probs, v, finish)
    p_probs, p_v, p_finish = pending
    p_finish(*_attn_output(*p_probs, p_v(), head0))


def _attn_prompt(q1, q4, q16, k1, k4, k16, v1, v4, v16, band_bias):
    S = q1.shape[1]
    assert S % ATT_BLOCK == 0
    steps = S // ATT_BLOCK

    def cur(d):
        return pl.BlockSpec((None, ATT_BLOCK // d, d * LANES), lambda p, c: (p, c, 0))

    def prev(d):
        per_step = ATT_BLOCK // d // Q_ROWS
        return pl.BlockSpec((None, Q_ROWS, d * LANES), lambda p, c: (p, jnp.maximum(c * per_step - 1, 0), 0))

    scratch = pltpu.VMEM((ATT_BLOCK, LANES), F32)
    return pl.pallas_call(
        _attn_body,
        grid=(PAIRS, steps),
        in_specs=[cur(1), cur(4), cur(16),
                  cur(1), prev(1), cur(4), prev(4), cur(16), prev(16),
                  cur(1), prev(1), cur(4), prev(4), cur(16), prev(16),
                  pl.BlockSpec((len(DILATIONS), None, 2, 2 * Q_ROWS), lambda p, c: (0, p, 0, 0))],
        out_specs=pl.BlockSpec((ATT_BLOCK, LANES), lambda p, c: (c, p)),
        out_shape=jax.ShapeDtypeStruct((S, W_ATT), BF16),
        scratch_shapes=[scratch, scratch, scratch, scratch,
                        pltpu.VMEM((len(DILATIONS), 2 * Q_ROWS, 2 * Q_ROWS), F32)],
        compiler_params=_cparams(2),
        name="attn_prompt",
    )(q1, q4, q16, k1, k1, k4, k4, k16, k16, v1, v1, v4, v4, v16, v16, band_bias)


def _group_norm_gate(o, g):
    mu = jnp.mean(o, axis=1, keepdims=True)
    var = jnp.mean(jnp.square(o - mu), axis=1, keepdims=True)
    return g * jax.nn.sigmoid(g) * ((o - mu) * lax.rsqrt(var + GN_EPS))


def _retention_decays(h):
    C = RET_CHUNK
    lg = LOG_GAMMA[h]
    n = lax.broadcasted_iota(jnp.int32, (C, 1), 0).astype(F32)
    diff = (lax.broadcasted_iota(jnp.int32, (C, C), 0) - lax.broadcasted_iota(jnp.int32, (C, C), 1)).astype(F32)
    return (jnp.where(diff >= 0, jnp.exp(lg * jnp.maximum(diff, 0.0)), 0.0),
            jnp.exp(lg * (n + 1.0)), jnp.exp(lg * (C - 1.0 - n)), math.exp(lg * C))


def _retention_chunk_matmuls(q, k, v, S, decays):
    decay, q_decay, k_decay, chunk_decay = decays
    qb, vb = q.astype(BF16), v.astype(BF16)
    scores = lax.dot_general(qb, k.astype(BF16), (((1,), (1,)), ((), ())), preferred_element_type=F32) * decay
    cross = jnp.dot(qb, S.astype(BF16), preferred_element_type=F32) * q_decay
    k_dec_t = (k * k_decay).T.astype(BF16)
    return scores.astype(BF16), cross, vb, chunk_decay * S + jnp.dot(k_dec_t, vb, preferred_element_type=F32)


def _retention_chunk_output(scores, cross, vb, g):
    return _group_norm_gate(jnp.dot(scores, vb, preferred_element_type=F32) + cross, g)


def _as_column(row_pair):
    return jnp.broadcast_to(row_pair, (LANES, LANES)).T


def _sample_mixer_body(qa_ref, kvw_ref, qr_ref, kr_ref, vr_ref, g_ref, cache_ref, st_ref, bias_ref, bias0_ref,
                       oatt_ref, yret_ref, stout_ref):
    W = cache_ref.shape[-1]
    row = pl.ds(pl.program_id(0), 1)
    qa, kvw = qa_ref[row, :], kvw_ref[row, :]
    logit_rows, self_rows = [], []
    for p in range(PAIRS):
        lanes = slice(p * LANES, (p + 1) * LANES)
        q_pair = qa[:, lanes]
        k_pair = kvw[:, lanes]
        q_col = _as_column(q_pair)
        for hh in range(2):
            feat = slice(hh * HD_ATT, (hh + 1) * HD_ATT)
            logit_rows.append(jnp.sum(cache_ref[0, 2 * p + hh] * q_col[feat, 0:1], axis=0, keepdims=True))
            self_rows.append(jnp.sum(q_pair[:, feat] * k_pair[:, feat], axis=1, keepdims=True))
    logits = jnp.concatenate(logit_rows, axis=0)
    s_self = jnp.concatenate(self_rows, axis=0) + bias0_ref[:, 0:1]

    probs, p_selfs, denoms, lses = [], [], [], []
    for i, (_, d) in enumerate(DILATIONS):
        lo = W - TAPS * d
        s = logits[:, lo:] + bias_ref[i, :, lo:]
        m = jnp.maximum(jnp.max(s, axis=1, keepdims=True), s_self)
        pr = jnp.exp(s - m)
        p_self = jnp.exp(s_self - m)
        l = jnp.sum(pr, axis=1, keepdims=True) + p_self
        probs.append(pr)
        p_selfs.append(p_self)
        denoms.append(l)
        lses.append(m + jnp.log(l))
    top = jnp.maximum(jnp.maximum(lses[0], lses[1]), lses[2])
    e = [jnp.exp(x - top) for x in lses]
    e_sum = e[0] + e[1] + e[2]
    coef = [e[i] / (e_sum * denoms[i]) for i in range(3)]
    lo4, lo1 = W - TAPS * 4, W - TAPS
    p16, p4, p1 = coef[2] * probs[2], coef[1] * probs[1], coef[0] * probs[0]
    p_all = jnp.concatenate([p16[:, :lo4], p16[:, lo4:lo1] + p4[:, :lo1 - lo4],
                             p16[:, lo1:] + p4[:, lo1 - lo4:] + p1], axis=1)
    c_self = coef[0] * p_selfs[0] + coef[1] * p_selfs[1] + coef[2] * p_selfs[2]
    out_pairs = []
    for p in range(PAIRS):
        lanes = slice(p * LANES, (p + 1) * LANES)
        v_col = _as_column(kvw[:, W_ATT + p * LANES:W_ATT + (p + 1) * LANES])
        out_cols = []
        for hh in range(2):
            h = 2 * p + hh
            feat = slice(hh * HD_ATT, (hh + 1) * HD_ATT)
            pv = jnp.sum(cache_ref[1, h] * p_all[h:h + 1, :], axis=1, keepdims=True)
            out_cols.append(pv + c_self[h:h + 1, :] * v_col[feat, 0:1])
        col = jnp.broadcast_to(jnp.concatenate(out_cols, axis=0), (LANES, LANES))
        out_pairs.append(col.T[0:1, :])
    oatt_ref[row, :] = jnp.concatenate(out_pairs, axis=1)

    qr, kr, vr, g = qr_ref[row, :], kr_ref[row, :], vr_ref[row, :], g_ref[row, :]
    y_heads = []
    for h in range(H_RET):
        cols = slice(h * DK_RET, (h + 1) * DK_RET)
        gamma = math.exp(LOG_GAMMA[h])
        qh, kh, vh = qr[:, cols], kr[:, cols], vr[:, cols]
        S = st_ref[h]
        qk = jnp.sum(qh * kh, axis=1, keepdims=True)
        cross = jnp.dot(jnp.broadcast_to(qh, (16, DK_RET)).astype(BF16), S.astype(BF16),
                        preferred_element_type=F32)[:1]
        o = qk * vh + cross * gamma
        stout_ref[h] = gamma * S + _as_column(kh) * vh
        y_heads.append(_group_norm_gate(o, g[:, cols]))
    yret_ref[row, :] = jnp.concatenate(y_heads, axis=1)


def _mixer_operands(qa, kvw, qr, kr, vr, g, cache_t, state, tap_bias):
    B, _, _, _, W = cache_t.shape
    assert W == MAX_WINDOW and PAST_LEN >= MAX_WINDOW
    vec = lambda width: pl.BlockSpec((B, width), lambda b: (0, 0))
    bias0 = jnp.broadcast_to(tap_bias[0, :, 0:1], (H_ATT, LANES))
    st_spec = pl.BlockSpec((None, H_RET, DK_RET, DV_RET), lambda b: (b, 0, 0, 0))
    in_specs = [vec(W_ATT), vec(2 * W_ATT), vec(W_RET), vec(W_RET), vec(W_RET), vec(W_RET),
                pl.BlockSpec((None, 2, H_ATT, HD_ATT, W), lambda b: (b, 0, 0, 0, 0)),
                st_spec,
                pl.BlockSpec((len(DILATIONS), H_ATT, W), lambda b: (0, 0, 0)),
                pl.BlockSpec((H_ATT, LANES), lambda b: (0, 0))]
    args = (qa, kvw, qr, kr, vr, g, cache_t, state, _window_bias(tap_bias, W), bias0)
    out_specs = [vec(W_ATT), vec(W_RET), st_spec]
    out_shape = [jax.ShapeDtypeStruct((B, W_ATT), F32), jax.ShapeDtypeStruct((B, W_RET), F32),
                 jax.ShapeDtypeStruct(state.shape, F32)]
    return in_specs, args, out_specs, out_shape


FF_CHUNK = 1024
FFN_MIN_ROWS = 128


def _layernorm(z, g, b):
    mu = jnp.mean(z, axis=1, keepdims=True)
    var = jnp.mean(jnp.square(z - mu), axis=1, keepdims=True)
    return (z - mu) * lax.rsqrt(var + LN_EPS) * g + b


def _out_ffn_body(x_ref, oatt_ref, yret_ref, wo_ref, g1_ref, b1_ref, wu_ref, wd_ref, g2_ref, b2_ref, y_ref):
    tm = x_ref.shape[0]
    n_parts = 2 if tm >= 2 * FFN_MIN_ROWS else 1
    parts = [slice(i * (tm // n_parts), (i + 1) * (tm // n_parts)) for i in range(n_parts)]
    mixes = [jnp.dot(oatt_ref[r, :].astype(BF16), wo_ref[:W_ATT, :], preferred_element_type=F32)
             + jnp.dot(yret_ref[r, :].astype(BF16), wo_ref[W_ATT:, :], preferred_element_type=F32) for r in parts]
    n_chunks = D_FF // FF_CHUNK
    for r, mix in zip(parts, mixes):
        x1 = _layernorm(ALPHA * x_ref[r, :] + mix, g1_ref[...], b1_ref[...])
        x1b = x1.astype(BF16)

        def up(j):
            cols = slice(j * FF_CHUNK, (j + 1) * FF_CHUNK)
            return jnp.square(jnp.maximum(jnp.dot(x1b, wu_ref[:, cols], preferred_element_type=F32),
                                          0.0)).astype(BF16)

        ffn = jnp.zeros_like(x1)
        h = up(0)
        for j in range(n_chunks):
            h_next = up(j + 1) if j + 1 < n_chunks else None
            ffn = ffn + jnp.dot(h, wd_ref[j * FF_CHUNK:(j + 1) * FF_CHUNK, :], preferred_element_type=F32)
            h = h_next
        y_ref[r, :] = _layernorm(ALPHA * x1 + ffn, g2_ref[...], b2_ref[...])


def _ffn_operands(x, oatt, yret, wo_b, g1, b1, wu_b, wd_b, g2, b2, tm):
    row = lambda i: (i, 0)
    const = lambda shape: pl.BlockSpec(shape, lambda i: (0, 0), pipeline_mode=pl.Buffered(1))
    in_specs = [pl.BlockSpec((tm, D_MODEL), row), pl.BlockSpec((tm, W_ATT), row), pl.BlockSpec((tm, W_RET), row),
                const((W_ATT + W_RET, D_MODEL)), const((1, D_MODEL)), const((1, D_MODEL)),
                const((D_MODEL, D_FF)), const((D_FF, D_MODEL)), const((1, D_MODEL)), const((1, D_MODEL))]
    args = (x, oatt, yret, wo_b, g1, b1, wu_b, wd_b, g2, b2)
    return in_specs, args, pl.BlockSpec((tm, D_MODEL), row), jax.ShapeDtypeStruct((x.shape[0], D_MODEL), F32)


def _out_ffn(ffn_args, tm, name):
    in_specs, args, out_spec, out_shape = _ffn_operands(*ffn_args, tm)
    return pl.pallas_call(
        _out_ffn_body, grid=(args[0].shape[0] // tm,), in_specs=in_specs, out_specs=out_spec, out_shape=out_shape,
        compiler_params=_cparams(1), name=name,
    )(*args)


def _ffn_and_mixer_body(*refs, n_ffn_in, n_mix_in):
    ffn_in, mix_in = refs[:n_ffn_in], refs[n_ffn_in:n_ffn_in + n_mix_in]
    y_ref, *mix_out = refs[n_ffn_in + n_mix_in:]
    _sample_mixer_body(*mix_in, *mix_out)
    _out_ffn_body(*ffn_in, y_ref)


def _out_ffn_prompt_and_sample_mixer(ffn_args, mixer_args):
    B = mixer_args[0].shape[0]
    tm = ffn_args[0].shape[0] // B
    f_in, f_args, f_out, f_shape = _ffn_operands(*ffn_args, tm)
    m_in, m_args, m_out, m_shape = _mixer_operands(*mixer_args)
    y, oatt, yret, st = pl.pallas_call(
        functools.partial(_ffn_and_mixer_body, n_ffn_in=len(f_in), n_mix_in=len(m_in)),
        grid=(B,), in_specs=f_in + m_in, out_specs=[f_out] + m_out, out_shape=[f_shape] + m_shape,
        compiler_params=_cparams(1), name="out_ffn_prompt_sample_mixer",
    )(*f_args, *m_args)
    return y, oatt, yret, st


PROMPT_ROWS = 512


def kernel(x_prompt, x_sample, cache_kv_win, state_ret, w_in, rel_bias, w_out,
           ln1_g, ln1_b, w_up, w_down, ln2_g, ln2_b):
    assert x_prompt.shape[0] == 1 and x_sample.shape[1] == 1 and w_in.shape[0] == DEPTH
    S = x_prompt.shape[1]
    B = x_sample.shape[0]
    w_in_b = w_in[0].astype(BF16)
    g1, b1, g2, b2 = ln1_g[0][None], ln1_b[0][None], ln2_g[0][None], ln2_b[0][None]
    tap_bias = _tap_bias(rel_bias)

    xp = x_prompt[0]
    *qkv_views, kvw_p, yret, st_p, wo_b, wu_b, wd_b = _inproj_prompt(
        xp, w_in_b, (w_out[0], w_up[0], w_down[0]), PROMPT_ROWS)
    oatt = _attn_prompt(*qkv_views, _band_bias_rows(tap_bias))

    xs = x_sample[:, 0]
    qa_s, kvw_s, qr_s, kr_s, vr_s, g_s = _inproj_sample(xs, w_in_b)
    cache_t = jnp.transpose(cache_kv_win[0], (0, 2, 3, 4, 1))
    ffn_weights = (wo_b, g1, b1, wu_b, wd_b, g2, b2)
    y_p, oatt_s, yret_s, st_s = _out_ffn_prompt_and_sample_mixer(
        (xp, oatt, yret) + ffn_weights, (qa_s, kvw_s, qr_s, kr_s, vr_s, g_s, cache_t, state_ret[0], tap_bias))
    y_s = _out_ffn((xs, oatt_s, yret_s) + ffn_weights, B, "out_ffn_sample")

    win = kvw_p.shape[1]
    kv_win_p = jnp.transpose(kvw_p.reshape(2, H_ATT, HD_ATT, win), (3, 0, 1, 2))
    return (y_p[None], y_s[:, None],
            kv_win_p[None, None], kvw_s.reshape(1, B, 1, 2, H_ATT, HD_ATT),
            st_p[None, None], st_s[None])
```

```python
import functools
import math

import jax
import jax.numpy as jnp
from jax import lax
from jax.experimental import pallas as pl
from jax.experimental.pallas import tpu as pltpu

F32 = jnp.float32
BF16 = jnp.bfloat16

D_MODEL = 1024
DEPTH = 1
PAST_LEN = 16384
W_ATT = 512
HD_ATT = 64
H_ATT = 8
DILATIONS = ((128, 1), (512, 4), (2048, 16))
TAPS = 128
MAX_WINDOW = 2048
N_BUCKETS = 32
MAX_EXACT = N_BUCKETS // 2
W_RET = 512
H_RET = 4
DK_RET = 128
DV_RET = 128
RET_CHUNK = 128
ROPE_BASE = 10000.0
D_FF = 4096
N_SPLITS = 7
ALPHA = (2.0 * DEPTH) ** 0.25
LN_EPS = 1e-5
GN_EPS = 1e-6
LOG_GAMMA = tuple(math.log(1.0 - 2.0 ** (-5.0 - h)) for h in range(H_RET))

LANES = 128
PAIRS = W_ATT // LANES
NEG = -1e30
LOG2E = math.log2(math.e)

ATT_BLOCK = 2048
Q_ROWS = 128
VMEM_LIMIT = 56 * 1024 * 1024


def _cparams(n_axes):
    return pltpu.CompilerParams(dimension_semantics=("arbitrary",) * n_axes, vmem_limit_bytes=VMEM_LIMIT)


def _rotary(u, cos, sin_signed):
    outs = []
    for h in range(H_RET):
        xh = u[:, h * LANES:(h + 1) * LANES]
        outs.append(xh * cos + pltpu.roll(xh, LANES // 2, 1) * sin_signed)
    return jnp.concatenate(outs, axis=1)


def _proj(xb, w_ref, j):
    return jnp.dot(xb, w_ref[:, j * W_ATT:(j + 1) * W_ATT], preferred_element_type=F32)


def _store_rows(val, rows, nat_ref, rows_scr):
    for p in range(PAIRS):
        x = val[:, p * LANES:(p + 1) * LANES]
        nat_ref[p, rows, :] = x.astype(BF16)
        rows_scr[p, rows, :] = x


def _store_gathered(by4_ref, by16_ref, rows_scr, by4_scr):
    tm = rows_scr.shape[1]
    for p in range(PAIRS):
        for r4 in range(4):
            a = rows_scr[p, pl.ds(r4, tm // 4, stride=4), :]
            by4_ref[p, :, r4 * LANES:(r4 + 1) * LANES] = a.astype(BF16)
            by4_scr[p * 4 + r4] = a
    for p in range(PAIRS):
        for r4 in range(4):
            for rr in range(4):
                r16 = r4 + 4 * rr
                b = by4_scr[p * 4 + r4, pl.ds(rr, tm // 16, stride=4), :]
                by16_ref[p, :, r16 * LANES:(r16 + 1) * LANES] = b.astype(BF16)


def _retention_qkvg(xb, w_ref, cos, sin):
    return (_rotary(_proj(xb, w_ref, 3), cos, sin), _rotary(_proj(xb, w_ref, 4), cos, sin) * DK_RET ** -0.5,
            _proj(xb, w_ref, 5), _proj(xb, w_ref, 6))


def _inproj_prompt_body(*refs, win_first_step, n_cast):
    x_ref, w_ref, cos_row_ref, sin_row_ref, cos_step_ref, sin_step_ref = refs[:6]
    cast_in, outs = refs[6:6 + n_cast], refs[6 + n_cast:]
    q1_ref, q4_ref, q16_ref, k1_ref, k4_ref, k16_ref, v1_ref, v4_ref, v16_ref, kvw_ref, yret_ref, st_ref = outs[:12]
    cast_out = outs[12:12 + n_cast]
    q_rows, q_by4, k_rows, k_by4, v_rows, v_by4, state = outs[12 + n_cast:]

    @pl.when(pl.program_id(0) == 0)
    def _():
        state[...] = jnp.zeros_like(state)

    for src, dst in zip(cast_in, cast_out):
        dst[...] = src[...].astype(BF16)

    xb = x_ref[...].astype(BF16)
    tm = xb.shape[0]
    half = tm // 2

    step = pl.ds(pl.program_id(0), 1)
    ca, sa = cos_step_ref[step, :], sin_step_ref[step, :]
    cb, sb = cos_row_ref[...], sin_row_ref[...]
    sign = jnp.where(lax.broadcasted_iota(jnp.int32, (1, LANES), 1) < LANES // 2, -1.0, 1.0)
    cos, sin = ca * cb - sa * sb, (sa * cb + ca * sb) * sign

    def retention_proj(j, rows):
        u = _proj(xb[rows], w_ref, 3 + j)
        if j < 2:
            u = _rotary(u, cos[rows], sin[rows])
        return u * DK_RET ** -0.5 if j == 1 else u

    attn_out = ((q1_ref, q4_ref, q16_ref, q_rows, q_by4), (k1_ref, k4_ref, k16_ref, k_rows, k_by4),
                (v1_ref, v4_ref, v16_ref, v_rows, v_by4))

    def attention_proj(j, rows, last):
        nat_ref, by4_ref, by16_ref, rows_scr, by4_scr = attn_out[j]
        val = _proj(xb[rows], w_ref, j)
        _store_rows(val * (HD_ATT ** -0.5 * LOG2E) if j == 0 else val, rows, nat_ref, rows_scr)
        if last:
            _store_gathered(by4_ref, by16_ref, rows_scr, by4_scr)

    halves = (slice(0, half), slice(half, tm))
    ret_in = [[retention_proj(j, halves[0]) for j in range(4)], [None] * 4]
    fills = [[("ret", 0), ("ret", 1), ("ret", 2), ("ret", 3), ("att", 0, 0)],
             [("att", 0, 1), ("att", 1, 0), ("att", 1, 1), ("att", 2, 0), ("att", 2, 1)]]
    decays = [_retention_decays(h) for h in range(H_RET)]
    S = [state[h] for h in range(H_RET)]

    def emit(fill):
        if fill[0] == "ret":
            ret_in[1][fill[1]] = retention_proj(fill[1], halves[1])
        else:
            attention_proj(fill[1], halves[fill[2]], last=fill[2] == 1)

    for ph in range(2):
        qr, kr, vr, g = ret_in[ph]
        pending = list(fills[ph])
        for c in range(half // RET_CHUNK):
            rows = slice(c * RET_CHUNK, (c + 1) * RET_CHUNK)
            out_rows = slice(ph * half + c * RET_CHUNK, ph * half + (c + 1) * RET_CHUNK)
            staged = []
            for h in range(H_RET):
                cols = slice(h * DK_RET, (h + 1) * DK_RET)
                scores, cross, vb, S[h] = _retention_chunk_matmuls(qr[rows, cols], kr[rows, cols], vr[rows, cols],
                                                                  S[h], decays[h])
                staged.append((scores, cross, vb))
            emit(pending.pop(0))
            for h in range(H_RET):
                cols = slice(h * DK_RET, (h + 1) * DK_RET)
                yret_ref[out_rows, cols] = _retention_chunk_output(*staged[h], g[rows, cols]).astype(BF16)
            emit(pending.pop(0))
        while pending:
            emit(pending.pop(0))
    for h in range(H_RET):
        state[h] = S[h]
        st_ref[h] = S[h]

    @pl.when(pl.program_id(0) >= win_first_step)
    def _():
        for p in range(PAIRS):
            kvw_ref[p * LANES:(p + 1) * LANES, :] = k_rows[p].T
            kvw_ref[W_ATT + p * LANES:W_ATT + (p + 1) * LANES, :] = v_rows[p].T


def _inproj_sample_body(x_ref, w_ref, cos_ref, sin_ref, qa_ref, kvw_ref, qr_ref, kr_ref, vr_ref, g_ref):
    xb = x_ref[...].astype(BF16)
    qa_ref[...] = _proj(xb, w_ref, 0) * HD_ATT ** -0.5
    kvw_ref[:, :W_ATT] = _proj(xb, w_ref, 1)
    kvw_ref[:, W_ATT:] = _proj(xb, w_ref, 2)
    qr, kr, vr, g = _retention_qkvg(xb, w_ref, cos_ref[...], sin_ref[...])
    qr_ref[...] = qr
    kr_ref[...] = kr
    vr_ref[...] = vr
    g_ref[...] = g


def _rope_tables(pos):
    half = DK_RET // 2
    inv_freq = 1.0 / (ROPE_BASE ** jnp.linspace(0.0, 1.0, half, dtype=F32))
    ang = pos.astype(F32)[:, None] * inv_freq[None, :]
    cos, sin = jnp.cos(ang), jnp.sin(ang)
    return jnp.concatenate([cos, cos], axis=1), jnp.concatenate([sin, sin], axis=1)


def _inproj_prompt(x, w_in_b, later_weights, tm):
    S = x.shape[0]
    win = min(MAX_WINDOW, S)
    steps = S // tm
    win_first_step = steps - win // tm
    cos, sin = _rope_tables(jnp.concatenate([jnp.arange(tm, dtype=jnp.int32),
                                             jnp.arange(steps, dtype=jnp.int32) * tm]))
    const = lambda i: (0, 0)
    row = lambda i: (i, 0)
    view_specs = [pl.BlockSpec((PAIRS, tm // d, d * LANES), lambda i: (0, i, 0)) for d in (1, 4, 16)] * 3
    view_shapes = [jax.ShapeDtypeStruct((PAIRS, S // d, d * LANES), BF16) for d in (1, 4, 16)] * 3
    state_shape = (H_RET, DK_RET, DV_RET)
    gather_scratch = [pltpu.VMEM((PAIRS, tm, LANES), F32), pltpu.VMEM((4 * PAIRS, tm // 4, LANES), F32)]
    slab_specs = [pl.BlockSpec((w.shape[0] // steps, w.shape[1]), row) for w in later_weights]
    return pl.pallas_call(
        functools.partial(_inproj_prompt_body, win_first_step=win_first_step, n_cast=len(later_weights)),
        grid=(steps,),
        in_specs=[pl.BlockSpec((tm, D_MODEL), row),
                  pl.BlockSpec((D_MODEL, N_SPLITS * W_ATT), lambda i: (0, 0)),
                  pl.BlockSpec((tm, LANES), const), pl.BlockSpec((tm, LANES), const),
                  pl.BlockSpec((steps, LANES), const), pl.BlockSpec((steps, LANES), const)] + slab_specs,
        out_specs=view_specs + [
            pl.BlockSpec((2 * W_ATT, tm), lambda i: (0, jnp.maximum(i - win_first_step, 0))),
            pl.BlockSpec((tm, W_RET), row),
            pl.BlockSpec(state_shape, lambda i: (0, 0, 0))] + slab_specs,
        out_shape=view_shapes + [
            jax.ShapeDtypeStruct((2 * W_ATT, win), F32),
            jax.ShapeDtypeStruct((S, W_RET), BF16),
            jax.ShapeDtypeStruct(state_shape, F32)] + [jax.ShapeDtypeStruct(w.shape, BF16) for w in later_weights],
        scratch_shapes=gather_scratch * 3 + [pltpu.VMEM(state_shape, F32)],
        compiler_params=_cparams(1),
        name="inproj_retention_prompt",
    )(x, w_in_b, cos[:tm], sin[:tm], cos[tm:], sin[tm:], *later_weights)


def _inproj_sample(x, w_in_b):
    B = x.shape[0]
    cos, sin = _rope_tables(jnp.full((B,), PAST_LEN, dtype=jnp.int32))
    sin = sin * jnp.where(jnp.arange(LANES) < LANES // 2, -1.0, 1.0)
    full = lambda shape: pl.BlockSpec(shape, lambda i: (0, 0))
    return pl.pallas_call(
        _inproj_sample_body,
        grid=(1,),
        in_specs=[full((B, D_MODEL)), full((D_MODEL, N_SPLITS * W_ATT)), full((B, LANES)), full((B, LANES))],
        out_specs=[full((B, W_ATT)), full((B, 2 * W_ATT)), full((B, W_RET)), full((B, W_RET)),
                   full((B, W_RET)), full((B, W_RET))],
        out_shape=[jax.ShapeDtypeStruct((B, W_ATT), F32),
                   jax.ShapeDtypeStruct((B, 2 * W_ATT), F32),
                   jax.ShapeDtypeStruct((B, W_RET), F32),
                   jax.ShapeDtypeStruct((B, W_RET), F32),
                   jax.ShapeDtypeStruct((B, W_RET), F32),
                   jax.ShapeDtypeStruct((B, W_RET), F32)],
        compiler_params=_cparams(1),
        name="inproj_sample",
    )(x, w_in_b, cos, sin)


def _t5_bucket(dist):
    is_small = dist < MAX_EXACT
    d_f = jnp.maximum(dist, 1).astype(F32)
    large = MAX_EXACT + (jnp.log(d_f / MAX_EXACT) / math.log(MAX_WINDOW / MAX_EXACT)
                         * (N_BUCKETS - MAX_EXACT)).astype(jnp.int32)
    large = jnp.minimum(large, N_BUCKETS - 1)
    return jnp.where(is_small, dist, large)


def _tap_bias(rel_bias):
    taps = jnp.arange(TAPS + 1)
    return jnp.stack([rel_bias[_t5_bucket(taps * d)].astype(F32).T for _, d in DILATIONS])


def _band_bias_rows(tap_bias):
    rows = jnp.concatenate([tap_bias[..., ::-1] * LOG2E,
                            jnp.full(tap_bias.shape[:-1] + (2 * Q_ROWS - TAPS - 1,), NEG, F32)], axis=-1)
    return rows.reshape(len(DILATIONS), PAIRS, 2, 2 * Q_ROWS)


def _window_bias(tap_bias, window):
    rows = []
    for i, (_, d) in enumerate(DILATIONS):
        taps_rev = tap_bias[i, :, TAPS:0:-1]
        seg = jnp.concatenate([taps_rev[..., None], jnp.full((H_ATT, TAPS, d - 1), NEG, F32)], axis=-1)
        rows.append(jnp.concatenate([jnp.full((H_ATT, window - TAPS * d), NEG, F32),
                                     seg.reshape(H_ATT, TAPS * d)], axis=-1))
    return jnp.stack(rows)


def _attn_probs(q, k, bias2, head0_b):
    qs = jnp.concatenate([q * head0_b, q * (1 - head0_b)], axis=0)
    s = lax.dot_general(qs, k, (((1,), (1,)), ((), ())), preferred_element_type=F32) + bias2
    m = jnp.max(s, axis=1, keepdims=True)
    p = jnp.exp2(s - m)
    return p.astype(BF16), jnp.sum(p, axis=1, keepdims=True), m


def _attn_output(p, l, m, v, head0):
    o = jnp.dot(p, v, preferred_element_type=F32) / l
    lse = m + jnp.log2(l)
    o_pair = jnp.where(head0, o[:Q_ROWS], o[Q_ROWS:])
    lse_pair = jnp.where(head0, jnp.broadcast_to(lse[:Q_ROWS], (Q_ROWS, LANES)),
                         jnp.broadcast_to(lse[Q_ROWS:], (Q_ROWS, LANES)))
    return o_pair, lse_pair


def _attn_body(q1, q4, q16, k1c, k1p, k4c, k4p, k16c, k16p, v1c, v1p, v4c, v4p, v16c, v16p, bias_ref,
               out_ref, o16, l16, o4, l4, band):
    first = pl.program_id(1) == 0
    lane = lax.broadcasted_iota(jnp.int32, (Q_ROWS, LANES), 1)
    head0 = lane < HD_ATT
    head0_b = jnp.where(head0, 1.0, 0.0).astype(BF16)
    col = lax.broadcasted_iota(jnp.int32, (2 * Q_ROWS, 2 * Q_ROWS), 1)
    no_prev = jnp.where(jnp.logical_and(first, col < Q_ROWS), NEG, 0.0)

    @pl.when(first)
    def _():
        for branch in range(len(DILATIONS)):
            for hh in range(2):
                row = jnp.broadcast_to(bias_ref[branch, hh:hh + 1, :], (Q_ROWS, 2 * Q_ROWS))
                band[branch, hh * Q_ROWS:(hh + 1) * Q_ROWS, :] = pltpu.roll(row, 0, 1, stride=1, stride_axis=0)

    def bias2(branch, masked_prev):
        return band[branch] + no_prev if masked_prev else band[branch]

    def cat(a, b):
        return jnp.concatenate([a, b], axis=0)

    def window(cur, prev, b, sl):
        if b == 0:
            return cat(prev[:, sl], cur[0:Q_ROWS, sl])
        return cur[(b - 1) * Q_ROWS:(b + 1) * Q_ROWS, sl]

    units = []
    b16 = bias2(2, True)
    for r in range(16):
        sl = slice(r * LANES, (r + 1) * LANES)

        def store16(o, l, r=r):
            o16[pl.ds(r, Q_ROWS, stride=16), :] = o
            l16[pl.ds(r, Q_ROWS, stride=16), :] = l

        units.append((lambda sl=sl: q16[:, sl], lambda sl=sl: window(k16c, k16p, 0, sl),
                      lambda sl=sl: window(v16c, v16p, 0, sl), b16, store16))

    b4 = (bias2(1, True), bias2(1, False))
    for r in range(4):
        sl = slice(r * LANES, (r + 1) * LANES)
        for b in range(4):
            rows = slice(b * Q_ROWS, (b + 1) * Q_ROWS)

            def store4(o, l, r=r, b=b):
                o4[pl.ds(b * 4 * Q_ROWS + r, Q_ROWS, stride=4), :] = o
                l4[pl.ds(b * 4 * Q_ROWS + r, Q_ROWS, stride=4), :] = l

            units.append((lambda rows=rows, sl=sl: q4[rows, sl], lambda b=b, sl=sl: window(k4c, k4p, b, sl),
                          lambda b=b, sl=sl: window(v4c, v4p, b, sl), b4[min(b, 1)], store4))

    b1 = (bias2(0, True), bias2(0, False))
    everything = slice(None)
    for b in range(ATT_BLOCK // Q_ROWS):
        rows = slice(b * Q_ROWS, (b + 1) * Q_ROWS)

        def mix(oa, la, rows=rows):
            ob, lb = o4[rows, :], l4[rows, :]
            oc, lc = o16[rows, :], l16[rows, :]
            top = jnp.maximum(jnp.maximum(la, lb), lc)
            ea, eb, ec = jnp.exp2(la - top), jnp.exp2(lb - top), jnp.exp2(lc - top)
            out_ref[rows, :] = ((ea * oa + eb * ob + ec * oc) / (ea + eb + ec)).astype(BF16)

        units.append((lambda rows=rows: q1[rows, :], lambda b=b: window(k1c, k1p, b, everything),
                      lambda b=b: window(v1c, v1p, b, everything), b1[min(b, 1)], mix))

    pending = None
    for q, k, v, bias, finish in units:
        probs = _attn_probs(q(), k(), bias, head0_b)
        if pending is not None:
            p_probs, p_v, p_finish = pending
            p_finish(*_attn_output(*p_probs, p_v(), head0))
        pending = (probs, v, finish)
    p_probs, p_v, p_finish = pending
    p_finish(*_attn_output(*p_probs, p_v(), head0))


def _attn_prompt(q1, q4, q16, k1, k4, k16, v1, v4, v16, band_bias):
    S = q1.shape[1]
    assert S % ATT_BLOCK == 0
    steps = S // ATT_BLOCK

    def cur(d):
        return pl.BlockSpec((None, ATT_BLOCK // d, d * LANES), lambda p, c: (p, c, 0))

    def prev(d):
        per_step = ATT_BLOCK // d // Q_ROWS
        return pl.BlockSpec((None, Q_ROWS, d * LANES), lambda p, c: (p, jnp.maximum(c * per_step - 1, 0), 0))

    scratch = pltpu.VMEM((ATT_BLOCK, LANES), F32)
    return pl.pallas_call(
        _attn_body,
        grid=(PAIRS, steps),
        in_specs=[cur(1), cur(4), cur(16),
                  cur(1), prev(1), cur(4), prev(4), cur(16), prev(16),
                  cur(1), prev(1), cur(4), prev(4), cur(16), prev(16),
                  pl.BlockSpec((len(DILATIONS), None, 2, 2 * Q_ROWS), lambda p, c: (0, p, 0, 0))],
        out_specs=pl.BlockSpec((ATT_BLOCK, LANES), lambda p, c: (c, p)),
        out_shape=jax.ShapeDtypeStruct((S, W_ATT), BF16),
        scratch_shapes=[scratch, scratch, scratch, scratch,
                        pltpu.VMEM((len(DILATIONS), 2 * Q_ROWS, 2 * Q_ROWS), F32)],
        compiler_params=_cparams(2),
        name="attn_prompt",
    )(q1, q4, q16, k1, k1, k4, k4, k16, k16, v1, v1, v4, v4, v16, v16, band_bias)


def _group_norm_gate(o, g):
    mu = jnp.mean(o, axis=1, keepdims=True)
    var = jnp.mean(jnp.square(o - mu), axis=1, keepdims=True)
    return g * jax.nn.sigmoid(g) * ((o - mu) * lax.rsqrt(var + GN_EPS))


def _retention_decays(h):
    C = RET_CHUNK
    lg = LOG_GAMMA[h]
    n = lax.broadcasted_iota(jnp.int32, (C, 1), 0).astype(F32)
    diff = (lax.broadcasted_iota(jnp.int32, (C, C), 0) - lax.broadcasted_iota(jnp.int32, (C, C), 1)).astype(F32)
    return (jnp.where(diff >= 0, jnp.exp(lg * jnp.maximum(diff, 0.0)), 0.0),
            jnp.exp(lg * (n + 1.0)), jnp.exp(lg * (C - 1.0 - n)), math.exp(lg * C))


def _retention_chunk_matmuls(q, k, v, S, decays):
    decay, q_decay, k_decay, chunk_decay = decays
    qb, vb = q.astype(BF16), v.astype(BF16)
    scores = lax.dot_general(qb, k.astype(BF16), (((1,), (1,)), ((), ())), preferred_element_type=F32) * decay
    cross = jnp.dot(qb, S.astype(BF16), preferred_element_type=F32) * q_decay
    k_dec_t = (k * k_decay).T.astype(BF16)
    return scores.astype(BF16), cross, vb, chunk_decay * S + jnp.dot(k_dec_t, vb, preferred_element_type=F32)


def _retention_chunk_output(scores, cross, vb, g):
    return _group_norm_gate(jnp.dot(scores, vb, preferred_element_type=F32) + cross, g)


def _as_column(row_pair):
    return jnp.broadcast_to(row_pair, (LANES, LANES)).T


def _sample_mixer_body(qa_ref, kvw_ref, qr_ref, kr_ref, vr_ref, g_ref, cache_ref, st_ref, bias_ref, bias0_ref,
                       oatt_ref, yret_ref, stout_ref):
    W = cache_ref.shape[-1]
    row = pl.ds(pl.program_id(0), 1)
    qa, kvw = qa_ref[row, :], kvw_ref[row, :]
    logit_rows, self_rows = [], []
    for p in range(PAIRS):
        lanes = slice(p * LANES, (p + 1) * LANES)
        q_pair = qa[:, lanes]
        k_pair = kvw[:, lanes]
        q_col = _as_column(q_pair)
        for hh in range(2):
            feat = slice(hh * HD_ATT, (hh + 1) * HD_ATT)
            logit_rows.append(jnp.sum(cache_ref[0, 2 * p + hh] * q_col[feat, 0:1], axis=0, keepdims=True))
            self_rows.append(jnp.sum(q_pair[:, feat] * k_pair[:, feat], axis=1, keepdims=True))
    logits = jnp.concatenate(logit_rows, axis=0)
    s_self = jnp.concatenate(self_rows, axis=0) + bias0_ref[:, 0:1]

    probs, p_selfs, denoms, lses = [], [], [], []
    for i, (_, d) in enumerate(DILATIONS):
        lo = W - TAPS * d
        s = logits[:, lo:] + bias_ref[i, :, lo:]
        m = jnp.maximum(jnp.max(s, axis=1, keepdims=True), s_self)
        pr = jnp.exp(s - m)
        p_self = jnp.exp(s_self - m)
        l = jnp.sum(pr, axis=1, keepdims=True) + p_self
        probs.append(pr)
        p_selfs.append(p_self)
        denoms.append(l)
        lses.append(m + jnp.log(l))
    top = jnp.maximum(jnp.maximum(lses[0], lses[1]), lses[2])
    e = [jnp.exp(x - top) for x in lses]
    e_sum = e[0] + e[1] + e[2]
    coef = [e[i] / (e_sum * denoms[i]) for i in range(3)]
    lo4, lo1 = W - TAPS * 4, W - TAPS
    p16, p4, p1 = coef[2] * probs[2], coef[1] * probs[1], coef[0] * probs[0]
    p_all = jnp.concatenate([p16[:, :lo4], p16[:, lo4:lo1] + p4[:, :lo1 - lo4],
                             p16[:, lo1:] + p4[:, lo1 - lo4:] + p1], axis=1)
    c_self = coef[0] * p_selfs[0] + coef[1] * p_selfs[1] + coef[2] * p_selfs[2]
    out_pairs = []
    for p in range(PAIRS):
        lanes = slice(p * LANES, (p + 1) * LANES)
        v_col = _as_column(kvw[:, W_ATT + p * LANES:W_ATT + (p + 1) * LANES])
        out_cols = []
        for hh in range(2):
            h = 2 * p + hh
            feat = slice(hh * HD_ATT, (hh + 1) * HD_ATT)
            pv = jnp.sum(cache_ref[1, h] * p_all[h:h + 1, :], axis=1, keepdims=True)
            out_cols.append(pv + c_self[h:h + 1, :] * v_col[feat, 0:1])
        col = jnp.broadcast_to(jnp.concatenate(out_cols, axis=0), (LANES, LANES))
        out_pairs.append(col.T[0:1, :])
    oatt_ref[row, :] = jnp.concatenate(out_pairs, axis=1)

    qr, kr, vr, g = qr_ref[row, :], kr_ref[row, :], vr_ref[row, :], g_ref[row, :]
    y_heads = []
    for h in range(H_RET):
        cols = slice(h * DK_RET, (h + 1) * DK_RET)
        gamma = math.exp(LOG_GAMMA[h])
        qh, kh, vh = qr[:, cols], kr[:, cols], vr[:, cols]
        S = st_ref[h]
        qk = jnp.sum(qh * kh, axis=1, keepdims=True)
        cross = jnp.dot(jnp.broadcast_to(qh, (16, DK_RET)).astype(BF16), S.astype(BF16),
                        preferred_element_type=F32)[:1]
        o = qk * vh + cross * gamma
        stout_ref[h] = gamma * S + _as_column(kh) * vh
        y_heads.append(_group_norm_gate(o, g[:, cols]))
    yret_ref[row, :] = jnp.concatenate(y_heads, axis=1)


def _mixer_operands(qa, kvw, qr, kr, vr, g, cache_t, state, tap_bias):
    B, _, _, _, W = cache_t.shape
    assert W == MAX_WINDOW and PAST_LEN >= MAX_WINDOW
    vec = lambda width: pl.BlockSpec((B, width), lambda b: (0, 0))
    bias0 = jnp.broadcast_to(tap_bias[0, :, 0:1], (H_ATT, LANES))
    st_spec = pl.BlockSpec((None, H_RET, DK_RET, DV_RET), lambda b: (b, 0, 0, 0))
    in_specs = [vec(W_ATT), vec(2 * W_ATT), vec(W_RET), vec(W_RET), vec(W_RET), vec(W_RET),
                pl.BlockSpec((None, 2, H_ATT, HD_ATT, W), lambda b: (b, 0, 0, 0, 0)),
                st_spec,
                pl.BlockSpec((len(DILATIONS), H_ATT, W), lambda b: (0, 0, 0)),
                pl.BlockSpec((H_ATT, LANES), lambda b: (0, 0))]
    args = (qa, kvw, qr, kr, vr, g, cache_t, state, _window_bias(tap_bias, W), bias0)
    out_specs = [vec(W_ATT), vec(W_RET), st_spec]
    out_shape = [jax.ShapeDtypeStruct((B, W_ATT), F32), jax.ShapeDtypeStruct((B, W_RET), F32),
                 jax.ShapeDtypeStruct(state.shape, F32)]
    return in_specs, args, out_specs, out_shape


FF_CHUNK = 1024
FFN_MIN_ROWS = 128


def _layernorm(z, g, b):
    mu = jnp.mean(z, axis=1, keepdims=True)
    var = jnp.mean(jnp.square(z - mu), axis=1, keepdims=True)
    return (z - mu) * lax.rsqrt(var + LN_EPS) * g + b


def _out_ffn_body(x_ref, oatt_ref, yret_ref, wo_ref, g1_ref, b1_ref, wu_ref, wd_ref, g2_ref, b2_ref, y_ref):
    tm = x_ref.shape[0]
    n_parts = 2 if tm >= 2 * FFN_MIN_ROWS else 1
    parts = [slice(i * (tm // n_parts), (i + 1) * (tm // n_parts)) for i in range(n_parts)]
    mixes = [jnp.dot(oatt_ref[r, :].astype(BF16), wo_ref[:W_ATT, :], preferred_element_type=F32)
             + jnp.dot(yret_ref[r, :].astype(BF16), wo_ref[W_ATT:, :], preferred_element_type=F32) for r in parts]
    n_chunks = D_FF // FF_CHUNK
    for r, mix in zip(parts, mixes):
        x1 = _layernorm(ALPHA * x_ref[r, :] + mix, g1_ref[...], b1_ref[...])
        x1b = x1.astype(BF16)

        def up(j):
            cols = slice(j * FF_CHUNK, (j + 1) * FF_CHUNK)
            return jnp.square(jnp.maximum(jnp.dot(x1b, wu_ref[:, cols], preferred_element_type=F32),
                                          0.0)).astype(BF16)

        ffn = jnp.zeros_like(x1)
        h = up(0)
        for j in range(n_chunks):
            h_next = up(j + 1) if j + 1 < n_chunks else None
            ffn = ffn + jnp.dot(h, wd_ref[j * FF_CHUNK:(j + 1) * FF_CHUNK, :], preferred_element_type=F32)
            h = h_next
        y_ref[r, :] = _layernorm(ALPHA * x1 + ffn, g2_ref[...], b2_ref[...])


def _ffn_operands(x, oatt, yret, wo_b, g1, b1, wu_b, wd_b, g2, b2, tm):
    row = lambda i: (i, 0)
    const = lambda shape: pl.BlockSpec(shape, lambda i: (0, 0), pipeline_mode=pl.Buffered(1))
    in_specs = [pl.BlockSpec((tm, D_MODEL), row), pl.BlockSpec((tm, W_ATT), row), pl.BlockSpec((tm, W_RET), row),
                const((W_ATT + W_RET, D_MODEL)), const((1, D_MODEL)), const((1, D_MODEL)),
                const((D_MODEL, D_FF)), const((D_FF, D_MODEL)), const((1, D_MODEL)), const((1, D_MODEL))]
    args = (x, oatt, yret, wo_b, g1, b1, wu_b, wd_b, g2, b2)
    return in_specs, args, pl.BlockSpec((tm, D_MODEL), row), jax.ShapeDtypeStruct((x.shape[0], D_MODEL), F32)


def _out_ffn(ffn_args, tm, name):
    in_specs, args, out_spec, out_shape = _ffn_operands(*ffn_args, tm)
    return pl.pallas_call(
        _out_ffn_body, grid=(args[0].shape[0] // tm,), in_specs=in_specs, out_specs=out_spec, out_shape=out_shape,
        compiler_params=_cparams(1), name=name,
    )(*args)


def _ffn_and_mixer_body(*refs, n_ffn_in, n_mix_in):
    ffn_in, mix_in = refs[:n_ffn_in], refs[n_ffn_in:n_ffn_in + n_mix_in]
    y_ref, *mix_out = refs[n_ffn_in + n_mix_in:]
    _sample_mixer_body(*mix_in, *mix_out)
    _out_ffn_body(*ffn_in, y_ref)


def _out_ffn_prompt_and_sample_mixer(ffn_args, mixer_args):
    B = mixer_args[0].shape[0]
    tm = ffn_args[0].shape[0] // B
    f_in, f_args, f_out, f_shape = _ffn_operands(*ffn_args, tm)
    m_in, m_args, m_out, m_shape = _mixer_operands(*mixer_args)
    y, oatt, yret, st = pl.pallas_call(
        functools.partial(_ffn_and_mixer_body, n_ffn_in=len(f_in), n_mix_in=len(m_in)),
        grid=(B,), in_specs=f_in + m_in, out_specs=[f_out] + m_out, out_shape=[f_shape] + m_shape,
        compiler_params=_cparams(1), name="out_ffn_prompt_sample_mixer",
    )(*f_args, *m_args)
    return y, oatt, yret, st


PROMPT_ROWS = 512


def kernel(x_prompt, x_sample, cache_kv_win, state_ret, w_in, rel_bias, w_out,
           ln1_g, ln1_b, w_up, w_down, ln2_g, ln2_b):
    assert x_prompt.shape[0] == 1 and x_sample.shape[1] == 1 and w_in.shape[0] == DEPTH
    S = x_prompt.shape[1]
    B = x_sample.shape[0]
    w_in_b = w_in[0].astype(BF16)
    g1, b1, g2, b2 = ln1_g[0][None], ln1_b[0][None], ln2_g[0][None], ln2_b[0][None]
    tap_bias = _tap_bias(rel_bias)

    xp = x_prompt[0]
    *qkv_views, kvw_p, yret, st_p, wo_b, wu_b, wd_b = _inproj_prompt(
        xp, w_in_b, (w_out[0], w_up[0], w_down[0]), PROMPT_ROWS)
    oatt = _attn_prompt(*qkv_views, _band_bias_rows(tap_bias))

    xs = x_sample[:, 0]
    qa_s, kvw_s, qr_s, kr_s, vr_s, g_s = _inproj_sample(xs, w_in_b)
    cache_t = jnp.transpose(cache_kv_win[0], (0, 2, 3, 4, 1))
    ffn_weights = (wo_b, g1, b1, wu_b, wd_b, g2, b2)
    y_p, oatt_s, yret_s, st_s = _out_ffn_prompt_and_sample_mixer(
        (xp, oatt, yret) + ffn_weights, (qa_s, kvw_s, qr_s, kr_s, vr_s, g_s, cache_t, state_ret[0], tap_bias))
    y_s = _out_ffn((xs, oatt_s, yret_s) + ffn_weights, B, "out_ffn_sample")

    win = kvw_p.shape[1]
    kv_win_p = jnp.transpose(kvw_p.reshape(2, H_ATT, HD_ATT, win), (3, 0, 1, 2))
    return (y_p[None], y_s[:, None],
            kv_win_p[None, None], kvw_s.reshape(1, B, 1, 2, H_ATT, HD_ATT),
            st_p[None, None], st_s[None])
```

```python
import functools
import math

import jax
import jax.numpy as jnp
from jax import lax
from jax.experimental import pallas as pl
from jax.experimental.pallas import tpu as pltpu

F32 = jnp.float32
BF16 = jnp.bfloat16

D_MODEL = 1024
DEPTH = 1
PAST_LEN = 16384
W_ATT = 512
HD_ATT = 64
H_ATT = 8
DILATIONS = ((128, 1), (512, 4), (2048, 16))
TAPS = 128
MAX_WINDOW = 2048
N_BUCKETS = 32
MAX_EXACT = N_BUCKETS // 2
W_RET = 512
H_RET = 4
DK_RET = 128
DV_RET = 128
RET_CHUNK = 128
ROPE_BASE = 10000.0
D_FF = 4096
N_SPLITS = 7
ALPHA = (2.0 * DEPTH) ** 0.25
LN_EPS = 1e-5
GN_EPS = 1e-6
LOG_GAMMA = tuple(math.log(1.0 - 2.0 ** (-5.0 - h)) for h in range(H_RET))

LANES = 128
PAIRS = W_ATT // LANES
NEG = -1e30
LOG2E = math.log2(math.e)

ATT_BLOCK = 2048
Q_ROWS = 128
VMEM_LIMIT = 56 * 1024 * 1024


def _cparams(n_axes):
    return pltpu.CompilerParams(dimension_semantics=("arbitrary",) * n_axes, vmem_limit_bytes=VMEM_LIMIT)


def _rotary(u, cos, sin_signed):
    outs = []
    for h in range(H_RET):
        xh = u[:, h * LANES:(h + 1) * LANES]
        outs.append(xh * cos + pltpu.roll(xh, LANES // 2, 1) * sin_signed)
    return jnp.concatenate(outs, axis=1)


def _proj(xb, w_ref, j):
    return jnp.dot(xb, w_ref[:, j * W_ATT:(j + 1) * W_ATT], preferred_element_type=F32)


def _store_rows(val, rows, nat_ref, rows_scr):
    for p in range(PAIRS):
        x = val[:, p * LANES:(p + 1) * LANES]
        nat_ref[p, rows, :] = x.astype(BF16)
        rows_scr[p, rows, :] = x


def _store_gathered(by4_ref, by16_ref, rows_scr, by4_scr):
    tm = rows_scr.shape[1]
    for p in range(PAIRS):
        for r4 in range(4):
            a = rows_scr[p, pl.ds(r4, tm // 4, stride=4), :]
            by4_ref[p, :, r4 * LANES:(r4 + 1) * LANES] = a.astype(BF16)
            by4_scr[p * 4 + r4] = a
    for p in range(PAIRS):
        for r4 in range(4):
            for rr in range(4):
                r16 = r4 + 4 * rr
                b = by4_scr[p * 4 + r4, pl.ds(rr, tm // 16, stride=4), :]
                by16_ref[p, :, r16 * LANES:(r16 + 1) * LANES] = b.astype(BF16)


def _retention_qkvg(xb, w_ref, cos, sin):
    return (_rotary(_proj(xb, w_ref, 3), cos, sin), _rotary(_proj(xb, w_ref, 4), cos, sin) * DK_RET ** -0.5,
            _proj(xb, w_ref, 5), _proj(xb, w_ref, 6))


N_SAMPLE_IN = 3
N_SAMPLE_OUT = 6


def _inproj_prompt_body(*refs, win_first_step, n_steps, n_cast):
    x_ref, w_f32_ref, cos_row_ref, sin_row_ref, cos_step_ref, sin_step_ref = refs[:6]
    sample_in = refs[6:6 + N_SAMPLE_IN]
    cast_in, outs = refs[6 + N_SAMPLE_IN:6 + N_SAMPLE_IN + n_cast], refs[6 + N_SAMPLE_IN + n_cast:]
    q1_ref, q4_ref, q16_ref, k1_ref, k4_ref, k16_ref, v1_ref, v4_ref, v16_ref, kvw_ref, yret_ref, st_ref = outs[:12]
    sample_out = outs[12:12 + N_SAMPLE_OUT]
    cast_out = outs[12 + N_SAMPLE_OUT:12 + N_SAMPLE_OUT + n_cast]
    q_rows, q_by4, k_rows, k_by4, v_rows, v_by4, state, w_ref = outs[12 + N_SAMPLE_OUT + n_cast:]

    @pl.when(pl.program_id(0) == 0)
    def _():
        state[...] = jnp.zeros_like(state)
        for j in range(N_SPLITS):
            cols = slice(j * W_ATT, (j + 1) * W_ATT)
            w_ref[:, cols] = w_f32_ref[:, cols].astype(BF16)

    for src, dst in zip(cast_in, cast_out):
        dst[...] = src[...].astype(BF16)

    xb = x_ref[...].astype(BF16)
    tm = xb.shape[0]
    half = tm // 2

    step = pl.ds(pl.program_id(0), 1)
    ca, sa = cos_step_ref[step, :], sin_step_ref[step, :]
    cb, sb = cos_row_ref[...], sin_row_ref[...]
    sign = jnp.where(lax.broadcasted_iota(jnp.int32, (1, LANES), 1) < LANES // 2, -1.0, 1.0)
    cos, sin = ca * cb - sa * sb, (sa * cb + ca * sb) * sign

    def retention_proj(j, rows):
        u = _proj(xb[rows], w_ref, 3 + j)
        if j < 2:
            u = _rotary(u, cos[rows], sin[rows])
        return u * DK_RET ** -0.5 if j == 1 else u

    attn_out = ((q1_ref, q4_ref, q16_ref, q_rows, q_by4), (k1_ref, k4_ref, k16_ref, k_rows, k_by4),
                (v1_ref, v4_ref, v16_ref, v_rows, v_by4))

    def attention_proj(j, rows, last):
        nat_ref, by4_ref, by16_ref, rows_scr, by4_scr = attn_out[j]
        val = _proj(xb[rows], w_ref, j)
        _store_rows(val * (HD_ATT ** -0.5 * LOG2E) if j == 0 else val, rows, nat_ref, rows_scr)
        if last:
            _store_gathered(by4_ref, by16_ref, rows_scr, by4_scr)

    halves = (slice(0, half), slice(half, tm))
    ret_in = [[retention_proj(j, halves[0]) for j in range(4)], [None] * 4]
    fills = [[("ret", 0), ("ret", 1), ("ret", 2), ("ret", 3), ("att", 0, 0)],
             [("att", 0, 1), ("att", 1, 0), ("att", 1, 1), ("att", 2, 0), ("att", 2, 1)]]
    decays = [_retention_decays(h) for h in range(H_RET)]
    S = [state[h] for h in range(H_RET)]

    def emit(fill):
        if fill[0] == "ret":
            ret_in[1][fill[1]] = retention_proj(fill[1], halves[1])
        else:
            attention_proj(fill[1], halves[fill[2]], last=fill[2] == 1)

    for ph in range(2):
        qr, kr, vr, g = ret_in[ph]
        pending = list(fills[ph])
        for c in range(half // RET_CHUNK):
            rows = slice(c * RET_CHUNK, (c + 1) * RET_CHUNK)
            out_rows = slice(ph * half + c * RET_CHUNK, ph * half + (c + 1) * RET_CHUNK)
            staged = []
            for h in range(H_RET):
                cols = slice(h * DK_RET, (h + 1) * DK_RET)
                scores, cross, vb, S[h] = _retention_chunk_matmuls(qr[rows, cols], kr[rows, cols], vr[rows, cols],
                                                                  S[h], decays[h])
                staged.append((scores, cross, vb))
            emit(pending.pop(0))
            for h in range(H_RET):
                cols = slice(h * DK_RET, (h + 1) * DK_RET)
                yret_ref[out_rows, cols] = _retention_chunk_output(*staged[h], g[rows, cols]).astype(BF16)
            emit(pending.pop(0))
        while pending:
            emit(pending.pop(0))
    for h in range(H_RET):
        state[h] = S[h]
        st_ref[h] = S[h]

    @pl.when(pl.program_id(0) >= win_first_step)
    def _():
        for p in range(PAIRS):
            kvw_ref[p * LANES:(p + 1) * LANES, :] = k_rows[p].T
            kvw_ref[W_ATT + p * LANES:W_ATT + (p + 1) * LANES, :] = v_rows[p].T

    @pl.when(pl.program_id(0) == n_steps - 1)
    def _():
        _inproj_sample(*sample_in, w_ref, *sample_out)


def _inproj_sample(x_ref, cos_ref, sin_ref, w_ref, qa_ref, kvw_ref, qr_ref, kr_ref, vr_ref, g_ref):
    xb = x_ref[...].astype(BF16)
    qa_ref[...] = _proj(xb, w_ref, 0) * HD_ATT ** -0.5
    kvw_ref[:, :W_ATT] = _proj(xb, w_ref, 1)
    kvw_ref[:, W_ATT:] = _proj(xb, w_ref, 2)
    qr, kr, vr, g = _retention_qkvg(xb, w_ref, cos_ref[...], sin_ref[...])
    qr_ref[...] = qr
    kr_ref[...] = kr
    vr_ref[...] = vr
    g_ref[...] = g


def _rope_tables(pos):
    half = DK_RET // 2
    inv_freq = 1.0 / (ROPE_BASE ** jnp.linspace(0.0, 1.0, half, dtype=F32))
    ang = pos.astype(F32)[:, None] * inv_freq[None, :]
    cos, sin = jnp.cos(ang), jnp.sin(ang)
    return jnp.concatenate([cos, cos], axis=1), jnp.concatenate([sin, sin], axis=1)


def _inproj_prompt(x, x_sample, w_in, later_weights, tm):
    S = x.shape[0]
    B = x_sample.shape[0]
    win = min(MAX_WINDOW, S)
    steps = S // tm
    win_first_step = steps - win // tm
    cos, sin = _rope_tables(jnp.concatenate([jnp.arange(tm, dtype=jnp.int32),
                                             jnp.arange(steps, dtype=jnp.int32) * tm]))
    const = lambda i: (0, 0)
    row = lambda i: (i, 0)
    view_specs = [pl.BlockSpec((PAIRS, tm // d, d * LANES), lambda i: (0, i, 0)) for d in (1, 4, 16)] * 3
    view_shapes = [jax.ShapeDtypeStruct((PAIRS, S // d, d * LANES), BF16) for d in (1, 4, 16)] * 3
    state_shape = (H_RET, DK_RET, DV_RET)
    gather_scratch = [pltpu.VMEM((PAIRS, tm, LANES), F32), pltpu.VMEM((4 * PAIRS, tm // 4, LANES), F32)]
    slab_specs = [pl.BlockSpec((w.shape[0] // steps, w.shape[1]), row) for w in later_weights]
    cos_s, sin_s = _rope_tables(jnp.full((B,), PAST_LEN, dtype=jnp.int32))
    sin_s = sin_s * jnp.where(jnp.arange(LANES) < LANES // 2, -1.0, 1.0)
    whole = lambda width: pl.BlockSpec((B, width), const)
    sample_widths = (W_ATT, 2 * W_ATT, W_RET, W_RET, W_RET, W_RET)
    return pl.pallas_call(
        functools.partial(_inproj_prompt_body, win_first_step=win_first_step, n_steps=steps,
                          n_cast=len(later_weights)),
        grid=(steps,),
        in_specs=[pl.BlockSpec((tm, D_MODEL), row),
                  pl.BlockSpec((D_MODEL, N_SPLITS * W_ATT), lambda i: (0, 0)),
                  pl.BlockSpec((tm, LANES), const), pl.BlockSpec((tm, LANES), const),
                  pl.BlockSpec((steps, LANES), const), pl.BlockSpec((steps, LANES), const),
                  whole(D_MODEL), whole(LANES), whole(LANES)] + slab_specs,
        out_specs=view_specs + [
            pl.BlockSpec((2 * W_ATT, tm), lambda i: (0, jnp.maximum(i - win_first_step, 0))),
            pl.BlockSpec((tm, W_RET), row),
            pl.BlockSpec(state_shape, lambda i: (0, 0, 0))] + [whole(n) for n in sample_widths] + slab_specs,
        out_shape=view_shapes + [
            jax.ShapeDtypeStruct((2 * W_ATT, win), F32),
            jax.ShapeDtypeStruct((S, W_RET), BF16),
            jax.ShapeDtypeStruct(state_shape, F32)] + [jax.ShapeDtypeStruct((B, n), F32) for n in sample_widths]
        + [jax.ShapeDtypeStruct(w.shape, BF16) for w in later_weights],
        scratch_shapes=gather_scratch * 3 + [pltpu.VMEM(state_shape, F32),
                                             pltpu.VMEM((D_MODEL, N_SPLITS * W_ATT), BF16)],
        compiler_params=_cparams(1),
        name="inproj_retention_prompt",
    )(x, w_in, cos[:tm], sin[:tm], cos[tm:], sin[tm:], x_sample, cos_s, sin_s, *later_weights)


def _t5_bucket(dist):
    is_small = dist < MAX_EXACT
    d_f = jnp.maximum(dist, 1).astype(F32)
    large = MAX_EXACT + (jnp.log(d_f / MAX_EXACT) / math.log(MAX_WINDOW / MAX_EXACT)
                         * (N_BUCKETS - MAX_EXACT)).astype(jnp.int32)
    large = jnp.minimum(large, N_BUCKETS - 1)
    return jnp.where(is_small, dist, large)


def _tap_bias(rel_bias):
    taps = jnp.arange(TAPS + 1)
    buckets = jnp.stack([_t5_bucket(taps * d) for _, d in DILATIONS])
    return jnp.transpose(rel_bias[buckets].astype(F32), (0, 2, 1))


def _band_bias_rows(tap_bias):
    rows = jnp.concatenate([tap_bias[..., ::-1] * LOG2E,
                            jnp.full(tap_bias.shape[:-1] + (2 * Q_ROWS - TAPS - 1,), NEG, F32)], axis=-1)
    return rows.reshape(len(DILATIONS), PAIRS, 2, 2 * Q_ROWS)


def _window_bias(tap_bias, window):
    rows = []
    for i, (_, d) in enumerate(DILATIONS):
        taps_rev = tap_bias[i, :, TAPS:0:-1]
        seg = jnp.concatenate([taps_rev[..., None], jnp.full((H_ATT, TAPS, d - 1), NEG, F32)], axis=-1)
        rows.append(jnp.concatenate([jnp.full((H_ATT, window - TAPS * d), NEG, F32),
                                     seg.reshape(H_ATT, TAPS * d)], axis=-1))
    return jnp.stack(rows)


def _attn_probs(q, k, bias2, head0_b):
    qs = jnp.concatenate([q * head0_b, q * (1 - head0_b)], axis=0)
    s = lax.dot_general(qs, k, (((1,), (1,)), ((), ())), preferred_element_type=F32) + bias2
    m = jnp.max(s, axis=1, keepdims=True)
    p = jnp.exp2(s - m)
    return p.astype(BF16), jnp.sum(p, axis=1, keepdims=True), m


def _attn_output(p, l, m, v, head0):
    o = jnp.dot(p, v, preferred_element_type=F32) / l
    lse = m + jnp.log2(l)
    o_pair = jnp.where(head0, o[:Q_ROWS], o[Q_ROWS:])
    lse_pair = jnp.where(head0, jnp.broadcast_to(lse[:Q_ROWS], (Q_ROWS, LANES)),
                         jnp.broadcast_to(lse[Q_ROWS:], (Q_ROWS, LANES)))
    return o_pair, lse_pair


def _attn_body(q1, q4, q16, k1c, k1p, k4c, k4p, k16c, k16p, v1c, v1p, v4c, v4p, v16c, v16p, bias_ref,
               out_ref, o16, l16, o4, l4, band):
    first = pl.program_id(1) == 0
    lane = lax.broadcasted_iota(jnp.int32, (Q_ROWS, LANES), 1)
    head0 = lane < HD_ATT
    head0_b = jnp.where(head0, 1.0, 0.0).astype(BF16)
    col = lax.broadcasted_iota(jnp.int32, (2 * Q_ROWS, 2 * Q_ROWS), 1)
    no_prev = jnp.where(jnp.logical_and(first, col < Q_ROWS), NEG, 0.0)

    @pl.when(first)
    def _():
        for branch in range(len(DILATIONS)):
            for hh in range(2):
                row = jnp.broadcast_to(bias_ref[branch, hh:hh + 1, :], (Q_ROWS, 2 * Q_ROWS))
                band[branch, hh * Q_ROWS:(hh + 1) * Q_ROWS, :] = pltpu.roll(row, 0, 1, stride=1, stride_axis=0)

    def bias2(branch, masked_prev):
        return band[branch] + no_prev if masked_prev else band[branch]

    def cat(a, b):
        return jnp.concatenate([a, b], axis=0)

    def window(cur, prev, b, sl):
        if b == 0:
            return cat(prev[:, sl], cur[0:Q_ROWS, sl])
        return cur[(b - 1) * Q_ROWS:(b + 1) * Q_ROWS, sl]

    units = []
    b16 = bias2(2, True)
    for r in range(16):
        sl = slice(r * LANES, (r + 1) * LANES)

        def store16(o, l, r=r):
            o16[pl.ds(r, Q_ROWS, stride=16), :] = o
            l16[pl.ds(r, Q_ROWS, stride=16), :] = l

        units.append((lambda sl=sl: q16[:, sl], lambda sl=sl: window(k16c, k16p, 0, sl),
                      lambda sl=sl: window(v16c, v16p, 0, sl), b16, store16))

    b4 = (bias2(1, True), bias2(1, False))
    for r in range(4):
        sl = slice(r * LANES, (r + 1) * LANES)
        for b in range(4):
            rows = slice(b * Q_ROWS, (b + 1) * Q_ROWS)

            def store4(o, l, r=r, b=b):
                o4[pl.ds(b * 4 * Q_ROWS + r, Q_ROWS, stride=4), :] = o
                l4[pl.ds(b * 4 * Q_ROWS + r, Q_ROWS, stride=4), :] = l

            units.append((lambda rows=rows, sl=sl: q4[rows, sl], lambda b=b, sl=sl: window(k4c, k4p, b, sl),
                          lambda b=b, sl=sl: window(v4c, v4p, b, sl), b4[min(b, 1)], store4))

    b1 = (bias2(0, True), bias2(0, False))
    everything = slice(None)
    for b in range(ATT_BLOCK // Q_ROWS):
        rows = slice(b * Q_ROWS, (b + 1) * Q_ROWS)

        def mix(oa, la, rows=rows):
            ob, lb = o4[rows, :], l4[rows, :]
            oc, lc = o16[rows, :], l16[rows, :]
            top = jnp.maximum(jnp.maximum(la, lb), lc)
            ea, eb, ec = jnp.exp2(la - top), jnp.exp2(lb - top), jnp.exp2(lc - top)
            out_ref[rows, :] = ((ea * oa + eb * ob + ec * oc) / (ea + eb + ec)).astype(BF16)

        units.append((lambda rows=rows: q1[rows, :], lambda b=b: window(k1c, k1p, b, everything),
                      lambda b=b: window(v1c, v1p, b, everything), b1[min(b, 1)], mix))

    pending = None
    for q, k, v, bias, finish in units:
        probs = _attn_probs(q(), k(), bias, head0_b)
        if pending is not None:
            p_probs, p_v, p_finish = pending
            p_finish(*_attn_output(*p_probs, p_v(), head0))
        pending = (probs, v, finish)
    p_probs, p_v, p_finish = pending
    p_finish(*_attn_output(*p_probs, p_v(), head0))


def _attn_prompt(q1, q4, q16, k1, k4, k16, v1, v4, v16, band_bias):
    S = q1.shape[1]
    assert S % ATT_BLOCK == 0
    steps = S // ATT_BLOCK

    def cur(d):
        return pl.BlockSpec((None, ATT_BLOCK // d, d * LANES), lambda p, c: (p, c, 0))

    def prev(d):
        per_step = ATT_BLOCK // d // Q_ROWS
        return pl.BlockSpec((None, Q_ROWS, d * LANES), lambda p, c: (p, jnp.maximum(c * per_step - 1, 0), 0))

    scratch = pltpu.VMEM((ATT_BLOCK, LANES), F32)
    return pl.pallas_call(
        _attn_body,
        grid=(PAIRS, steps),
        in_specs=[cur(1), cur(4), cur(16),
                  cur(1), prev(1), cur(4), prev(4), cur(16), prev(16),
                  cur(1), prev(1), cur(4), prev(4), cur(16), prev(16),
                  pl.BlockSpec((len(DILATIONS), None, 2, 2 * Q_ROWS), lambda p, c: (0, p, 0, 0))],
        out_specs=pl.BlockSpec((ATT_BLOCK, LANES), lambda p, c: (c, p)),
        out_shape=jax.ShapeDtypeStruct((S, W_ATT), BF16),
        scratch_shapes=[scratch, scratch, scratch, scratch,
                        pltpu.VMEM((len(DILATIONS), 2 * Q_ROWS, 2 * Q_ROWS), F32)],
        compiler_params=_cparams(2),
        name="attn_prompt",
    )(q1, q4, q16, k1, k1, k4, k4, k16, k16, v1, v1, v4, v4, v16, v16, band_bias)


def _group_norm_gate(o, g):
    mu = jnp.mean(o, axis=1, keepdims=True)
    var = jnp.mean(jnp.square(o - mu), axis=1, keepdims=True)
    return g * jax.nn.sigmoid(g) * ((o - mu) * lax.rsqrt(var + GN_EPS))


def _retention_decays(h):
    C = RET_CHUNK
    lg = LOG_GAMMA[h]
    n = lax.broadcasted_iota(jnp.int32, (C, 1), 0).astype(F32)
    diff = (lax.broadcasted_iota(jnp.int32, (C, C), 0) - lax.broadcasted_iota(jnp.int32, (C, C), 1)).astype(F32)
    return (jnp.where(diff >= 0, jnp.exp(lg * jnp.maximum(diff, 0.0)), 0.0),
            jnp.exp(lg * (n + 1.0)), jnp.exp(lg * (C - 1.0 - n)), math.exp(lg * C))


def _retention_chunk_matmuls(q, k, v, S, decays):
    decay, q_decay, k_decay, chunk_decay = decays
    qb, vb = q.astype(BF16), v.astype(BF16)
    scores = lax.dot_general(qb, k.astype(BF16), (((1,), (1,)), ((), ())), preferred_element_type=F32) * decay
    cross = jnp.dot(qb, S.astype(BF16), preferred_element_type=F32) * q_decay
    k_dec_t = (k * k_decay).T.astype(BF16)
    return scores.astype(BF16), cross, vb, chunk_decay * S + jnp.dot(k_dec_t, vb, preferred_element_type=F32)


def _retention_chunk_output(scores, cross, vb, g):
    return _group_norm_gate(jnp.dot(scores, vb, preferred_element_type=F32) + cross, g)


def _as_column(row_pair):
    return jnp.broadcast_to(row_pair, (LANES, LANES)).T


def _sample_mixer_body(qa_ref, kvw_ref, qr_ref, kr_ref, vr_ref, g_ref, cache_ref, st_ref, bias_ref, bias0_ref,
                       oatt_ref, yret_ref, stout_ref):
    W = cache_ref.shape[-1]
    row = pl.ds(pl.program_id(0), 1)
    qa, kvw = qa_ref[row, :], kvw_ref[row, :]
    logit_rows, self_rows = [], []
    for p in range(PAIRS):
        lanes = slice(p * LANES, (p + 1) * LANES)
        q_pair = qa[:, lanes]
        k_pair = kvw[:, lanes]
        q_col = _as_column(q_pair)
        for hh in range(2):
            feat = slice(hh * HD_ATT, (hh + 1) * HD_ATT)
            logit_rows.append(jnp.sum(cache_ref[0, 2 * p + hh] * q_col[feat, 0:1], axis=0, keepdims=True))
            self_rows.append(jnp.sum(q_pair[:, feat] * k_pair[:, feat], axis=1, keepdims=True))
    logits = jnp.concatenate(logit_rows, axis=0)
    s_self = jnp.concatenate(self_rows, axis=0) + bias0_ref[:, 0:1]

    probs, p_selfs, denoms, lses = [], [], [], []
    for i, (_, d) in enumerate(DILATIONS):
        lo = W - TAPS * d
        s = logits[:, lo:] + bias_ref[i, :, lo:]
        m = jnp.maximum(jnp.max(s, axis=1, keepdims=True), s_self)
        pr = jnp.exp(s - m)
        p_self = jnp.exp(s_self - m)
        l = jnp.sum(pr, axis=1, keepdims=True) + p_self
        probs.append(pr)
        p_selfs.append(p_self)
        denoms.append(l)
        lses.append(m + jnp.log(l))
    top = jnp.maximum(jnp.maximum(lses[0], lses[1]), lses[2])
    e = [jnp.exp(x - top) for x in lses]
    e_sum = e[0] + e[1] + e[2]
    coef = [e[i] / (e_sum * denoms[i]) for i in range(3)]
    lo4, lo1 = W - TAPS * 4, W - TAPS
    p16, p4, p1 = coef[2] * probs[2], coef[1] * probs[1], coef[0] * probs[0]
    p_all = jnp.concatenate([p16[:, :lo4], p16[:, lo4:lo1] + p4[:, :lo1 - lo4],
                             p16[:, lo1:] + p4[:, lo1 - lo4:] + p1], axis=1)
    c_self = coef[0] * p_selfs[0] + coef[1] * p_selfs[1] + coef[2] * p_selfs[2]
    out_pairs = []
    for p in range(PAIRS):
        lanes = slice(p * LANES, (p + 1) * LANES)
        v_col = _as_column(kvw[:, W_ATT + p * LANES:W_ATT + (p + 1) * LANES])
        out_cols = []
        for hh in range(2):
            h = 2 * p + hh
            feat = slice(hh * HD_ATT, (hh + 1) * HD_ATT)
            pv = jnp.sum(cache_ref[1, h] * p_all[h:h + 1, :], axis=1, keepdims=True)
            out_cols.append(pv + c_self[h:h + 1, :] * v_col[feat, 0:1])
        col = jnp.broadcast_to(jnp.concatenate(out_cols, axis=0), (LANES, LANES))
        out_pairs.append(col.T[0:1, :])
    oatt_ref[row, :] = jnp.concatenate(out_pairs, axis=1)

    qr, kr, vr, g = qr_ref[row, :], kr_ref[row, :], vr_ref[row, :], g_ref[row, :]
    y_heads = []
    for h in range(H_RET):
        cols = slice(h * DK_RET, (h + 1) * DK_RET)
        gamma = math.exp(LOG_GAMMA[h])
        qh, kh, vh = qr[:, cols], kr[:, cols], vr[:, cols]
        S = st_ref[h]
        qk = jnp.sum(qh * kh, axis=1, keepdims=True)
        cross = jnp.dot(jnp.broadcast_to(qh, (16, DK_RET)).astype(BF16), S.astype(BF16),
                        preferred_element_type=F32)[:1]
        o = qk * vh + cross * gamma
        stout_ref[h] = gamma * S + _as_column(kh) * vh
        y_heads.append(_group_norm_gate(o, g[:, cols]))
    yret_ref[row, :] = jnp.concatenate(y_heads, axis=1)


def _mixer_operands(qa, kvw, qr, kr, vr, g, cache_t, state, tap_bias):
    B, _, _, _, W = cache_t.shape
    assert W == MAX_WINDOW and PAST_LEN >= MAX_WINDOW
    vec = lambda width: pl.BlockSpec((B, width), lambda b: (0, 0))
    bias0 = jnp.broadcast_to(tap_bias[0, :, 0:1], (H_ATT, LANES))
    token = lambda b: jnp.minimum(b, B - 1)
    st_spec = pl.BlockSpec((None, H_RET, DK_RET, DV_RET), lambda b: (token(b), 0, 0, 0))
    in_specs = [vec(W_ATT), vec(2 * W_ATT), vec(W_RET), vec(W_RET), vec(W_RET), vec(W_RET),
                pl.BlockSpec((None, 2, H_ATT, HD_ATT, W), lambda b: (token(b), 0, 0, 0, 0)),
                st_spec,
                pl.BlockSpec((len(DILATIONS), H_ATT, W), lambda b: (0, 0, 0)),
                pl.BlockSpec((H_ATT, LANES), lambda b: (0, 0))]
    args = (qa, kvw, qr, kr, vr, g, cache_t, state, _window_bias(tap_bias, W), bias0)
    out_specs = [vec(W_ATT), vec(W_RET), st_spec]
    out_shape = [jax.ShapeDtypeStruct((B, W_ATT), F32), jax.ShapeDtypeStruct((B, W_RET), F32),
                 jax.ShapeDtypeStruct(state.shape, F32)]
    return in_specs, args, out_specs, out_shape


FF_CHUNK = 1024
FFN_MIN_ROWS = 128


def _layernorm(z, g, b):
    mu = jnp.mean(z, axis=1, keepdims=True)
    var = jnp.mean(jnp.square(z - mu), axis=1, keepdims=True)
    return (z - mu) * lax.rsqrt(var + LN_EPS) * g + b


def _out_ffn_body(x_ref, oatt_ref, yret_ref, wo_ref, g1_ref, b1_ref, wu_ref, wd_ref, g2_ref, b2_ref, y_ref):
    tm = x_ref.shape[0]
    n_parts = 2 if tm >= 2 * FFN_MIN_ROWS else 1
    parts = [slice(i * (tm // n_parts), (i + 1) * (tm // n_parts)) for i in range(n_parts)]
    mixes = [jnp.dot(oatt_ref[r, :].astype(BF16), wo_ref[:W_ATT, :], preferred_element_type=F32)
             + jnp.dot(yret_ref[r, :].astype(BF16), wo_ref[W_ATT:, :], preferred_element_type=F32) for r in parts]
    n_chunks = D_FF // FF_CHUNK
    for r, mix in zip(parts, mixes):
        x1 = _layernorm(ALPHA * x_ref[r, :] + mix, g1_ref[...], b1_ref[...])
        x1b = x1.astype(BF16)

        def up(j):
            cols = slice(j * FF_CHUNK, (j + 1) * FF_CHUNK)
            return jnp.square(jnp.maximum(jnp.dot(x1b, wu_ref[:, cols], preferred_element_type=F32),
                                          0.0)).astype(BF16)

        ffn = jnp.zeros_like(x1)
        h = up(0)
        for j in range(n_chunks):
            h_next = up(j + 1) if j + 1 < n_chunks else None
            ffn = ffn + jnp.dot(h, wd_ref[j * FF_CHUNK:(j + 1) * FF_CHUNK, :], preferred_element_type=F32)
            h = h_next
        y_ref[r, :] = _layernorm(ALPHA * x1 + ffn, g2_ref[...], b2_ref[...])


def _ffn_operands(x, oatt, yret, wo_b, g1, b1, wu_b, wd_b, g2, b2, tm):
    row = lambda i: (jnp.minimum(i, x.shape[0] // tm - 1), 0)
    const = lambda shape: pl.BlockSpec(shape, lambda i: (0, 0), pipeline_mode=pl.Buffered(1))
    in_specs = [pl.BlockSpec((tm, D_MODEL), row), pl.BlockSpec((tm, W_ATT), row), pl.BlockSpec((tm, W_RET), row),
                const((W_ATT + W_RET, D_MODEL)), const((1, D_MODEL)), const((1, D_MODEL)),
                const((D_MODEL, D_FF)), const((D_FF, D_MODEL)), const((1, D_MODEL)), const((1, D_MODEL))]
    args = (x, oatt, yret, wo_b, g1, b1, wu_b, wd_b, g2, b2)
    return in_specs, args, pl.BlockSpec((tm, D_MODEL), row), jax.ShapeDtypeStruct((x.shape[0], D_MODEL), F32)


N_FFN_ACTIVATIONS = 3


def _ffn_mixer_tail_body(*refs, n_ffn_in, n_mix_in, n_tokens):
    ffn_in, mix_in = refs[:n_ffn_in], refs[n_ffn_in:n_ffn_in + n_mix_in]
    xs_ref = refs[n_ffn_in + n_mix_in]
    y_ref, oatt_s_ref, yret_s_ref, st_ref, ys_ref = refs[n_ffn_in + n_mix_in + 1:]
    step = pl.program_id(0)

    @pl.when(step < n_tokens)
    def _():
        _sample_mixer_body(*mix_in, oatt_s_ref, yret_s_ref, st_ref)
        _out_ffn_body(*ffn_in, y_ref)

    @pl.when(step == n_tokens)
    def _():
        _out_ffn_body(xs_ref, oatt_s_ref, yret_s_ref, *ffn_in[N_FFN_ACTIVATIONS:], ys_ref)


def _out_ffn_prompt_and_sample(ffn_args, mixer_args, x_sample):
    B = mixer_args[0].shape[0]
    tm = ffn_args[0].shape[0] // B
    f_in, f_args, f_out, f_shape = _ffn_operands(*ffn_args, tm)
    m_in, m_args, m_out, m_shape = _mixer_operands(*mixer_args)
    whole = pl.BlockSpec((B, D_MODEL), lambda i: (0, 0))
    y, oatt, yret, st, y_s = pl.pallas_call(
        functools.partial(_ffn_mixer_tail_body, n_ffn_in=len(f_in), n_mix_in=len(m_in), n_tokens=B),
        grid=(B + 1,), in_specs=f_in + m_in + [whole], out_specs=[f_out] + m_out + [whole],
        out_shape=[f_shape] + m_shape + [jax.ShapeDtypeStruct((B, D_MODEL), F32)],
        compiler_params=_cparams(1), name="out_ffn_prompt_sample_mixer",
    )(*f_args, *m_args, x_sample)
    return y, st, y_s


PROMPT_ROWS = 512


def kernel(x_prompt, x_sample, cache_kv_win, state_ret, w_in, rel_bias, w_out,
           ln1_g, ln1_b, w_up, w_down, ln2_g, ln2_b):
    assert x_prompt.shape[0] == 1 and x_sample.shape[1] == 1 and w_in.shape[0] == DEPTH
    B = x_sample.shape[0]
    g1, b1, g2, b2 = ln1_g[0][None], ln1_b[0][None], ln2_g[0][None], ln2_b[0][None]
    tap_bias = _tap_bias(rel_bias)

    xp = x_prompt[0]
    xs = x_sample[:, 0]
    *qkv_views, kvw_p, yret, st_p, qa_s, kvw_s, qr_s, kr_s, vr_s, g_s, wo_b, wu_b, wd_b = _inproj_prompt(
        xp, xs, w_in[0], (w_out[0], w_up[0], w_down[0]), PROMPT_ROWS)
    oatt = _attn_prompt(*qkv_views, _band_bias_rows(tap_bias))

    cache_t = jnp.transpose(cache_kv_win[0], (0, 2, 3, 4, 1))
    ffn_weights = (wo_b, g1, b1, wu_b, wd_b, g2, b2)
    y_p, st_s, y_s = _out_ffn_prompt_and_sample(
        (xp, oatt, yret) + ffn_weights, (qa_s, kvw_s, qr_s, kr_s, vr_s, g_s, cache_t, state_ret[0], tap_bias), xs)

    win = kvw_p.shape[1]
    kv_win_p = jnp.transpose(kvw_p.reshape(2, H_ATT, HD_ATT, win), (3, 0, 1, 2))
    return (y_p[None], y_s[:, None],
            kv_win_p[None, None], kvw_s.reshape(1, B, 1, 2, H_ATT, HD_ATT),
            st_p[None, None], st_s[None])
```

```python
import functools
import math

import jax
import jax.numpy as jnp
from jax import lax
from jax.experimental import pallas as pl
from jax.experimental.pallas import tpu as pltpu

F32 = jnp.float32
BF16 = jnp.bfloat16

D_MODEL = 1024
DEPTH = 1
PAST_LEN = 16384
W_ATT = 512
HD_ATT = 64
H_ATT = 8
DILATIONS = ((128, 1), (512, 4), (2048, 16))
TAPS = 128
MAX_WINDOW = 2048
N_BUCKETS = 32
MAX_EXACT = N_BUCKETS // 2
W_RET = 512
H_RET = 4
DK_RET = 128
DV_RET = 128
RET_CHUNK = 128
ROPE_BASE = 10000.0
D_FF = 4096
N_SPLITS = 7
ALPHA = (2.0 * DEPTH) ** 0.25
LN_EPS = 1e-5
GN_EPS = 1e-6
LOG_GAMMA = tuple(math.log(1.0 - 2.0 ** (-5.0 - h)) for h in range(H_RET))

LANES = 128
PAIRS = W_ATT // LANES
NEG = -1e30
LOG2E = math.log2(math.e)

ATT_BLOCK = 2048
Q_ROWS = 128
VMEM_LIMIT = 56 * 1024 * 1024


def _cparams(n_axes):
    return pltpu.CompilerParams(dimension_semantics=("arbitrary",) * n_axes, vmem_limit_bytes=VMEM_LIMIT)


def _rotary(u, cos, sin_signed):
    outs = []
    for h in range(H_RET):
        xh = u[:, h * LANES:(h + 1) * LANES]
        outs.append(xh * cos + pltpu.roll(xh, LANES // 2, 1) * sin_signed)
    return jnp.concatenate(outs, axis=1)


def _proj(xb, w_ref, j):
    return jnp.dot(xb, w_ref[:, j * W_ATT:(j + 1) * W_ATT], preferred_element_type=F32)


def _store_rows(val, rows, nat_ref, rows_scr):
    for p in range(PAIRS):
        x = val[:, p * LANES:(p + 1) * LANES]
        nat_ref[p, rows, :] = x.astype(BF16)
        rows_scr[p, rows, :] = x


def _store_gathered(by4_ref, by16_ref, rows_scr, by4_scr):
    tm = rows_scr.shape[1]
    for p in range(PAIRS):
        for r4 in range(4):
            a = rows_scr[p, pl.ds(r4, tm // 4, stride=4), :]
            by4_ref[p, :, r4 * LANES:(r4 + 1) * LANES] = a.astype(BF16)
            by4_scr[p * 4 + r4] = a
    for p in range(PAIRS):
        for r4 in range(4):
            for rr in range(4):
                r16 = r4 + 4 * rr
                b = by4_scr[p * 4 + r4, pl.ds(rr, tm // 16, stride=4), :]
                by16_ref[p, :, r16 * LANES:(r16 + 1) * LANES] = b.astype(BF16)


def _retention_qkvg(xb, w_ref, cos, sin):
    return (_rotary(_proj(xb, w_ref, 3), cos, sin), _rotary(_proj(xb, w_ref, 4), cos, sin) * DK_RET ** -0.5,
            _proj(xb, w_ref, 5), _proj(xb, w_ref, 6))


N_SAMPLE_IN = 3
N_SAMPLE_OUT = 6


def _inproj_prompt_body(*refs, win_first_step, n_steps, n_cast):
    x_ref, w_f32_ref, cos_row_ref, sin_row_ref, cos_step_ref, sin_step_ref = refs[:6]
    sample_in = refs[6:6 + N_SAMPLE_IN]
    cast_in, outs = refs[6 + N_SAMPLE_IN:6 + N_SAMPLE_IN + n_cast], refs[6 + N_SAMPLE_IN + n_cast:]
    q1_ref, q4_ref, q16_ref, k1_ref, k4_ref, k16_ref, v1_ref, v4_ref, v16_ref, kvw_ref, yret_ref, st_ref = outs[:12]
    sample_out = outs[12:12 + N_SAMPLE_OUT]
    cast_out = outs[12 + N_SAMPLE_OUT:12 + N_SAMPLE_OUT + n_cast]
    q_rows, q_by4, k_rows, k_by4, v_rows, v_by4, state, w_ref = outs[12 + N_SAMPLE_OUT + n_cast:]

    @pl.when(pl.program_id(0) == 0)
    def _():
        state[...] = jnp.zeros_like(state)
        for j in range(N_SPLITS):
            cols = slice(j * W_ATT, (j + 1) * W_ATT)
            w_ref[:, cols] = w_f32_ref[:, cols].astype(BF16)

    for src, dst in zip(cast_in, cast_out):
        dst[...] = src[...].astype(BF16)

    xb = x_ref[...].astype(BF16)
    tm = xb.shape[0]
    half = tm // 2

    step = pl.ds(pl.program_id(0), 1)
    ca, sa = cos_step_ref[step, :], sin_step_ref[step, :]
    cb, sb = cos_row_ref[...], sin_row_ref[...]
    sign = jnp.where(lax.broadcasted_iota(jnp.int32, (1, LANES), 1) < LANES // 2, -1.0, 1.0)
    cos, sin = ca * cb - sa * sb, (sa * cb + ca * sb) * sign

    def retention_proj(j, rows):
        u = _proj(xb[rows], w_ref, 3 + j)
        if j < 2:
            u = _rotary(u, cos[rows], sin[rows])
        return u * DK_RET ** -0.5 if j == 1 else u

    attn_out = ((q1_ref, q4_ref, q16_ref, q_rows, q_by4), (k1_ref, k4_ref, k16_ref, k_rows, k_by4),
                (v1_ref, v4_ref, v16_ref, v_rows, v_by4))

    def attention_proj(j, rows, last):
        nat_ref, by4_ref, by16_ref, rows_scr, by4_scr = attn_out[j]
        val = _proj(xb[rows], w_ref, j)
        _store_rows(val * (HD_ATT ** -0.5 * LOG2E) if j == 0 else val, rows, nat_ref, rows_scr)
        if last:
            _store_gathered(by4_ref, by16_ref, rows_scr, by4_scr)

    halves = (slice(0, half), slice(half, tm))
    ret_in = [[retention_proj(j, halves[0]) for j in range(4)], [None] * 4]
    fills = [[("ret", 0), ("ret", 1), ("ret", 2), ("ret", 3)],
             [("att", 0), ("att", 1), ("att", 2)]]
    decays = [_retention_decays(h) for h in range(H_RET)]
    S = [state[h] for h in range(H_RET)]

    def emit(fill):
        if fill[0] == "ret":
            ret_in[1][fill[1]] = retention_proj(fill[1], halves[1])
        else:
            attention_proj(fill[1], slice(0, tm), last=True)

    for ph in range(2):
        qr, kr, vr, g = ret_in[ph]
        pending = list(fills[ph])
        for c in range(half // RET_CHUNK):
            rows = slice(c * RET_CHUNK, (c + 1) * RET_CHUNK)
            out_rows = slice(ph * half + c * RET_CHUNK, ph * half + (c + 1) * RET_CHUNK)
            staged = []
            for h in range(H_RET):
                cols = slice(h * DK_RET, (h + 1) * DK_RET)
                scores, cross, vb, S[h] = _retention_chunk_matmuls(qr[rows, cols], kr[rows, cols], vr[rows, cols],
                                                                  S[h], decays[h])
                staged.append((scores, cross, vb))
            if pending:
                emit(pending.pop(0))
            for h in range(H_RET):
                cols = slice(h * DK_RET, (h + 1) * DK_RET)
                yret_ref[out_rows, cols] = _retention_chunk_output(*staged[h], g[rows, cols]).astype(BF16)
            if pending:
                emit(pending.pop(0))
        while pending:
            emit(pending.pop(0))
    for h in range(H_RET):
        state[h] = S[h]
        st_ref[h] = S[h]

    @pl.when(pl.program_id(0) >= win_first_step)
    def _():
        for p in range(PAIRS):
            kvw_ref[p * LANES:(p + 1) * LANES, :] = k_rows[p].T
            kvw_ref[W_ATT + p * LANES:W_ATT + (p + 1) * LANES, :] = v_rows[p].T

    @pl.when(pl.program_id(0) == n_steps - 1)
    def _():
        _inproj_sample(*sample_in, w_ref, *sample_out)


def _inproj_sample(x_ref, cos_ref, sin_ref, w_ref, qa_ref, kvw_ref, qr_ref, kr_ref, vr_ref, g_ref):
    xb = x_ref[...].astype(BF16)
    qa_ref[...] = _proj(xb, w_ref, 0) * HD_ATT ** -0.5
    kvw_ref[:, :W_ATT] = _proj(xb, w_ref, 1)
    kvw_ref[:, W_ATT:] = _proj(xb, w_ref, 2)
    qr, kr, vr, g = _retention_qkvg(xb, w_ref, cos_ref[...], sin_ref[...])
    qr_ref[...] = qr
    kr_ref[...] = kr
    vr_ref[...] = vr
    g_ref[...] = g


def _rope_tables(pos):
    half = DK_RET // 2
    inv_freq = 1.0 / (ROPE_BASE ** jnp.linspace(0.0, 1.0, half, dtype=F32))
    ang = pos.astype(F32)[:, None] * inv_freq[None, :]
    cos, sin = jnp.cos(ang), jnp.sin(ang)
    return jnp.concatenate([cos, cos], axis=1), jnp.concatenate([sin, sin], axis=1)


def _inproj_prompt(x, x_sample, w_in, later_weights, tm):
    S = x.shape[0]
    B = x_sample.shape[0]
    win = min(MAX_WINDOW, S)
    steps = S // tm
    win_first_step = steps - win // tm
    cos, sin = _rope_tables(jnp.concatenate([jnp.arange(tm, dtype=jnp.int32),
                                             jnp.arange(steps, dtype=jnp.int32) * tm]))
    const = lambda i: (0, 0)
    row = lambda i: (i, 0)
    view_specs = [pl.BlockSpec((PAIRS, tm // d, d * LANES), lambda i: (0, i, 0)) for d in (1, 4, 16)] * 3
    view_shapes = [jax.ShapeDtypeStruct((PAIRS, S // d, d * LANES), BF16) for d in (1, 4, 16)] * 3
    state_shape = (H_RET, DK_RET, DV_RET)
    gather_scratch = [pltpu.VMEM((PAIRS, tm, LANES), F32), pltpu.VMEM((4 * PAIRS, tm // 4, LANES), F32)]
    slab_specs = [pl.BlockSpec((w.shape[0] // steps, w.shape[1]), row) for w in later_weights]
    cos_s, sin_s = _rope_tables(jnp.full((B,), PAST_LEN, dtype=jnp.int32))
    sin_s = sin_s * jnp.where(jnp.arange(LANES) < LANES // 2, -1.0, 1.0)
    whole = lambda width: pl.BlockSpec((B, width), const)
    sample_widths = (W_ATT, 2 * W_ATT, W_RET, W_RET, W_RET, W_RET)
    return pl.pallas_call(
        functools.partial(_inproj_prompt_body, win_first_step=win_first_step, n_steps=steps,
                          n_cast=len(later_weights)),
        grid=(steps,),
        in_specs=[pl.BlockSpec((tm, D_MODEL), row),
                  pl.BlockSpec((D_MODEL, N_SPLITS * W_ATT), lambda i: (0, 0)),
                  pl.BlockSpec((tm, LANES), const), pl.BlockSpec((tm, LANES), const),
                  pl.BlockSpec((steps, LANES), const), pl.BlockSpec((steps, LANES), const),
                  whole(D_MODEL), whole(LANES), whole(LANES)] + slab_specs,
        out_specs=view_specs + [
            pl.BlockSpec((2 * W_ATT, tm), lambda i: (0, jnp.maximum(i - win_first_step, 0))),
            pl.BlockSpec((tm, W_RET), row),
            pl.BlockSpec(state_shape, lambda i: (0, 0, 0))] + [whole(n) for n in sample_widths] + slab_specs,
        out_shape=view_shapes + [
            jax.ShapeDtypeStruct((2 * W_ATT, win), F32),
            jax.ShapeDtypeStruct((S, W_RET), BF16),
            jax.ShapeDtypeStruct(state_shape, F32)] + [jax.ShapeDtypeStruct((B, n), F32) for n in sample_widths]
        + [jax.ShapeDtypeStruct(w.shape, BF16) for w in later_weights],
        scratch_shapes=gather_scratch * 3 + [pltpu.VMEM(state_shape, F32),
                                             pltpu.VMEM((D_MODEL, N_SPLITS * W_ATT), BF16)],
        compiler_params=_cparams(1),
        name="inproj_retention_prompt",
    )(x, w_in, cos[:tm], sin[:tm], cos[tm:], sin[tm:], x_sample, cos_s, sin_s, *later_weights)


def _t5_bucket(dist):
    is_small = dist < MAX_EXACT
    d_f = jnp.maximum(dist, 1).astype(F32)
    large = MAX_EXACT + (jnp.log(d_f / MAX_EXACT) / math.log(MAX_WINDOW / MAX_EXACT)
                         * (N_BUCKETS - MAX_EXACT)).astype(jnp.int32)
    large = jnp.minimum(large, N_BUCKETS - 1)
    return jnp.where(is_small, dist, large)


def _tap_bias(rel_bias):
    taps = jnp.arange(TAPS + 1)
    buckets = jnp.stack([_t5_bucket(taps * d) for _, d in DILATIONS])
    return jnp.transpose(rel_bias[buckets].astype(F32), (0, 2, 1))


def _band_bias_rows(tap_bias):
    rows = jnp.concatenate([tap_bias[..., ::-1] * LOG2E,
                            jnp.full(tap_bias.shape[:-1] + (2 * Q_ROWS - TAPS - 1,), NEG, F32)], axis=-1)
    return rows.reshape(len(DILATIONS), PAIRS, 2, 2 * Q_ROWS)


def _window_bias(tap_bias, window):
    rows = []
    for i, (_, d) in enumerate(DILATIONS):
        taps_rev = tap_bias[i, :, TAPS:0:-1]
        seg = jnp.concatenate([taps_rev[..., None], jnp.full((H_ATT, TAPS, d - 1), NEG, F32)], axis=-1)
        rows.append(jnp.concatenate([jnp.full((H_ATT, window - TAPS * d), NEG, F32),
                                     seg.reshape(H_ATT, TAPS * d)], axis=-1))
    return jnp.stack(rows)


def _attn_probs(q, k, bias2, head0_b):
    qs = jnp.concatenate([q * head0_b, q * (1 - head0_b)], axis=0)
    s = lax.dot_general(qs, k, (((1,), (1,)), ((), ())), preferred_element_type=F32) + bias2
    m = jnp.max(s, axis=1, keepdims=True)
    p = jnp.exp2(s - m)
    return p.astype(BF16), jnp.sum(p, axis=1, keepdims=True), m


def _attn_output(p, l, m, v, head0):
    o = jnp.dot(p, v, preferred_element_type=F32) / l
    lse = m + jnp.log2(l)
    o_pair = jnp.where(head0, o[:Q_ROWS], o[Q_ROWS:])
    lse_pair = jnp.where(head0, jnp.broadcast_to(lse[:Q_ROWS], (Q_ROWS, LANES)),
                         jnp.broadcast_to(lse[Q_ROWS:], (Q_ROWS, LANES)))
    return o_pair, lse_pair


def _attn_body(q1, q4, q16, k1c, k1p, k4c, k4p, k16c, k16p, v1c, v1p, v4c, v4p, v16c, v16p, bias_ref,
               out_ref, o16, l16, o4, l4, band):
    first = pl.program_id(1) == 0
    lane = lax.broadcasted_iota(jnp.int32, (Q_ROWS, LANES), 1)
    head0 = lane < HD_ATT
    head0_b = jnp.where(head0, 1.0, 0.0).astype(BF16)
    col = lax.broadcasted_iota(jnp.int32, (2 * Q_ROWS, 2 * Q_ROWS), 1)
    no_prev = jnp.where(jnp.logical_and(first, col < Q_ROWS), NEG, 0.0)

    @pl.when(first)
    def _():
        for branch in range(len(DILATIONS)):
            for hh in range(2):
                row = jnp.broadcast_to(bias_ref[branch, hh:hh + 1, :], (Q_ROWS, 2 * Q_ROWS))
                band[branch, hh * Q_ROWS:(hh + 1) * Q_ROWS, :] = pltpu.roll(row, 0, 1, stride=1, stride_axis=0)

    def bias2(branch, masked_prev):
        return band[branch] + no_prev if masked_prev else band[branch]

    def cat(a, b):
        return jnp.concatenate([a, b], axis=0)

    def window(cur, prev, b, sl):
        if b == 0:
            return cat(prev[:, sl], cur[0:Q_ROWS, sl])
        return cur[(b - 1) * Q_ROWS:(b + 1) * Q_ROWS, sl]

    units = []
    b16 = bias2(2, True)
    for r in range(16):
        sl = slice(r * LANES, (r + 1) * LANES)

        def store16(o, l, r=r):
            o16[pl.ds(r, Q_ROWS, stride=16), :] = o
            l16[pl.ds(r, Q_ROWS, stride=16), :] = l

        units.append((lambda sl=sl: q16[:, sl], lambda sl=sl: window(k16c, k16p, 0, sl),
                      lambda sl=sl: window(v16c, v16p, 0, sl), b16, store16))

    b4 = (bias2(1, True), bias2(1, False))
    for r in range(4):
        sl = slice(r * LANES, (r + 1) * LANES)
        for b in range(4):
            rows = slice(b * Q_ROWS, (b + 1) * Q_ROWS)

            def store4(o, l, r=r, b=b):
                o4[pl.ds(b * 4 * Q_ROWS + r, Q_ROWS, stride=4), :] = o
                l4[pl.ds(b * 4 * Q_ROWS + r, Q_ROWS, stride=4), :] = l

            units.append((lambda rows=rows, sl=sl: q4[rows, sl], lambda b=b, sl=sl: window(k4c, k4p, b, sl),
                          lambda b=b, sl=sl: window(v4c, v4p, b, sl), b4[min(b, 1)], store4))

    b1 = (bias2(0, True), bias2(0, False))
    everything = slice(None)
    for b in range(ATT_BLOCK // Q_ROWS):
        rows = slice(b * Q_ROWS, (b + 1) * Q_ROWS)

        def mix(oa, la, rows=rows):
            ob, lb = o4[rows, :], l4[rows, :]
            oc, lc = o16[rows, :], l16[rows, :]
            top = jnp.maximum(jnp.maximum(la, lb), lc)
            ea, eb, ec = jnp.exp2(la - top), jnp.exp2(lb - top), jnp.exp2(lc - top)
            out_ref[rows, :] = ((ea * oa + eb * ob + ec * oc) / (ea + eb + ec)).astype(BF16)

        units.append((lambda rows=rows: q1[rows, :], lambda b=b: window(k1c, k1p, b, everything),
                      lambda b=b: window(v1c, v1p, b, everything), b1[min(b, 1)], mix))

    pending = None
    for q, k, v, bias, finish in units:
        probs = _attn_probs(q(), k(), bias, head0_b)
        if pending is not None:
            p_probs, p_v, p_finish = pending
            p_finish(*_attn_output(*p_probs, p_v(), head0))
        pending = (probs, v, finish)
    p_probs, p_v, p_finish = pending
    p_finish(*_attn_output(*p_probs, p_v(), head0))


def _attn_prompt(q1, q4, q16, k1, k4, k16, v1, v4, v16, band_bias):
    S = q1.shape[1]
    assert S % ATT_BLOCK == 0
    steps = S // ATT_BLOCK

    def cur(d):
        return pl.BlockSpec((None, ATT_BLOCK // d, d * LANES), lambda p, c: (p, c, 0))

    def prev(d):
        per_step = ATT_BLOCK // d // Q_ROWS
        return pl.BlockSpec((None, Q_ROWS, d * LANES), lambda p, c: (p, jnp.maximum(c * per_step - 1, 0), 0))

    scratch = pltpu.VMEM((ATT_BLOCK, LANES), F32)
    return pl.pallas_call(
        _attn_body,
        grid=(PAIRS, steps),
        in_specs=[cur(1), cur(4), cur(16),
                  cur(1), prev(1), cur(4), prev(4), cur(16), prev(16),
                  cur(1), prev(1), cur(4), prev(4), cur(16), prev(16),
                  pl.BlockSpec((len(DILATIONS), None, 2, 2 * Q_ROWS), lambda p, c: (0, p, 0, 0))],
        out_specs=pl.BlockSpec((ATT_BLOCK, LANES), lambda p, c: (c, p)),
        out_shape=jax.ShapeDtypeStruct((S, W_ATT), BF16),
        scratch_shapes=[scratch, scratch, scratch, scratch,
                        pltpu.VMEM((len(DILATIONS), 2 * Q_ROWS, 2 * Q_ROWS), F32)],
        compiler_params=_cparams(2),
        name="attn_prompt",
    )(q1, q4, q16, k1, k1, k4, k4, k16, k16, v1, v1, v4, v4, v16, v16, band_bias)


def _group_norm_gate(o, g):
    mu = jnp.mean(o, axis=1, keepdims=True)
    var = jnp.mean(jnp.square(o - mu), axis=1, keepdims=True)
    return g * jax.nn.sigmoid(g) * ((o - mu) * lax.rsqrt(var + GN_EPS))


def _retention_decays(h):
    C = RET_CHUNK
    lg = LOG_GAMMA[h]
    n = lax.broadcasted_iota(jnp.int32, (C, 1), 0).astype(F32)
    diff = (lax.broadcasted_iota(jnp.int32, (C, C), 0) - lax.broadcasted_iota(jnp.int32, (C, C), 1)).astype(F32)
    return (jnp.where(diff >= 0, jnp.exp(lg * jnp.maximum(diff, 0.0)), 0.0),
            jnp.exp(lg * (n + 1.0)), jnp.exp(lg * (C - 1.0 - n)), math.exp(lg * C))


def _retention_chunk_matmuls(q, k, v, S, decays):
    decay, q_decay, k_decay, chunk_decay = decays
    qb, vb = q.astype(BF16), v.astype(BF16)
    scores = lax.dot_general(qb, k.astype(BF16), (((1,), (1,)), ((), ())), preferred_element_type=F32) * decay
    cross = jnp.dot(qb, S.astype(BF16), preferred_element_type=F32) * q_decay
    k_dec_t = (k * k_decay).T.astype(BF16)
    return scores.astype(BF16), cross, vb, chunk_decay * S + jnp.dot(k_dec_t, vb, preferred_element_type=F32)


def _retention_chunk_output(scores, cross, vb, g):
    return _group_norm_gate(jnp.dot(scores, vb, preferred_element_type=F32) + cross, g)


def _as_column(row_pair):
    return jnp.broadcast_to(row_pair, (LANES, LANES)).T


def _sample_mixer_body(qa_ref, kvw_ref, qr_ref, kr_ref, vr_ref, g_ref, cache_ref, st_ref, bias_ref, bias0_ref,
                       oatt_ref, yret_ref, stout_ref):
    W = cache_ref.shape[-1]
    row = pl.ds(pl.program_id(0), 1)
    qa, kvw = qa_ref[row, :], kvw_ref[row, :]
    logit_rows, self_rows = [], []
    for p in range(PAIRS):
        lanes = slice(p * LANES, (p + 1) * LANES)
        q_pair = qa[:, lanes]
        k_pair = kvw[:, lanes]
        q_col = _as_column(q_pair)
        for hh in range(2):
            feat = slice(hh * HD_ATT, (hh + 1) * HD_ATT)
            logit_rows.append(jnp.sum(cache_ref[0, 2 * p + hh] * q_col[feat, 0:1], axis=0, keepdims=True))
            self_rows.append(jnp.sum(q_pair[:, feat] * k_pair[:, feat], axis=1, keepdims=True))
    logits = jnp.concatenate(logit_rows, axis=0)
    s_self = jnp.concatenate(self_rows, axis=0) + bias0_ref[:, 0:1]

    probs, p_selfs, denoms, lses = [], [], [], []
    for i, (_, d) in enumerate(DILATIONS):
        lo = W - TAPS * d
        s = logits[:, lo:] + bias_ref[i, :, lo:]
        m = jnp.maximum(jnp.max(s, axis=1, keepdims=True), s_self)
        pr = jnp.exp(s - m)
        p_self = jnp.exp(s_self - m)
        l = jnp.sum(pr, axis=1, keepdims=True) + p_self
        probs.append(pr)
        p_selfs.append(p_self)
        denoms.append(l)
        lses.append(m + jnp.log(l))
    top = jnp.maximum(jnp.maximum(lses[0], lses[1]), lses[2])
    e = [jnp.exp(x - top) for x in lses]
    e_sum = e[0] + e[1] + e[2]
    coef = [e[i] / (e_sum * denoms[i]) for i in range(3)]
    lo4, lo1 = W - TAPS * 4, W - TAPS
    p16, p4, p1 = coef[2] * probs[2], coef[1] * probs[1], coef[0] * probs[0]
    p_all = jnp.concatenate([p16[:, :lo4], p16[:, lo4:lo1] + p4[:, :lo1 - lo4],
                             p16[:, lo1:] + p4[:, lo1 - lo4:] + p1], axis=1)
    c_self = coef[0] * p_selfs[0] + coef[1] * p_selfs[1] + coef[2] * p_selfs[2]
    out_pairs = []
    for p in range(PAIRS):
        lanes = slice(p * LANES, (p + 1) * LANES)
        v_col = _as_column(kvw[:, W_ATT + p * LANES:W_ATT + (p + 1) * LANES])
        out_cols = []
        for hh in range(2):
            h = 2 * p + hh
            feat = slice(hh * HD_ATT, (hh + 1) * HD_ATT)
            pv = jnp.sum(cache_ref[1, h] * p_all[h:h + 1, :], axis=1, keepdims=True)
            out_cols.append(pv + c_self[h:h + 1, :] * v_col[feat, 0:1])
        col = jnp.broadcast_to(jnp.concatenate(out_cols, axis=0), (LANES, LANES))
        out_pairs.append(col.T[0:1, :])
    oatt_ref[row, :] = jnp.concatenate(out_pairs, axis=1)

    qr, kr, vr, g = qr_ref[row, :], kr_ref[row, :], vr_ref[row, :], g_ref[row, :]
    y_heads = []
    for h in range(H_RET):
        cols = slice(h * DK_RET, (h + 1) * DK_RET)
        gamma = math.exp(LOG_GAMMA[h])
        qh, kh, vh = qr[:, cols], kr[:, cols], vr[:, cols]
        S = st_ref[h]
        qk = jnp.sum(qh * kh, axis=1, keepdims=True)
        cross = jnp.dot(jnp.broadcast_to(qh, (16, DK_RET)).astype(BF16), S.astype(BF16),
                        preferred_element_type=F32)[:1]
        o = qk * vh + cross * gamma
        stout_ref[h] = gamma * S + _as_column(kh) * vh
        y_heads.append(_group_norm_gate(o, g[:, cols]))
    yret_ref[row, :] = jnp.concatenate(y_heads, axis=1)


def _mixer_operands(qa, kvw, qr, kr, vr, g, cache_t, state, tap_bias):
    B, _, _, _, W = cache_t.shape
    assert W == MAX_WINDOW and PAST_LEN >= MAX_WINDOW
    vec = lambda width: pl.BlockSpec((B, width), lambda b: (0, 0))
    bias0 = jnp.broadcast_to(tap_bias[0, :, 0:1], (H_ATT, LANES))
    token = lambda b: jnp.minimum(b, B - 1)
    st_spec = pl.BlockSpec((None, H_RET, DK_RET, DV_RET), lambda b: (token(b), 0, 0, 0))
    in_specs = [vec(W_ATT), vec(2 * W_ATT), vec(W_RET), vec(W_RET), vec(W_RET), vec(W_RET),
                pl.BlockSpec((None, 2, H_ATT, HD_ATT, W), lambda b: (token(b), 0, 0, 0, 0)),
                st_spec,
                pl.BlockSpec((len(DILATIONS), H_ATT, W), lambda b: (0, 0, 0)),
                pl.BlockSpec((H_ATT, LANES), lambda b: (0, 0))]
    args = (qa, kvw, qr, kr, vr, g, cache_t, state, _window_bias(tap_bias, W), bias0)
    out_specs = [vec(W_ATT), vec(W_RET), st_spec]
    out_shape = [jax.ShapeDtypeStruct((B, W_ATT), F32), jax.ShapeDtypeStruct((B, W_RET), F32),
                 jax.ShapeDtypeStruct(state.shape, F32)]
    return in_specs, args, out_specs, out_shape


FF_CHUNK = 1024
FFN_MIN_ROWS = 128


def _layernorm(z, g, b):
    mu = jnp.mean(z, axis=1, keepdims=True)
    var = jnp.mean(jnp.square(z - mu), axis=1, keepdims=True)
    return (z - mu) * lax.rsqrt(var + LN_EPS) * g + b


def _out_ffn_body(x_ref, oatt_ref, yret_ref, wo_ref, g1_ref, b1_ref, wu_ref, wd_ref, g2_ref, b2_ref, y_ref):
    tm = x_ref.shape[0]
    n_parts = 2 if tm >= 2 * FFN_MIN_ROWS else 1
    parts = [slice(i * (tm // n_parts), (i + 1) * (tm // n_parts)) for i in range(n_parts)]
    mixes = [jnp.dot(oatt_ref[r, :].astype(BF16), wo_ref[:W_ATT, :], preferred_element_type=F32)
             + jnp.dot(yret_ref[r, :].astype(BF16), wo_ref[W_ATT:, :], preferred_element_type=F32) for r in parts]
    n_chunks = D_FF // FF_CHUNK
    for r, mix in zip(parts, mixes):
        x1 = _layernorm(ALPHA * x_ref[r, :] + mix, g1_ref[...], b1_ref[...])
        x1b = x1.astype(BF16)

        def up(j):
            cols = slice(j * FF_CHUNK, (j + 1) * FF_CHUNK)
            return jnp.square(jnp.maximum(jnp.dot(x1b, wu_ref[:, cols], preferred_element_type=F32),
                                          0.0)).astype(BF16)

        ffn = jnp.zeros_like(x1)
        h = up(0)
        for j in range(n_chunks):
            h_next = up(j + 1) if j + 1 < n_chunks else None
            ffn = ffn + jnp.dot(h, wd_ref[j * FF_CHUNK:(j + 1) * FF_CHUNK, :], preferred_element_type=F32)
            h = h_next
        y_ref[r, :] = _layernorm(ALPHA * x1 + ffn, g2_ref[...], b2_ref[...])


def _ffn_operands(x, oatt, yret, wo_b, g1, b1, wu_b, wd_b, g2, b2, tm):
    row = lambda i: (jnp.minimum(i, x.shape[0] // tm - 1), 0)
    const = lambda shape: pl.BlockSpec(shape, lambda i: (0, 0), pipeline_mode=pl.Buffered(1))
    in_specs = [pl.BlockSpec((tm, D_MODEL), row), pl.BlockSpec((tm, W_ATT), row), pl.BlockSpec((tm, W_RET), row),
                const((W_ATT + W_RET, D_MODEL)), const((1, D_MODEL)), const((1, D_MODEL)),
                const((D_MODEL, D_FF)), const((D_FF, D_MODEL)), const((1, D_MODEL)), const((1, D_MODEL))]
    args = (x, oatt, yret, wo_b, g1, b1, wu_b, wd_b, g2, b2)
    return in_specs, args, pl.BlockSpec((tm, D_MODEL), row), jax.ShapeDtypeStruct((x.shape[0], D_MODEL), F32)


N_FFN_ACTIVATIONS = 3


def _ffn_mixer_tail_body(*refs, n_ffn_in, n_mix_in, n_tokens):
    ffn_in, mix_in = refs[:n_ffn_in], refs[n_ffn_in:n_ffn_in + n_mix_in]
    xs_ref = refs[n_ffn_in + n_mix_in]
    y_ref, oatt_s_ref, yret_s_ref, st_ref, ys_ref = refs[n_ffn_in + n_mix_in + 1:]
    step = pl.program_id(0)

    @pl.when(step < n_tokens)
    def _():
        _sample_mixer_body(*mix_in, oatt_s_ref, yret_s_ref, st_ref)
        _out_ffn_body(*ffn_in, y_ref)

    @pl.when(step == n_tokens)
    def _():
        _out_ffn_body(xs_ref, oatt_s_ref, yret_s_ref, *ffn_in[N_FFN_ACTIVATIONS:], ys_ref)


def _out_ffn_prompt_and_sample(ffn_args, mixer_args, x_sample):
    B = mixer_args[0].shape[0]
    tm = ffn_args[0].shape[0] // B
    f_in, f_args, f_out, f_shape = _ffn_operands(*ffn_args, tm)
    m_in, m_args, m_out, m_shape = _mixer_operands(*mixer_args)
    whole = pl.BlockSpec((B, D_MODEL), lambda i: (0, 0))
    y, oatt, yret, st, y_s = pl.pallas_call(
        functools.partial(_ffn_mixer_tail_body, n_ffn_in=len(f_in), n_mix_in=len(m_in), n_tokens=B),
        grid=(B + 1,), in_specs=f_in + m_in + [whole], out_specs=[f_out] + m_out + [whole],
        out_shape=[f_shape] + m_shape + [jax.ShapeDtypeStruct((B, D_MODEL), F32)],
        compiler_params=_cparams(1), name="out_ffn_prompt_sample_mixer",
    )(*f_args, *m_args, x_sample)
    return y, st, y_s


PROMPT_ROWS = 512


def kernel(x_prompt, x_sample, cache_kv_win, state_ret, w_in, rel_bias, w_out,
           ln1_g, ln1_b, w_up, w_down, ln2_g, ln2_b):
    assert x_prompt.shape[0] == 1 and x_sample.shape[1] == 1 and w_in.shape[0] == DEPTH
    B = x_sample.shape[0]
    g1, b1, g2, b2 = ln1_g[0][None], ln1_b[0][None], ln2_g[0][None], ln2_b[0][None]
    tap_bias = _tap_bias(rel_bias)

    xp = x_prompt[0]
    xs = x_sample[:, 0]
    *qkv_views, kvw_p, yret, st_p, qa_s, kvw_s, qr_s, kr_s, vr_s, g_s, wo_b, wu_b, wd_b = _inproj_prompt(
        xp, xs, w_in[0], (w_out[0], w_up[0], w_down[0]), PROMPT_ROWS)
    oatt = _attn_prompt(*qkv_views, _band_bias_rows(tap_bias))

    cache_t = jnp.transpose(cache_kv_win[0], (0, 2, 3, 4, 1))
    ffn_weights = (wo_b, g1, b1, wu_b, wd_b, g2, b2)
    y_p, st_s, y_s = _out_ffn_prompt_and_sample(
        (xp, oatt, yret) + ffn_weights, (qa_s, kvw_s, qr_s, kr_s, vr_s, g_s, cache_t, state_ret[0], tap_bias), xs)

    win = kvw_p.shape[1]
    kv_win_p = jnp.transpose(kvw_p.reshape(2, H_ATT, HD_ATT, win), (3, 0, 1, 2))
    return (y_p[None], y_s[:, None],
            kv_win_p[None, None], kvw_s.reshape(1, B, 1, 2, H_ATT, HD_ATT),
            st_p[None, None], st_s[None])
```

```python
import functools
import math

import jax
import jax.numpy as jnp
from jax import lax
from jax.experimental import pallas as pl
from jax.experimental.pallas import tpu as pltpu

F32 = jnp.float32
BF16 = jnp.bfloat16

D_MODEL = 1024
DEPTH = 1
PAST_LEN = 16384
W_ATT = 512
HD_ATT = 64
H_ATT = 8
DILATIONS = ((128, 1), (512, 4), (2048, 16))
TAPS = 128
MAX_WINDOW = 2048
N_BUCKETS = 32
MAX_EXACT = N_BUCKETS // 2
W_RET = 512
H_RET = 4
DK_RET = 128
DV_RET = 128
RET_CHUNK = 128
ROPE_BASE = 10000.0
D_FF = 4096
N_SPLITS = 7
ALPHA = (2.0 * DEPTH) ** 0.25
LN_EPS = 1e-5
GN_EPS = 1e-6
LOG_GAMMA = tuple(math.log(1.0 - 2.0 ** (-5.0 - h)) for h in range(H_RET))

LANES = 128
PAIRS = W_ATT // LANES
NEG = -1e30
LOG2E = math.log2(math.e)

ATT_BLOCK = 2048
Q_ROWS = 128
VMEM_LIMIT = 56 * 1024 * 1024


def _cparams(n_axes):
    return pltpu.CompilerParams(dimension_semantics=("arbitrary",) * n_axes, vmem_limit_bytes=VMEM_LIMIT)


def _rotary(u, cos, sin_signed):
    outs = []
    for h in range(H_RET):
        xh = u[:, h * LANES:(h + 1) * LANES]
        outs.append(xh * cos + pltpu.roll(xh, LANES // 2, 1) * sin_signed)
    return jnp.concatenate(outs, axis=1)


def _proj(xb, w_ref, j):
    return jnp.dot(xb, w_ref[:, j * W_ATT:(j + 1) * W_ATT], preferred_element_type=F32)


def _store_rows(val, rows, nat_ref, rows_scr):
    for p in range(PAIRS):
        x = val[:, p * LANES:(p + 1) * LANES]
        nat_ref[p, rows, :] = x.astype(BF16)
        rows_scr[p, rows, :] = x


def _store_gathered(by4_ref, by16_ref, rows_scr, by4_scr):
    tm = rows_scr.shape[1]
    for p in range(PAIRS):
        for r4 in range(4):
            a = rows_scr[p, pl.ds(r4, tm // 4, stride=4), :]
            by4_ref[p, :, r4 * LANES:(r4 + 1) * LANES] = a.astype(BF16)
            by4_scr[p * 4 + r4] = a
    for p in range(PAIRS):
        for r4 in range(4):
            for rr in range(4):
                r16 = r4 + 4 * rr
                b = by4_scr[p * 4 + r4, pl.ds(rr, tm // 16, stride=4), :]
                by16_ref[p, :, r16 * LANES:(r16 + 1) * LANES] = b.astype(BF16)


def _retention_qkvg(xb, w_ref, cos, sin):
    return (_rotary(_proj(xb, w_ref, 3), cos, sin), _rotary(_proj(xb, w_ref, 4), cos, sin) * DK_RET ** -0.5,
            _proj(xb, w_ref, 5), _proj(xb, w_ref, 6))


N_SAMPLE_IN = 3
N_SAMPLE_OUT = 6


def _inproj_prompt_body(*refs, win_first_step, n_steps, n_cast):
    x_ref, w_f32_ref, cos_row_ref, sin_row_ref, cos_step_ref, sin_step_ref = refs[:6]
    sample_in = refs[6:6 + N_SAMPLE_IN]
    cast_in, outs = refs[6 + N_SAMPLE_IN:6 + N_SAMPLE_IN + n_cast], refs[6 + N_SAMPLE_IN + n_cast:]
    q1_ref, q4_ref, q16_ref, k1_ref, k4_ref, k16_ref, v1_ref, v4_ref, v16_ref, kvw_ref, yret_ref, st_ref = outs[:12]
    sample_out = outs[12:12 + N_SAMPLE_OUT]
    cast_out = outs[12 + N_SAMPLE_OUT:12 + N_SAMPLE_OUT + n_cast]
    q_rows, q_by4, k_rows, k_by4, v_rows, v_by4, state, w_ref = outs[12 + N_SAMPLE_OUT + n_cast:]

    @pl.when(pl.program_id(0) == 0)
    def _():
        state[...] = jnp.zeros_like(state)
        for j in range(N_SPLITS):
            cols = slice(j * W_ATT, (j + 1) * W_ATT)
            w_ref[:, cols] = w_f32_ref[:, cols].astype(BF16)

    for src, dst in zip(cast_in, cast_out):
        dst[...] = src[...].astype(BF16)

    xb = x_ref[...].astype(BF16)
    tm = xb.shape[0]
    half = tm // 2

    step = pl.ds(pl.program_id(0), 1)
    ca, sa = cos_step_ref[step, :], sin_step_ref[step, :]
    cb, sb = cos_row_ref[...], sin_row_ref[...]
    sign = jnp.where(lax.broadcasted_iota(jnp.int32, (1, LANES), 1) < LANES // 2, -1.0, 1.0)
    cos, sin = ca * cb - sa * sb, (sa * cb + ca * sb) * sign

    def retention_proj(j, rows):
        u = _proj(xb[rows], w_ref, 3 + j)
        if j < 2:
            u = _rotary(u, cos[rows], sin[rows])
        return u * DK_RET ** -0.5 if j == 1 else u

    attn_out = ((q1_ref, q4_ref, q16_ref, q_rows, q_by4), (k1_ref, k4_ref, k16_ref, k_rows, k_by4),
                (v1_ref, v4_ref, v16_ref, v_rows, v_by4))

    def attention_proj(j, rows, last):
        nat_ref, by4_ref, by16_ref, rows_scr, by4_scr = attn_out[j]
        val = _proj(xb[rows], w_ref, j)
        _store_rows(val * (HD_ATT ** -0.5 * LOG2E) if j == 0 else val, rows, nat_ref, rows_scr)
        if last:
            _store_gathered(by4_ref, by16_ref, rows_scr, by4_scr)

    halves = (slice(0, half), slice(half, tm))
    ret_in = [[retention_proj(j, halves[0]) for j in range(4)], [None] * 4]
    fills = [[("ret", 0), ("ret", 1), ("ret", 2), ("ret", 3)],
             [("att", 0), ("att", 1), ("att", 2)]]
    decays = [_retention_decays(h) for h in range(H_RET)]
    S = [state[h] for h in range(H_RET)]

    def emit(fill):
        if fill[0] == "ret":
            ret_in[1][fill[1]] = retention_proj(fill[1], halves[1])
        else:
            attention_proj(fill[1], slice(0, tm), last=True)

    for ph in range(2):
        qr, kr, vr, g = ret_in[ph]
        pending = list(fills[ph])
        for c in range(half // RET_CHUNK):
            rows = slice(c * RET_CHUNK, (c + 1) * RET_CHUNK)
            out_rows = slice(ph * half + c * RET_CHUNK, ph * half + (c + 1) * RET_CHUNK)
            staged = []
            for h in range(H_RET):
                cols = slice(h * DK_RET, (h + 1) * DK_RET)
                scores, cross, vb, S[h] = _retention_chunk_matmuls(qr[rows, cols], kr[rows, cols], vr[rows, cols],
                                                                  S[h], decays[h])
                staged.append((scores, cross, vb))
            if pending:
                emit(pending.pop(0))
            for h in range(H_RET):
                cols = slice(h * DK_RET, (h + 1) * DK_RET)
                yret_ref[out_rows, cols] = _retention_chunk_output(*staged[h], g[rows, cols]).astype(BF16)
            if pending:
                emit(pending.pop(0))
        while pending:
            emit(pending.pop(0))
    for h in range(H_RET):
        state[h] = S[h]
        st_ref[h] = S[h]

    @pl.when(pl.program_id(0) >= win_first_step)
    def _():
        for p in range(PAIRS):
            kvw_ref[p * LANES:(p + 1) * LANES, :] = k_rows[p].T
            kvw_ref[W_ATT + p * LANES:W_ATT + (p + 1) * LANES, :] = v_rows[p].T

    @pl.when(pl.program_id(0) == n_steps - 1)
    def _():
        _inproj_sample(*sample_in, w_ref, *sample_out)


def _inproj_sample(x_ref, cos_ref, sin_ref, w_ref, qa_ref, kvw_ref, qr_ref, kr_ref, vr_ref, g_ref):
    xb = x_ref[...].astype(BF16)
    qa_ref[...] = _proj(xb, w_ref, 0) * HD_ATT ** -0.5
    kvw_ref[:, :W_ATT] = _proj(xb, w_ref, 1)
    kvw_ref[:, W_ATT:] = _proj(xb, w_ref, 2)
    qr, kr, vr, g = _retention_qkvg(xb, w_ref, cos_ref[...], sin_ref[...])
    qr_ref[...] = qr
    kr_ref[...] = kr
    vr_ref[...] = vr
    g_ref[...] = g


def _rope_tables(pos):
    half = DK_RET // 2
    inv_freq = 1.0 / (ROPE_BASE ** jnp.linspace(0.0, 1.0, half, dtype=F32))
    ang = pos.astype(F32)[:, None] * inv_freq[None, :]
    cos, sin = jnp.cos(ang), jnp.sin(ang)
    return jnp.concatenate([cos, cos], axis=1), jnp.concatenate([sin, sin], axis=1)


def _inproj_prompt(x, x_sample, w_in, later_weights, tm):
    S = x.shape[0]
    B = x_sample.shape[0]
    win = min(MAX_WINDOW, S)
    steps = S // tm
    win_first_step = steps - win // tm
    cos, sin = _rope_tables(jnp.concatenate([jnp.arange(tm, dtype=jnp.int32),
                                             jnp.arange(steps, dtype=jnp.int32) * tm]))
    const = lambda i: (0, 0)
    row = lambda i: (i, 0)
    view_specs = [pl.BlockSpec((PAIRS, tm // d, d * LANES), lambda i: (0, i, 0)) for d in (1, 4, 16)] * 3
    view_shapes = [jax.ShapeDtypeStruct((PAIRS, S // d, d * LANES), BF16) for d in (1, 4, 16)] * 3
    state_shape = (H_RET, DK_RET, DV_RET)
    gather_scratch = [pltpu.VMEM((PAIRS, tm, LANES), F32), pltpu.VMEM((4 * PAIRS, tm // 4, LANES), F32)]
    slab_specs = [pl.BlockSpec((w.shape[0] // steps, w.shape[1]), row) for w in later_weights]
    cos_s, sin_s = _rope_tables(jnp.full((B,), PAST_LEN, dtype=jnp.int32))
    sin_s = sin_s * jnp.where(jnp.arange(LANES) < LANES // 2, -1.0, 1.0)
    whole = lambda width: pl.BlockSpec((B, width), const)
    sample_widths = (W_ATT, 2 * W_ATT, W_RET, W_RET, W_RET, W_RET)
    return pl.pallas_call(
        functools.partial(_inproj_prompt_body, win_first_step=win_first_step, n_steps=steps,
                          n_cast=len(later_weights)),
        grid=(steps,),
        in_specs=[pl.BlockSpec((tm, D_MODEL), row),
                  pl.BlockSpec((D_MODEL, N_SPLITS * W_ATT), lambda i: (0, 0)),
                  pl.BlockSpec((tm, LANES), const), pl.BlockSpec((tm, LANES), const),
                  pl.BlockSpec((steps, LANES), const), pl.BlockSpec((steps, LANES), const),
                  whole(D_MODEL), whole(LANES), whole(LANES)] + slab_specs,
        out_specs=view_specs + [
            pl.BlockSpec((2 * W_ATT, tm), lambda i: (0, jnp.maximum(i - win_first_step, 0))),
            pl.BlockSpec((tm, W_RET), row),
            pl.BlockSpec(state_shape, lambda i: (0, 0, 0))] + [whole(n) for n in sample_widths] + slab_specs,
        out_shape=view_shapes + [
            jax.ShapeDtypeStruct((2 * W_ATT, win), F32),
            jax.ShapeDtypeStruct((S, W_RET), BF16),
            jax.ShapeDtypeStruct(state_shape, F32)] + [jax.ShapeDtypeStruct((B, n), F32) for n in sample_widths]
        + [jax.ShapeDtypeStruct(w.shape, BF16) for w in later_weights],
        scratch_shapes=gather_scratch * 3 + [pltpu.VMEM(state_shape, F32),
                                             pltpu.VMEM((D_MODEL, N_SPLITS * W_ATT), BF16)],
        compiler_params=_cparams(1),
        name="inproj_retention_prompt",
    )(x, w_in, cos[:tm], sin[:tm], cos[tm:], sin[tm:], x_sample, cos_s, sin_s, *later_weights)


def _t5_bucket(dist):
    is_small = dist < MAX_EXACT
    d_f = jnp.maximum(dist, 1).astype(F32)
    large = MAX_EXACT + (jnp.log(d_f / MAX_EXACT) / math.log(MAX_WINDOW / MAX_EXACT)
                         * (N_BUCKETS - MAX_EXACT)).astype(jnp.int32)
    large = jnp.minimum(large, N_BUCKETS - 1)
    return jnp.where(is_small, dist, large)


def _tap_bias(rel_bias):
    taps = jnp.arange(TAPS + 1)
    buckets = jnp.stack([_t5_bucket(taps * d) for _, d in DILATIONS])
    return jnp.transpose(rel_bias[buckets].astype(F32), (0, 2, 1))


def _band_bias_rows(tap_bias):
    rows = jnp.concatenate([tap_bias[..., ::-1] * LOG2E,
                            jnp.full(tap_bias.shape[:-1] + (2 * Q_ROWS - TAPS - 1,), NEG, F32)], axis=-1)
    return rows.reshape(len(DILATIONS), PAIRS, 2, 2 * Q_ROWS)


def _window_bias(tap_bias, window):
    rows = []
    for i, (_, d) in enumerate(DILATIONS):
        taps_rev = tap_bias[i, :, TAPS:0:-1]
        seg = jnp.concatenate([taps_rev[..., None], jnp.full((H_ATT, TAPS, d - 1), NEG, F32)], axis=-1)
        rows.append(jnp.concatenate([jnp.full((H_ATT, window - TAPS * d), NEG, F32),
                                     seg.reshape(H_ATT, TAPS * d)], axis=-1))
    return jnp.stack(rows)


def _attn_probs(q, k, bias2, head0_b):
    qs = jnp.concatenate([q * head0_b, q * (1 - head0_b)], axis=0)
    s = lax.dot_general(qs, k, (((1,), (1,)), ((), ())), preferred_element_type=F32) + bias2
    m = jnp.max(s, axis=1, keepdims=True)
    p = jnp.exp2(s - m)
    return p.astype(BF16), jnp.sum(p, axis=1, keepdims=True), m


def _attn_output(p, l, m, v, head0):
    pv = jnp.dot(p, v, preferred_element_type=F32)

    def pair(x):
        tile = (Q_ROWS, LANES)
        return jnp.where(head0, jnp.broadcast_to(x[:Q_ROWS], tile), jnp.broadcast_to(x[Q_ROWS:], tile))

    l_pair = pair(l)
    return pair(pv) / l_pair, pair(m) + jnp.log2(l_pair)


def _attn_body(q1, q4, q16, k1c, k1p, k4c, k4p, k16c, k16p, v1c, v1p, v4c, v4p, v16c, v16p, bias_ref,
               out_ref, o16, l16, o4, l4, band):
    first = pl.program_id(1) == 0
    lane = lax.broadcasted_iota(jnp.int32, (Q_ROWS, LANES), 1)
    head0 = lane < HD_ATT
    head0_b = jnp.where(head0, 1.0, 0.0).astype(BF16)
    col = lax.broadcasted_iota(jnp.int32, (2 * Q_ROWS, 2 * Q_ROWS), 1)
    no_prev = jnp.where(jnp.logical_and(first, col < Q_ROWS), NEG, 0.0)

    @pl.when(first)
    def _():
        for branch in range(len(DILATIONS)):
            for hh in range(2):
                row = jnp.broadcast_to(bias_ref[branch, hh:hh + 1, :], (Q_ROWS, 2 * Q_ROWS))
                band[branch, hh * Q_ROWS:(hh + 1) * Q_ROWS, :] = pltpu.roll(row, 0, 1, stride=1, stride_axis=0)

    def bias2(branch, masked_prev):
        return band[branch] + no_prev if masked_prev else band[branch]

    def cat(a, b):
        return jnp.concatenate([a, b], axis=0)

    def window(cur, prev, b, sl):
        if b == 0:
            return cat(prev[:, sl], cur[0:Q_ROWS, sl])
        return cur[(b - 1) * Q_ROWS:(b + 1) * Q_ROWS, sl]

    units = []
    b16 = bias2(2, True)
    for r in range(16):
        sl = slice(r * LANES, (r + 1) * LANES)

        def store16(o, l, r=r):
            o16[pl.ds(r, Q_ROWS, stride=16), :] = o
            l16[pl.ds(r, Q_ROWS, stride=16), :] = l

        units.append((lambda sl=sl: q16[:, sl], lambda sl=sl: window(k16c, k16p, 0, sl),
                      lambda sl=sl: window(v16c, v16p, 0, sl), b16, store16))

    b4 = (bias2(1, True), bias2(1, False))
    for r in range(4):
        sl = slice(r * LANES, (r + 1) * LANES)
        for b in range(4):
            rows = slice(b * Q_ROWS, (b + 1) * Q_ROWS)

            def store4(o, l, r=r, b=b):
                o4[pl.ds(b * 4 * Q_ROWS + r, Q_ROWS, stride=4), :] = o
                l4[pl.ds(b * 4 * Q_ROWS + r, Q_ROWS, stride=4), :] = l

            units.append((lambda rows=rows, sl=sl: q4[rows, sl], lambda b=b, sl=sl: window(k4c, k4p, b, sl),
                          lambda b=b, sl=sl: window(v4c, v4p, b, sl), b4[min(b, 1)], store4))

    b1 = (bias2(0, True), bias2(0, False))
    everything = slice(None)
    for b in range(ATT_BLOCK // Q_ROWS):
        rows = slice(b * Q_ROWS, (b + 1) * Q_ROWS)

        def mix(oa, la, rows=rows):
            ob, lb = o4[rows, :], l4[rows, :]
            oc, lc = o16[rows, :], l16[rows, :]
            top = jnp.maximum(jnp.maximum(la, lb), lc)
            ea, eb, ec = jnp.exp2(la - top), jnp.exp2(lb - top), jnp.exp2(lc - top)
            out_ref[rows, :] = ((ea * oa + eb * ob + ec * oc) / (ea + eb + ec)).astype(BF16)

        units.append((lambda rows=rows: q1[rows, :], lambda b=b: window(k1c, k1p, b, everything),
                      lambda b=b: window(v1c, v1p, b, everything), b1[min(b, 1)], mix))

    pending = None
    for q, k, v, bias, finish in units:
        probs = _attn_probs(q(), k(), bias, head0_b)
        if pending is not None:
            p_probs, p_v, p_finish = pending
            p_finish(*_attn_output(*p_probs, p_v(), head0))
        pending = (probs, v, finish)
    p_probs, p_v, p_finish = pending
    p_finish(*_attn_output(*p_probs, p_v(), head0))


def _attn_prompt(q1, q4, q16, k1, k4, k16, v1, v4, v16, band_bias):
    S = q1.shape[1]
    assert S % ATT_BLOCK == 0
    steps = S // ATT_BLOCK

    def cur(d):
        return pl.BlockSpec((None, ATT_BLOCK // d, d * LANES), lambda p, c: (p, c, 0))

    def prev(d):
        per_step = ATT_BLOCK // d // Q_ROWS
        return pl.BlockSpec((None, Q_ROWS, d * LANES), lambda p, c: (p, jnp.maximum(c * per_step - 1, 0), 0))

    scratch = pltpu.VMEM((ATT_BLOCK, LANES), F32)
    return pl.pallas_call(
        _attn_body,
        grid=(PAIRS, steps),
        in_specs=[cur(1), cur(4), cur(16),
                  cur(1), prev(1), cur(4), prev(4), cur(16), prev(16),
                  cur(1), prev(1), cur(4), prev(4), cur(16), prev(16),
                  pl.BlockSpec((len(DILATIONS), None, 2, 2 * Q_ROWS), lambda p, c: (0, p, 0, 0))],
        out_specs=pl.BlockSpec((ATT_BLOCK, LANES), lambda p, c: (c, p)),
        out_shape=jax.ShapeDtypeStruct((S, W_ATT), BF16),
        scratch_shapes=[scratch, scratch, scratch, scratch,
                        pltpu.VMEM((len(DILATIONS), 2 * Q_ROWS, 2 * Q_ROWS), F32)],
        compiler_params=_cparams(2),
        name="attn_prompt",
    )(q1, q4, q16, k1, k1, k4, k4, k16, k16, v1, v1, v4, v4, v16, v16, band_bias)


def _group_norm_gate(o, g):
    mu = jnp.mean(o, axis=1, keepdims=True)
    var = jnp.mean(jnp.square(o - mu), axis=1, keepdims=True)
    return g * jax.nn.sigmoid(g) * ((o - mu) * lax.rsqrt(var + GN_EPS))


def _retention_decays(h):
    C = RET_CHUNK
    lg = LOG_GAMMA[h]
    n = lax.broadcasted_iota(jnp.int32, (C, 1), 0).astype(F32)
    diff = (lax.broadcasted_iota(jnp.int32, (C, C), 0) - lax.broadcasted_iota(jnp.int32, (C, C), 1)).astype(F32)
    return (jnp.where(diff >= 0, jnp.exp(lg * jnp.maximum(diff, 0.0)), 0.0),
            jnp.exp(lg * (n + 1.0)), jnp.exp(lg * (C - 1.0 - n)), math.exp(lg * C))


def _retention_chunk_matmuls(q, k, v, S, decays):
    decay, q_decay, k_decay, chunk_decay = decays
    qb, vb = q.astype(BF16), v.astype(BF16)
    scores = lax.dot_general(qb, k.astype(BF16), (((1,), (1,)), ((), ())), preferred_element_type=F32) * decay
    cross = jnp.dot(qb, S.astype(BF16), preferred_element_type=F32) * q_decay
    k_dec_t = (k * k_decay).T.astype(BF16)
    return scores.astype(BF16), cross, vb, chunk_decay * S + jnp.dot(k_dec_t, vb, preferred_element_type=F32)


def _retention_chunk_output(scores, cross, vb, g):
    return _group_norm_gate(jnp.dot(scores, vb, preferred_element_type=F32) + cross, g)


def _as_column(row_pair):
    return jnp.broadcast_to(row_pair, (LANES, LANES)).T


def _sample_mixer_body(qa_ref, kvw_ref, qr_ref, kr_ref, vr_ref, g_ref, cache_ref, st_ref, bias_ref, bias0_ref,
                       oatt_ref, yret_ref, stout_ref):
    W = cache_ref.shape[-1]
    row = pl.ds(pl.program_id(0), 1)
    qa, kvw = qa_ref[row, :], kvw_ref[row, :]
    logit_rows, self_rows = [], []
    for p in range(PAIRS):
        lanes = slice(p * LANES, (p + 1) * LANES)
        q_pair = qa[:, lanes]
        k_pair = kvw[:, lanes]
        q_col = _as_column(q_pair)
        for hh in range(2):
            feat = slice(hh * HD_ATT, (hh + 1) * HD_ATT)
            logit_rows.append(jnp.sum(cache_ref[0, 2 * p + hh] * q_col[feat, 0:1], axis=0, keepdims=True))
            self_rows.append(jnp.sum(q_pair[:, feat] * k_pair[:, feat], axis=1, keepdims=True))
    logits = jnp.concatenate(logit_rows, axis=0)
    s_self = jnp.concatenate(self_rows, axis=0) + bias0_ref[:, 0:1]

    probs, p_selfs, denoms, lses = [], [], [], []
    for i, (_, d) in enumerate(DILATIONS):
        lo = W - TAPS * d
        s = logits[:, lo:] + bias_ref[i, :, lo:]
        m = jnp.maximum(jnp.max(s, axis=1, keepdims=True), s_self)
        pr = jnp.exp(s - m)
        p_self = jnp.exp(s_self - m)
        l = jnp.sum(pr, axis=1, keepdims=True) + p_self
        probs.append(pr)
        p_selfs.append(p_self)
        denoms.append(l)
        lses.append(m + jnp.log(l))
    top = jnp.maximum(jnp.maximum(lses[0], lses[1]), lses[2])
    e = [jnp.exp(x - top) for x in lses]
    e_sum = e[0] + e[1] + e[2]
    coef = [e[i] / (e_sum * denoms[i]) for i in range(3)]
    lo4, lo1 = W - TAPS * 4, W - TAPS
    p16, p4, p1 = coef[2] * probs[2], coef[1] * probs[1], coef[0] * probs[0]
    p_all = jnp.concatenate([p16[:, :lo4], p16[:, lo4:lo1] + p4[:, :lo1 - lo4],
                             p16[:, lo1:] + p4[:, lo1 - lo4:] + p1], axis=1)
    c_self = coef[0] * p_selfs[0] + coef[1] * p_selfs[1] + coef[2] * p_selfs[2]
    out_pairs = []
    for p in range(PAIRS):
        lanes = slice(p * LANES, (p + 1) * LANES)
        v_col = _as_column(kvw[:, W_ATT + p * LANES:W_ATT + (p + 1) * LANES])
        out_cols = []
        for hh in range(2):
            h = 2 * p + hh
            feat = slice(hh * HD_ATT, (hh + 1) * HD_ATT)
            pv = jnp.sum(cache_ref[1, h] * p_all[h:h + 1, :], axis=1, keepdims=True)
            out_cols.append(pv + c_self[h:h + 1, :] * v_col[feat, 0:1])
        col = jnp.broadcast_to(jnp.concatenate(out_cols, axis=0), (LANES, LANES))
        out_pairs.append(col.T[0:1, :])
    oatt_ref[row, :] = jnp.concatenate(out_pairs, axis=1)

    qr, kr, vr, g = qr_ref[row, :], kr_ref[row, :], vr_ref[row, :], g_ref[row, :]
    y_heads = []
    for h in range(H_RET):
        cols = slice(h * DK_RET, (h + 1) * DK_RET)
        gamma = math.exp(LOG_GAMMA[h])
        qh, kh, vh = qr[:, cols], kr[:, cols], vr[:, cols]
        S = st_ref[h]
        qk = jnp.sum(qh * kh, axis=1, keepdims=True)
        cross = jnp.dot(jnp.broadcast_to(qh, (16, DK_RET)).astype(BF16), S.astype(BF16),
                        preferred_element_type=F32)[:1]
        o = qk * vh + cross * gamma
        stout_ref[h] = gamma * S + _as_column(kh) * vh
        y_heads.append(_group_norm_gate(o, g[:, cols]))
    yret_ref[row, :] = jnp.concatenate(y_heads, axis=1)


def _mixer_operands(qa, kvw, qr, kr, vr, g, cache_t, state, tap_bias):
    B, _, _, _, W = cache_t.shape
    assert W == MAX_WINDOW and PAST_LEN >= MAX_WINDOW
    vec = lambda width: pl.BlockSpec((B, width), lambda b: (0, 0))
    bias0 = jnp.broadcast_to(tap_bias[0, :, 0:1], (H_ATT, LANES))
    token = lambda b: jnp.minimum(b, B - 1)
    st_spec = pl.BlockSpec((None, H_RET, DK_RET, DV_RET), lambda b: (token(b), 0, 0, 0))
    in_specs = [vec(W_ATT), vec(2 * W_ATT), vec(W_RET), vec(W_RET), vec(W_RET), vec(W_RET),
                pl.BlockSpec((None, 2, H_ATT, HD_ATT, W), lambda b: (token(b), 0, 0, 0, 0)),
                st_spec,
                pl.BlockSpec((len(DILATIONS), H_ATT, W), lambda b: (0, 0, 0)),
                pl.BlockSpec((H_ATT, LANES), lambda b: (0, 0))]
    args = (qa, kvw, qr, kr, vr, g, cache_t, state, _window_bias(tap_bias, W), bias0)
    out_specs = [vec(W_ATT), vec(W_RET), st_spec]
    out_shape = [jax.ShapeDtypeStruct((B, W_ATT), F32), jax.ShapeDtypeStruct((B, W_RET), F32),
                 jax.ShapeDtypeStruct(state.shape, F32)]
    return in_specs, args, out_specs, out_shape


FF_CHUNK = 1024
FFN_MIN_ROWS = 128


def _layernorm(z, g, b):
    mu = jnp.mean(z, axis=1, keepdims=True)
    var = jnp.mean(jnp.square(z - mu), axis=1, keepdims=True)
    return (z - mu) * lax.rsqrt(var + LN_EPS) * g + b


def _out_ffn_body(x_ref, oatt_ref, yret_ref, wo_ref, g1_ref, b1_ref, wu_ref, wd_ref, g2_ref, b2_ref, y_ref):
    tm = x_ref.shape[0]
    n_parts = 2 if tm >= 2 * FFN_MIN_ROWS else 1
    parts = [slice(i * (tm // n_parts), (i + 1) * (tm // n_parts)) for i in range(n_parts)]
    mixes = [jnp.dot(oatt_ref[r, :].astype(BF16), wo_ref[:W_ATT, :], preferred_element_type=F32)
             + jnp.dot(yret_ref[r, :].astype(BF16), wo_ref[W_ATT:, :], preferred_element_type=F32) for r in parts]
    n_chunks = D_FF // FF_CHUNK
    for r, mix in zip(parts, mixes):
        x1 = _layernorm(ALPHA * x_ref[r, :] + mix, g1_ref[...], b1_ref[...])
        x1b = x1.astype(BF16)

        def up(j):
            cols = slice(j * FF_CHUNK, (j + 1) * FF_CHUNK)
            return jnp.square(jnp.maximum(jnp.dot(x1b, wu_ref[:, cols], preferred_element_type=F32),
                                          0.0)).astype(BF16)

        ffn = jnp.zeros_like(x1)
        h = up(0)
        for j in range(n_chunks):
            h_next = up(j + 1) if j + 1 < n_chunks else None
            ffn = ffn + jnp.dot(h, wd_ref[j * FF_CHUNK:(j + 1) * FF_CHUNK, :], preferred_element_type=F32)
            h = h_next
        y_ref[r, :] = _layernorm(ALPHA * x1 + ffn, g2_ref[...], b2_ref[...])


def _ffn_operands(x, oatt, yret, wo_b, g1, b1, wu_b, wd_b, g2, b2, tm):
    row = lambda i: (jnp.minimum(i, x.shape[0] // tm - 1), 0)
    const = lambda shape: pl.BlockSpec(shape, lambda i: (0, 0), pipeline_mode=pl.Buffered(1))
    in_specs = [pl.BlockSpec((tm, D_MODEL), row), pl.BlockSpec((tm, W_ATT), row), pl.BlockSpec((tm, W_RET), row),
                const((W_ATT + W_RET, D_MODEL)), const((1, D_MODEL)), const((1, D_MODEL)),
                const((D_MODEL, D_FF)), const((D_FF, D_MODEL)), const((1, D_MODEL)), const((1, D_MODEL))]
    args = (x, oatt, yret, wo_b, g1, b1, wu_b, wd_b, g2, b2)
    return in_specs, args, pl.BlockSpec((tm, D_MODEL), row), jax.ShapeDtypeStruct((x.shape[0], D_MODEL), F32)


N_FFN_ACTIVATIONS = 3


def _ffn_mixer_tail_body(*refs, n_ffn_in, n_mix_in, n_tokens):
    ffn_in, mix_in = refs[:n_ffn_in], refs[n_ffn_in:n_ffn_in + n_mix_in]
    xs_ref = refs[n_ffn_in + n_mix_in]
    y_ref, oatt_s_ref, yret_s_ref, st_ref, ys_ref = refs[n_ffn_in + n_mix_in + 1:]
    step = pl.program_id(0)

    @pl.when(step < n_tokens)
    def _():
        _sample_mixer_body(*mix_in, oatt_s_ref, yret_s_ref, st_ref)
        _out_ffn_body(*ffn_in, y_ref)

    @pl.when(step == n_tokens)
    def _():
        _out_ffn_body(xs_ref, oatt_s_ref, yret_s_ref, *ffn_in[N_FFN_ACTIVATIONS:], ys_ref)


def _out_ffn_prompt_and_sample(ffn_args, mixer_args, x_sample):
    B = mixer_args[0].shape[0]
    tm = ffn_args[0].shape[0] // B
    f_in, f_args, f_out, f_shape = _ffn_operands(*ffn_args, tm)
    m_in, m_args, m_out, m_shape = _mixer_operands(*mixer_args)
    whole = pl.BlockSpec((B, D_MODEL), lambda i: (0, 0))
    y, oatt, yret, st, y_s = pl.pallas_call(
        functools.partial(_ffn_mixer_tail_body, n_ffn_in=len(f_in), n_mix_in=len(m_in), n_tokens=B),
        grid=(B + 1,), in_specs=f_in + m_in + [whole], out_specs=[f_out] + m_out + [whole],
        out_shape=[f_shape] + m_shape + [jax.ShapeDtypeStruct((B, D_MODEL), F32)],
        compiler_params=_cparams(1), name="out_ffn_prompt_sample_mixer",
    )(*f_args, *m_args, x_sample)
    return y, st, y_s


PROMPT_ROWS = 512


def kernel(x_prompt, x_sample, cache_kv_win, state_ret, w_in, rel_bias, w_out,
           ln1_g, ln1_b, w_up, w_down, ln2_g, ln2_b):
    assert x_prompt.shape[0] == 1 and x_sample.shape[1] == 1 and w_in.shape[0] == DEPTH
    B = x_sample.shape[0]
    g1, b1, g2, b2 = ln1_g[0][None], ln1_b[0][None], ln2_g[0][None], ln2_b[0][None]
    tap_bias = _tap_bias(rel_bias)

    xp = x_prompt[0]
    xs = x_sample[:, 0]
    *qkv_views, kvw_p, yret, st_p, qa_s, kvw_s, qr_s, kr_s, vr_s, g_s, wo_b, wu_b, wd_b = _inproj_prompt(
        xp, xs, w_in[0], (w_out[0], w_up[0], w_down[0]), PROMPT_ROWS)
    oatt = _attn_prompt(*qkv_views, _band_bias_rows(tap_bias))

    cache_t = jnp.transpose(cache_kv_win[0], (0, 2, 3, 4, 1))
    ffn_weights = (wo_b, g1, b1, wu_b, wd_b, g2, b2)
    y_p, st_s, y_s = _out_ffn_prompt_and_sample(
        (xp, oatt, yret) + ffn_weights, (qa_s, kvw_s, qr_s, kr_s, vr_s, g_s, cache_t, state_ret[0], tap_bias), xs)

    win = kvw_p.shape[1]
    kv_win_p = jnp.transpose(kvw_p.reshape(2, H_ATT, HD_ATT, win), (3, 0, 1, 2))
    return (y_p[None], y_s[:, None],
            kv_win_p[None, None], kvw_s.reshape(1, B, 1, 2, H_ATT, HD_ATT),
            st_p[None, None], st_s[None])
```

```python
import functools
import math

import jax
import jax.numpy as jnp
from jax import lax
from jax.experimental import pallas as pl
from jax.experimental.pallas import tpu as pltpu

F32 = jnp.float32
BF16 = jnp.bfloat16

D_MODEL = 1024
DEPTH = 1
PAST_LEN = 16384
W_ATT = 512
HD_ATT = 64
H_ATT = 8
DILATIONS = ((128, 1), (512, 4), (2048, 16))
TAPS = 128
MAX_WINDOW = 2048
N_BUCKETS = 32
MAX_EXACT = N_BUCKETS // 2
W_RET = 512
H_RET = 4
DK_RET = 128
DV_RET = 128
RET_CHUNK = 128
ROPE_BASE = 10000.0
D_FF = 4096
N_SPLITS = 7
ALPHA = (2.0 * DEPTH) ** 0.25
LN_EPS = 1e-5
GN_EPS = 1e-6
LOG_GAMMA = tuple(math.log(1.0 - 2.0 ** (-5.0 - h)) for h in range(H_RET))

LANES = 128
BF16_TILE_ROWS = 16
PAIRS = W_ATT // LANES
NEG = -1e30
LOG2E = math.log2(math.e)

ATT_BLOCK = 2048
Q_ROWS = 128
VMEM_LIMIT = 56 * 1024 * 1024


def _cparams(n_axes):
    return pltpu.CompilerParams(dimension_semantics=("arbitrary",) * n_axes, vmem_limit_bytes=VMEM_LIMIT)


def _rotary(u, cos, sin_signed):
    outs = []
    for h in range(H_RET):
        xh = u[:, h * LANES:(h + 1) * LANES]
        outs.append(xh * cos + pltpu.roll(xh, LANES // 2, 1) * sin_signed)
    return jnp.concatenate(outs, axis=1)


def _proj(xb, w_ref, j):
    return jnp.dot(xb, w_ref[:, j * W_ATT:(j + 1) * W_ATT], preferred_element_type=F32)


def _store_rows(val, rows, nat_ref, rows_scr):
    for p in range(PAIRS):
        x = val[:, p * LANES:(p + 1) * LANES]
        nat_ref[p, rows, :] = x.astype(BF16)
        rows_scr[p, rows, :] = x


def _store_gathered(by4_ref, by16_ref, rows_scr, by4_scr):
    tm = rows_scr.shape[1]
    for p in range(PAIRS):
        for r4 in range(4):
            a = rows_scr[p, pl.ds(r4, tm // 4, stride=4), :]
            by4_ref[p, :, r4 * LANES:(r4 + 1) * LANES] = a.astype(BF16)
            by4_scr[p * 4 + r4] = a
    for p in range(PAIRS):
        for r4 in range(4):
            for rr in range(4):
                r16 = r4 + 4 * rr
                b = by4_scr[p * 4 + r4, pl.ds(rr, tm // 16, stride=4), :]
                by16_ref[p, :, r16 * LANES:(r16 + 1) * LANES] = b.astype(BF16)


def _retention_qkvg(xb, w_ref, cos, sin):
    return (_rotary(_proj(xb, w_ref, 3), cos, sin), _rotary(_proj(xb, w_ref, 4), cos, sin) * DK_RET ** -0.5,
            _proj(xb, w_ref, 5), _proj(xb, w_ref, 6))


N_SAMPLE_IN = 3
N_SAMPLE_OUT = 6


def _inproj_prompt_body(*refs, win_first_step, n_steps):
    x_ref, w_f32_ref, cos_row_ref, sin_row_ref, cos_step_ref, sin_step_ref = refs[:6]
    sample_in, outs = refs[6:6 + N_SAMPLE_IN], refs[6 + N_SAMPLE_IN:]
    q1_ref, q4_ref, q16_ref, k1_ref, k4_ref, k16_ref, v1_ref, v4_ref, v16_ref, kvw_ref, yret_ref, st_ref = outs[:12]
    sample_out = outs[12:12 + N_SAMPLE_OUT]
    q_rows, q_by4, k_rows, k_by4, v_rows, v_by4, state, w_ref = outs[12 + N_SAMPLE_OUT:]

    @pl.when(pl.program_id(0) == 0)
    def _():
        state[...] = jnp.zeros_like(state)
        for j in range(N_SPLITS):
            cols = slice(j * W_ATT, (j + 1) * W_ATT)
            w_ref[:, cols] = w_f32_ref[:, cols].astype(BF16)

    xb = x_ref[...].astype(BF16)
    tm = xb.shape[0]
    half = tm // 2

    step = pl.ds(pl.program_id(0), 1)
    ca, sa = cos_step_ref[step, :], sin_step_ref[step, :]
    cb, sb = cos_row_ref[...], sin_row_ref[...]
    sign = jnp.where(lax.broadcasted_iota(jnp.int32, (1, LANES), 1) < LANES // 2, -1.0, 1.0)
    cos, sin = ca * cb - sa * sb, (sa * cb + ca * sb) * sign

    def retention_proj(j, rows):
        u = _proj(xb[rows], w_ref, 3 + j)
        if j < 2:
            u = _rotary(u, cos[rows], sin[rows])
        return u * DK_RET ** -0.5 if j == 1 else u

    attn_out = ((q1_ref, q4_ref, q16_ref, q_rows, q_by4), (k1_ref, k4_ref, k16_ref, k_rows, k_by4),
                (v1_ref, v4_ref, v16_ref, v_rows, v_by4))

    def attention_proj(j, rows, last):
        nat_ref, by4_ref, by16_ref, rows_scr, by4_scr = attn_out[j]
        val = _proj(xb[rows], w_ref, j)
        _store_rows(val * (HD_ATT ** -0.5 * LOG2E) if j == 0 else val, rows, nat_ref, rows_scr)
        if last:
            _store_gathered(by4_ref, by16_ref, rows_scr, by4_scr)

    halves = (slice(0, half), slice(half, tm))
    ret_in = [[retention_proj(j, halves[0]) for j in range(4)], [None] * 4]
    fills = [[("ret", 0), ("ret", 1), ("ret", 2), ("ret", 3)],
             [("att", 0), ("att", 1), ("att", 2)]]
    decays = [_retention_decays(h) for h in range(H_RET)]
    S = [state[h] for h in range(H_RET)]

    def emit(fill):
        if fill[0] == "ret":
            ret_in[1][fill[1]] = retention_proj(fill[1], halves[1])
        else:
            attention_proj(fill[1], slice(0, tm), last=True)

    for ph in range(2):
        qr, kr, vr, g = ret_in[ph]
        pending = list(fills[ph])
        for c in range(half // RET_CHUNK):
            rows = slice(c * RET_CHUNK, (c + 1) * RET_CHUNK)
            out_rows = slice(ph * half + c * RET_CHUNK, ph * half + (c + 1) * RET_CHUNK)
            staged = []
            for h in range(H_RET):
                cols = slice(h * DK_RET, (h + 1) * DK_RET)
                scores, cross, vb, S[h] = _retention_chunk_matmuls(qr[rows, cols], kr[rows, cols], vr[rows, cols],
                                                                  S[h], decays[h])
                staged.append((scores, cross, vb))
            if pending:
                emit(pending.pop(0))
            for h in range(H_RET):
                cols = slice(h * DK_RET, (h + 1) * DK_RET)
                yret_ref[out_rows, cols] = _retention_chunk_output(*staged[h], g[rows, cols]).astype(BF16)
            if pending:
                emit(pending.pop(0))
        while pending:
            emit(pending.pop(0))
    for h in range(H_RET):
        state[h] = S[h]
        st_ref[h] = S[h]

    @pl.when(pl.program_id(0) >= win_first_step)
    def _():
        for p in range(PAIRS):
            kvw_ref[p * LANES:(p + 1) * LANES, :] = k_rows[p].T
            kvw_ref[W_ATT + p * LANES:W_ATT + (p + 1) * LANES, :] = v_rows[p].T

    @pl.when(pl.program_id(0) == n_steps - 1)
    def _():
        _inproj_sample(*sample_in, w_ref, *sample_out)


def _inproj_sample(x_ref, cos_ref, sin_ref, w_ref, qa_ref, kvw_ref, qr_ref, kr_ref, vr_ref, g_ref):
    xb = x_ref[...].astype(BF16)
    qa_ref[...] = _proj(xb, w_ref, 0) * HD_ATT ** -0.5
    kvw_ref[:, :W_ATT] = _proj(xb, w_ref, 1)
    kvw_ref[:, W_ATT:] = _proj(xb, w_ref, 2)
    qr, kr, vr, g = _retention_qkvg(xb, w_ref, cos_ref[...], sin_ref[...])
    qr_ref[...] = qr
    kr_ref[...] = kr
    vr_ref[...] = vr
    g_ref[...] = g


def _rope_tables(pos):
    half = DK_RET // 2
    inv_freq = 1.0 / (ROPE_BASE ** jnp.linspace(0.0, 1.0, half, dtype=F32))
    ang = pos.astype(F32)[:, None] * inv_freq[None, :]
    cos, sin = jnp.cos(ang), jnp.sin(ang)
    return jnp.concatenate([cos, cos], axis=1), jnp.concatenate([sin, sin], axis=1)


def _inproj_prompt(x, x_sample, w_in, tm):
    S = x.shape[0]
    B = x_sample.shape[0]
    win = min(MAX_WINDOW, S)
    steps = S // tm
    win_first_step = steps - win // tm
    cos, sin = _rope_tables(jnp.concatenate([jnp.arange(tm, dtype=jnp.int32),
                                             jnp.arange(steps, dtype=jnp.int32) * tm]))
    const = lambda i: (0, 0)
    row = lambda i: (i, 0)
    view_specs = [pl.BlockSpec((PAIRS, tm // d, d * LANES), lambda i: (0, i, 0)) for d in (1, 4, 16)] * 3
    view_shapes = [jax.ShapeDtypeStruct((PAIRS, S // d, d * LANES), BF16) for d in (1, 4, 16)] * 3
    state_shape = (H_RET, DK_RET, DV_RET)
    gather_scratch = [pltpu.VMEM((PAIRS, tm, LANES), F32), pltpu.VMEM((4 * PAIRS, tm // 4, LANES), F32)]
    cos_s, sin_s = _rope_tables(jnp.full((B,), PAST_LEN, dtype=jnp.int32))
    sin_s = sin_s * jnp.where(jnp.arange(LANES) < LANES // 2, -1.0, 1.0)
    whole = lambda width: pl.BlockSpec((B, width), const)
    sample_widths = (W_ATT, 2 * W_ATT, W_RET, W_RET, W_RET, W_RET)
    return pl.pallas_call(
        functools.partial(_inproj_prompt_body, win_first_step=win_first_step, n_steps=steps),
        grid=(steps,),
        in_specs=[pl.BlockSpec((tm, D_MODEL), row),
                  pl.BlockSpec((D_MODEL, N_SPLITS * W_ATT), lambda i: (0, 0)),
                  pl.BlockSpec((tm, LANES), const), pl.BlockSpec((tm, LANES), const),
                  pl.BlockSpec((steps, LANES), const), pl.BlockSpec((steps, LANES), const),
                  whole(D_MODEL), whole(LANES), whole(LANES)],
        out_specs=view_specs + [
            pl.BlockSpec((2 * W_ATT, tm), lambda i: (0, jnp.maximum(i - win_first_step, 0))),
            pl.BlockSpec((tm, W_RET), row),
            pl.BlockSpec(state_shape, lambda i: (0, 0, 0))] + [whole(n) for n in sample_widths],
        out_shape=view_shapes + [
            jax.ShapeDtypeStruct((2 * W_ATT, win), F32),
            jax.ShapeDtypeStruct((S, W_RET), BF16),
            jax.ShapeDtypeStruct(state_shape, F32)] + [jax.ShapeDtypeStruct((B, n), F32) for n in sample_widths],
        scratch_shapes=gather_scratch * 3 + [pltpu.VMEM(state_shape, F32),
                                             pltpu.VMEM((D_MODEL, N_SPLITS * W_ATT), BF16)],
        compiler_params=_cparams(1),
        name="inproj_retention_prompt",
    )(x, w_in, cos[:tm], sin[:tm], cos[tm:], sin[tm:], x_sample, cos_s, sin_s)


def _t5_bucket(dist):
    is_small = dist < MAX_EXACT
    d_f = jnp.maximum(dist, 1).astype(F32)
    large = MAX_EXACT + (jnp.log(d_f / MAX_EXACT) / math.log(MAX_WINDOW / MAX_EXACT)
                         * (N_BUCKETS - MAX_EXACT)).astype(jnp.int32)
    large = jnp.minimum(large, N_BUCKETS - 1)
    return jnp.where(is_small, dist, large)


def _tap_bias(rel_bias):
    taps = jnp.arange(TAPS + 1)
    buckets = jnp.stack([_t5_bucket(taps * d) for _, d in DILATIONS])
    return jnp.transpose(rel_bias[buckets].astype(F32), (0, 2, 1))


def _band_bias_rows(tap_bias):
    rows = jnp.concatenate([tap_bias[..., ::-1] * LOG2E,
                            jnp.full(tap_bias.shape[:-1] + (2 * Q_ROWS - TAPS - 1,), NEG, F32)], axis=-1)
    return rows.reshape(len(DILATIONS), PAIRS, 2, 2 * Q_ROWS)


def _window_bias(tap_bias, window):
    rows = []
    for i, (_, d) in enumerate(DILATIONS):
        taps_rev = tap_bias[i, :, TAPS:0:-1]
        seg = jnp.concatenate([taps_rev[..., None], jnp.full((H_ATT, TAPS, d - 1), NEG, F32)], axis=-1)
        rows.append(jnp.concatenate([jnp.full((H_ATT, window - TAPS * d), NEG, F32),
                                     seg.reshape(H_ATT, TAPS * d)], axis=-1))
    return jnp.stack(rows)


def _attn_probs(q, k, bias2, head0_b):
    qs = jnp.concatenate([q * head0_b, q * (1 - head0_b)], axis=0)
    s = lax.dot_general(qs, k, (((1,), (1,)), ((), ())), preferred_element_type=F32) + bias2
    m = jnp.max(s, axis=1, keepdims=True)
    p = jnp.exp2(s - m)
    return p.astype(BF16), jnp.sum(p, axis=1, keepdims=True), m


def _attn_output(p, l, m, v, head0):
    pv = jnp.dot(p, v, preferred_element_type=F32)

    def pair(x):
        tile = (Q_ROWS, LANES)
        return jnp.where(head0, jnp.broadcast_to(x[:Q_ROWS], tile), jnp.broadcast_to(x[Q_ROWS:], tile))

    l_pair = pair(l)
    return pair(pv) / l_pair, pair(m) + jnp.log2(l_pair)


N_ATTN_IN = 16


def _attn_body(*refs, n_cast):
    q1, q4, q16, k1c, k1p, k4c, k4p, k16c, k16p, v1c, v1p, v4c, v4p, v16c, v16p, bias_ref = refs[:N_ATTN_IN]
    cast_in = refs[N_ATTN_IN:N_ATTN_IN + n_cast]
    out_ref, *cast_out = refs[N_ATTN_IN + n_cast:N_ATTN_IN + 2 * n_cast + 1]
    o16, l16, o4, l4, band = refs[N_ATTN_IN + 2 * n_cast + 1:]
    for src, dst in zip(cast_in, cast_out):
        dst[...] = src[...].astype(BF16)
    first = pl.program_id(1) == 0
    lane = lax.broadcasted_iota(jnp.int32, (Q_ROWS, LANES), 1)
    head0 = lane < HD_ATT
    head0_b = jnp.where(head0, 1.0, 0.0).astype(BF16)
    col = lax.broadcasted_iota(jnp.int32, (2 * Q_ROWS, 2 * Q_ROWS), 1)
    no_prev = jnp.where(jnp.logical_and(first, col < Q_ROWS), NEG, 0.0)

    @pl.when(first)
    def _():
        for branch in range(len(DILATIONS)):
            for hh in range(2):
                row = jnp.broadcast_to(bias_ref[branch, hh:hh + 1, :], (Q_ROWS, 2 * Q_ROWS))
                band[branch, hh * Q_ROWS:(hh + 1) * Q_ROWS, :] = pltpu.roll(row, 0, 1, stride=1, stride_axis=0)

    def bias2(branch, masked_prev):
        return band[branch] + no_prev if masked_prev else band[branch]

    def cat(a, b):
        return jnp.concatenate([a, b], axis=0)

    def window(cur, prev, b, sl):
        if b == 0:
            return cat(prev[:, sl], cur[0:Q_ROWS, sl])
        return cur[(b - 1) * Q_ROWS:(b + 1) * Q_ROWS, sl]

    units = []
    b16 = bias2(2, True)
    for r in range(16):
        sl = slice(r * LANES, (r + 1) * LANES)

        def store16(o, l, r=r):
            o16[pl.ds(r, Q_ROWS, stride=16), :] = o
            l16[pl.ds(r, Q_ROWS, stride=16), :] = l

        units.append((lambda sl=sl: q16[:, sl], lambda sl=sl: window(k16c, k16p, 0, sl),
                      lambda sl=sl: window(v16c, v16p, 0, sl), b16, store16))

    b4 = (bias2(1, True), bias2(1, False))
    for r in range(4):
        sl = slice(r * LANES, (r + 1) * LANES)
        for b in range(4):
            rows = slice(b * Q_ROWS, (b + 1) * Q_ROWS)

            def store4(o, l, r=r, b=b):
                o4[pl.ds(b * 4 * Q_ROWS + r, Q_ROWS, stride=4), :] = o
                l4[pl.ds(b * 4 * Q_ROWS + r, Q_ROWS, stride=4), :] = l

            units.append((lambda rows=rows, sl=sl: q4[rows, sl], lambda b=b, sl=sl: window(k4c, k4p, b, sl),
                          lambda b=b, sl=sl: window(v4c, v4p, b, sl), b4[min(b, 1)], store4))

    b1 = (bias2(0, True), bias2(0, False))
    everything = slice(None)
    for b in range(ATT_BLOCK // Q_ROWS):
        rows = slice(b * Q_ROWS, (b + 1) * Q_ROWS)

        def mix(oa, la, rows=rows):
            ob, lb = o4[rows, :], l4[rows, :]
            oc, lc = o16[rows, :], l16[rows, :]
            top = jnp.maximum(jnp.maximum(la, lb), lc)
            ea, eb, ec = jnp.exp2(la - top), jnp.exp2(lb - top), jnp.exp2(lc - top)
            out_ref[rows, :] = ((ea * oa + eb * ob + ec * oc) / (ea + eb + ec)).astype(BF16)

        units.append((lambda rows=rows: q1[rows, :], lambda b=b: window(k1c, k1p, b, everything),
                      lambda b=b: window(v1c, v1p, b, everything), b1[min(b, 1)], mix))

    pending = None
    for q, k, v, bias, finish in units:
        probs = _attn_probs(q(), k(), bias, head0_b)
        if pending is not None:
            p_probs, p_v, p_finish = pending
            p_finish(*_attn_output(*p_probs, p_v(), head0))
        pending = (probs, v, finish)
    p_probs, p_v, p_finish = pending
    p_finish(*_attn_output(*p_probs, p_v(), head0))


def _attn_prompt(q1, q4, q16, k1, k4, k16, v1, v4, v16, band_bias, later_weights):
    S = q1.shape[1]
    assert S % ATT_BLOCK == 0
    steps = S // ATT_BLOCK
    slab_specs = [pl.BlockSpec((w.shape[0] // (PAIRS * steps), w.shape[1]), lambda p, c: (p * steps + c, 0))
                  for w in later_weights]

    def cur(d):
        return pl.BlockSpec((None, ATT_BLOCK // d, d * LANES), lambda p, c: (p, c, 0))

    def prev(d):
        per_step = ATT_BLOCK // d // Q_ROWS
        return pl.BlockSpec((None, Q_ROWS, d * LANES), lambda p, c: (p, jnp.maximum(c * per_step - 1, 0), 0))

    scratch = pltpu.VMEM((ATT_BLOCK, LANES), F32)
    return pl.pallas_call(
        functools.partial(_attn_body, n_cast=len(later_weights)),
        grid=(PAIRS, steps),
        in_specs=[cur(1), cur(4), cur(16),
                  cur(1), prev(1), cur(4), prev(4), cur(16), prev(16),
                  cur(1), prev(1), cur(4), prev(4), cur(16), prev(16),
                  pl.BlockSpec((len(DILATIONS), None, 2, 2 * Q_ROWS), lambda p, c: (0, p, 0, 0))] + slab_specs,
        out_specs=[pl.BlockSpec((ATT_BLOCK, LANES), lambda p, c: (c, p))] + slab_specs,
        out_shape=[jax.ShapeDtypeStruct((S, W_ATT), BF16)]
        + [jax.ShapeDtypeStruct(w.shape, BF16) for w in later_weights],
        scratch_shapes=[scratch, scratch, scratch, scratch,
                        pltpu.VMEM((len(DILATIONS), 2 * Q_ROWS, 2 * Q_ROWS), F32)],
        compiler_params=_cparams(2),
        name="attn_prompt",
    )(q1, q4, q16, k1, k1, k4, k4, k16, k16, v1, v1, v4, v4, v16, v16, band_bias, *later_weights)


def _group_norm_gate(o, g):
    mu = jnp.mean(o, axis=1, keepdims=True)
    var = jnp.mean(jnp.square(o - mu), axis=1, keepdims=True)
    return g * jax.nn.sigmoid(g) * ((o - mu) * lax.rsqrt(var + GN_EPS))


def _retention_decays(h):
    C = RET_CHUNK
    lg = LOG_GAMMA[h]
    n = lax.broadcasted_iota(jnp.int32, (C, 1), 0).astype(F32)
    diff = (lax.broadcasted_iota(jnp.int32, (C, C), 0) - lax.broadcasted_iota(jnp.int32, (C, C), 1)).astype(F32)
    return (jnp.where(diff >= 0, jnp.exp(lg * jnp.maximum(diff, 0.0)), 0.0),
            jnp.exp(lg * (n + 1.0)), jnp.exp(lg * (C - 1.0 - n)), math.exp(lg * C))


def _retention_chunk_matmuls(q, k, v, S, decays):
    decay, q_decay, k_decay, chunk_decay = decays
    qb, vb = q.astype(BF16), v.astype(BF16)
    scores = lax.dot_general(qb, k.astype(BF16), (((1,), (1,)), ((), ())), preferred_element_type=F32) * decay
    cross = jnp.dot(qb, S.astype(BF16), preferred_element_type=F32) * q_decay
    k_dec_t = (k * k_decay).T.astype(BF16)
    return scores.astype(BF16), cross, vb, chunk_decay * S + jnp.dot(k_dec_t, vb, preferred_element_type=F32)


def _retention_chunk_output(scores, cross, vb, g):
    return _group_norm_gate(jnp.dot(scores, vb, preferred_element_type=F32) + cross, g)


def _as_column(row_pair):
    return jnp.broadcast_to(row_pair, (LANES, LANES)).T


def _sample_mixer_body(qa_ref, kvw_ref, qr_ref, kr_ref, vr_ref, g_ref, cache_ref, st_ref, bias_ref, bias0_ref,
                       oatt_ref, yret_ref, stout_ref):
    W = cache_ref.shape[-1]
    row = pl.ds(pl.program_id(0), 1)
    qa, kvw = qa_ref[row, :], kvw_ref[row, :]
    logit_rows, self_rows = [], []
    for p in range(PAIRS):
        lanes = slice(p * LANES, (p + 1) * LANES)
        q_pair = qa[:, lanes]
        k_pair = kvw[:, lanes]
        q_col = _as_column(q_pair)
        for hh in range(2):
            feat = slice(hh * HD_ATT, (hh + 1) * HD_ATT)
            logit_rows.append(jnp.sum(cache_ref[0, 2 * p + hh] * q_col[feat, 0:1], axis=0, keepdims=True))
            self_rows.append(jnp.sum(q_pair[:, feat] * k_pair[:, feat], axis=1, keepdims=True))
    logits = jnp.concatenate(logit_rows, axis=0)
    s_self = jnp.concatenate(self_rows, axis=0) + bias0_ref[:, 0:1]

    probs, p_selfs, denoms, lses = [], [], [], []
    for i, (_, d) in enumerate(DILATIONS):
        lo = W - TAPS * d
        s = logits[:, lo:] + bias_ref[i, :, lo:]
        m = jnp.maximum(jnp.max(s, axis=1, keepdims=True), s_self)
        pr = jnp.exp(s - m)
        p_self = jnp.exp(s_self - m)
        l = jnp.sum(pr, axis=1, keepdims=True) + p_self
        probs.append(pr)
        p_selfs.append(p_self)
        denoms.append(l)
        lses.append(m + jnp.log(l))
    top = jnp.maximum(jnp.maximum(lses[0], lses[1]), lses[2])
    e = [jnp.exp(x - top) for x in lses]
    e_sum = e[0] + e[1] + e[2]
    coef = [e[i] / (e_sum * denoms[i]) for i in range(3)]
    lo4, lo1 = W - TAPS * 4, W - TAPS
    p16, p4, p1 = coef[2] * probs[2], coef[1] * probs[1], coef[0] * probs[0]
    p_all = jnp.concatenate([p16[:, :lo4], p16[:, lo4:lo1] + p4[:, :lo1 - lo4],
                             p16[:, lo1:] + p4[:, lo1 - lo4:] + p1], axis=1)
    c_self = coef[0] * p_selfs[0] + coef[1] * p_selfs[1] + coef[2] * p_selfs[2]
    out_pairs = []
    for p in range(PAIRS):
        lanes = slice(p * LANES, (p + 1) * LANES)
        v_col = _as_column(kvw[:, W_ATT + p * LANES:W_ATT + (p + 1) * LANES])
        out_cols = []
        for hh in range(2):
            h = 2 * p + hh
            feat = slice(hh * HD_ATT, (hh + 1) * HD_ATT)
            pv = jnp.sum(cache_ref[1, h] * p_all[h:h + 1, :], axis=1, keepdims=True)
            out_cols.append(pv + c_self[h:h + 1, :] * v_col[feat, 0:1])
        col = jnp.broadcast_to(jnp.concatenate(out_cols, axis=0), (LANES, LANES))
        out_pairs.append(col.T[0:1, :])
    oatt_ref[row, :] = jnp.concatenate(out_pairs, axis=1)

    qr, kr, vr, g = qr_ref[row, :], kr_ref[row, :], vr_ref[row, :], g_ref[row, :]
    y_heads = []
    for h in range(H_RET):
        cols = slice(h * DK_RET, (h + 1) * DK_RET)
        gamma = math.exp(LOG_GAMMA[h])
        qh, kh, vh = qr[:, cols], kr[:, cols], vr[:, cols]
        S = st_ref[h]
        qk = jnp.sum(qh * kh, axis=1, keepdims=True)
        cross = jnp.dot(jnp.broadcast_to(qh, (BF16_TILE_ROWS, DK_RET)).astype(BF16), S.astype(BF16),
                        preferred_element_type=F32)[:1]
        o = qk * vh + cross * gamma
        stout_ref[h] = gamma * S + _as_column(kh) * vh
        y_heads.append(_group_norm_gate(o, g[:, cols]))
    yret_ref[row, :] = jnp.concatenate(y_heads, axis=1)


def _mixer_operands(qa, kvw, qr, kr, vr, g, cache_t, state, tap_bias):
    B, _, _, _, W = cache_t.shape
    assert W == MAX_WINDOW and PAST_LEN >= MAX_WINDOW
    vec = lambda width: pl.BlockSpec((B, width), lambda b: (0, 0))
    bias0 = jnp.broadcast_to(tap_bias[0, :, 0:1], (H_ATT, LANES))
    token = lambda b: jnp.minimum(b, B - 1)
    st_spec = pl.BlockSpec((None, H_RET, DK_RET, DV_RET), lambda b: (token(b), 0, 0, 0))
    in_specs = [vec(W_ATT), vec(2 * W_ATT), vec(W_RET), vec(W_RET), vec(W_RET), vec(W_RET),
                pl.BlockSpec((None, 2, H_ATT, HD_ATT, W), lambda b: (token(b), 0, 0, 0, 0)),
                st_spec,
                pl.BlockSpec((len(DILATIONS), H_ATT, W), lambda b: (0, 0, 0)),
                pl.BlockSpec((H_ATT, LANES), lambda b: (0, 0))]
    args = (qa, kvw, qr, kr, vr, g, cache_t, state, _window_bias(tap_bias, W), bias0)
    out_specs = [vec(W_ATT), vec(W_RET), st_spec]
    out_shape = [jax.ShapeDtypeStruct((B, W_ATT), F32), jax.ShapeDtypeStruct((B, W_RET), F32),
                 jax.ShapeDtypeStruct(state.shape, F32)]
    return in_specs, args, out_specs, out_shape


FF_CHUNK = 1024
FFN_MIN_ROWS = 128


def _layernorm(z, g, b):
    mu = jnp.mean(z, axis=1, keepdims=True)
    var = jnp.mean(jnp.square(z - mu), axis=1, keepdims=True)
    return (z - mu) * lax.rsqrt(var + LN_EPS) * g + b


def _out_ffn_body(x_ref, oatt_ref, yret_ref, wo_ref, g1_ref, b1_ref, wu_ref, wd_ref, g2_ref, b2_ref, y_ref):
    tm = x_ref.shape[0]
    n_parts = 2 if tm >= 2 * FFN_MIN_ROWS else 1
    parts = [slice(i * (tm // n_parts), (i + 1) * (tm // n_parts)) for i in range(n_parts)]
    mixes = [jnp.dot(oatt_ref[r, :].astype(BF16), wo_ref[:W_ATT, :], preferred_element_type=F32)
             + jnp.dot(yret_ref[r, :].astype(BF16), wo_ref[W_ATT:, :], preferred_element_type=F32) for r in parts]
    n_chunks = D_FF // FF_CHUNK
    for r, mix in zip(parts, mixes):
        x1 = _layernorm(ALPHA * x_ref[r, :] + mix, g1_ref[...], b1_ref[...])
        x1b = x1.astype(BF16)

        def up(j):
            cols = slice(j * FF_CHUNK, (j + 1) * FF_CHUNK)
            return jnp.square(jnp.maximum(jnp.dot(x1b, wu_ref[:, cols], preferred_element_type=F32),
                                          0.0)).astype(BF16)

        ffn = jnp.zeros_like(x1)
        h = up(0)
        for j in range(n_chunks):
            h_next = up(j + 1) if j + 1 < n_chunks else None
            ffn = ffn + jnp.dot(h, wd_ref[j * FF_CHUNK:(j + 1) * FF_CHUNK, :], preferred_element_type=F32)
            h = h_next
        y_ref[r, :] = _layernorm(ALPHA * x1 + ffn, g2_ref[...], b2_ref[...])


def _ffn_operands(x, oatt, yret, wo_b, g1, b1, wu_b, wd_b, g2, b2, tm):
    row = lambda i: (jnp.minimum(i, x.shape[0] // tm - 1), 0)
    const = lambda shape: pl.BlockSpec(shape, lambda i: (0, 0), pipeline_mode=pl.Buffered(1))
    in_specs = [pl.BlockSpec((tm, D_MODEL), row), pl.BlockSpec((tm, W_ATT), row), pl.BlockSpec((tm, W_RET), row),
                const((W_ATT + W_RET, D_MODEL)), const((1, D_MODEL)), const((1, D_MODEL)),
                const((D_MODEL, D_FF)), const((D_FF, D_MODEL)), const((1, D_MODEL)), const((1, D_MODEL))]
    args = (x, oatt, yret, wo_b, g1, b1, wu_b, wd_b, g2, b2)
    return in_specs, args, pl.BlockSpec((tm, D_MODEL), row), jax.ShapeDtypeStruct((x.shape[0], D_MODEL), F32)


N_FFN_ACTIVATIONS = 3


def _ffn_mixer_tail_body(*refs, n_ffn_in, n_mix_in, n_tokens):
    ffn_in, mix_in = refs[:n_ffn_in], refs[n_ffn_in:n_ffn_in + n_mix_in]
    xs_ref = refs[n_ffn_in + n_mix_in]
    y_ref, oatt_s_ref, yret_s_ref, st_ref, ys_ref = refs[n_ffn_in + n_mix_in + 1:]
    step = pl.program_id(0)

    @pl.when(step < n_tokens)
    def _():
        _sample_mixer_body(*mix_in, oatt_s_ref, yret_s_ref, st_ref)
        _out_ffn_body(*ffn_in, y_ref)

    @pl.when(step == n_tokens)
    def _():
        _out_ffn_body(xs_ref, oatt_s_ref, yret_s_ref, *ffn_in[N_FFN_ACTIVATIONS:], ys_ref)


def _out_ffn_prompt_and_sample(ffn_args, mixer_args, x_sample):
    B = mixer_args[0].shape[0]
    tm = ffn_args[0].shape[0] // B
    f_in, f_args, f_out, f_shape = _ffn_operands(*ffn_args, tm)
    m_in, m_args, m_out, m_shape = _mixer_operands(*mixer_args)
    whole = pl.BlockSpec((B, D_MODEL), lambda i: (0, 0))
    y, oatt, yret, st, y_s = pl.pallas_call(
        functools.partial(_ffn_mixer_tail_body, n_ffn_in=len(f_in), n_mix_in=len(m_in), n_tokens=B),
        grid=(B + 1,), in_specs=f_in + m_in + [whole], out_specs=[f_out] + m_out + [whole],
        out_shape=[f_shape] + m_shape + [jax.ShapeDtypeStruct((B, D_MODEL), F32)],
        compiler_params=_cparams(1), name="out_ffn_prompt_sample_mixer",
    )(*f_args, *m_args, x_sample)
    return y, st, y_s


PROMPT_ROWS = 512


def kernel(x_prompt, x_sample, cache_kv_win, state_ret, w_in, rel_bias, w_out,
           ln1_g, ln1_b, w_up, w_down, ln2_g, ln2_b):
    assert x_prompt.shape[0] == 1 and x_sample.shape[1] == 1 and w_in.shape[0] == DEPTH
    B = x_sample.shape[0]
    g1, b1, g2, b2 = ln1_g[0][None], ln1_b[0][None], ln2_g[0][None], ln2_b[0][None]
    tap_bias = _tap_bias(rel_bias)

    xp = x_prompt[0]
    xs = x_sample[:, 0]
    *qkv_views, kvw_p, yret, st_p, qa_s, kvw_s, qr_s, kr_s, vr_s, g_s = _inproj_prompt(xp, xs, w_in[0], PROMPT_ROWS)
    oatt, wo_b, wu_b, wd_b = _attn_prompt(*qkv_views, _band_bias_rows(tap_bias), (w_out[0], w_up[0], w_down[0]))

    cache_t = jnp.transpose(cache_kv_win[0], (0, 2, 3, 4, 1))
    ffn_weights = (wo_b, g1, b1, wu_b, wd_b, g2, b2)
    y_p, st_s, y_s = _out_ffn_prompt_and_sample(
        (xp, oatt, yret) + ffn_weights, (qa_s, kvw_s, qr_s, kr_s, vr_s, g_s, cache_t, state_ret[0], tap_bias), xs)

    win = kvw_p.shape[1]
    kv_win_p = jnp.transpose(kvw_p.reshape(2, H_ATT, HD_ATT, win), (3, 0, 1, 2))
    return (y_p[None], y_s[:, None],
            kv_win_p[None, None], kvw_s.reshape(1, B, 1, 2, H_ATT, HD_ATT),
            st_p[None, None], st_s[None])
```

```python
import functools
import math

import jax
import jax.numpy as jnp
from jax import lax
from jax.experimental import pallas as pl
from jax.experimental.pallas import tpu as pltpu

F32 = jnp.float32
BF16 = jnp.bfloat16

D_MODEL = 1024
DEPTH = 1
PAST_LEN = 16384
W_ATT = 512
HD_ATT = 64
H_ATT = 8
DILATIONS = ((128, 1), (512, 4), (2048, 16))
TAPS = 128
MAX_WINDOW = 2048
N_BUCKETS = 32
MAX_EXACT = N_BUCKETS // 2
W_RET = 512
H_RET = 4
DK_RET = 128
DV_RET = 128
RET_CHUNK = 128
ROPE_BASE = 10000.0
D_FF = 4096
N_SPLITS = 7
ALPHA = (2.0 * DEPTH) ** 0.25
LN_EPS = 1e-5
GN_EPS = 1e-6
LOG_GAMMA = tuple(math.log(1.0 - 2.0 ** (-5.0 - h)) for h in range(H_RET))

LANES = 128
PAIRS = W_ATT // LANES
NEG = -1e30
LOG2E = math.log2(math.e)

ATT_BLOCK = 4096
Q_ROWS = 128
VMEM_LIMIT = 56 * 1024 * 1024


def _cparams(n_axes):
    return pltpu.CompilerParams(dimension_semantics=("arbitrary",) * n_axes, vmem_limit_bytes=VMEM_LIMIT)


def _rotary(u, cos, sin_signed):
    outs = []
    for h in range(H_RET):
        xh = u[:, h * LANES:(h + 1) * LANES]
        outs.append(xh * cos + pltpu.roll(xh, LANES // 2, 1) * sin_signed)
    return jnp.concatenate(outs, axis=1)


def _proj(xb, w_ref, j):
    return jnp.dot(xb, w_ref[:, j * W_ATT:(j + 1) * W_ATT], preferred_element_type=F32)


def _store_rows(val, rows, nat_ref, rows_scr):
    for p in range(PAIRS):
        x = val[:, p * LANES:(p + 1) * LANES]
        nat_ref[p, rows, :] = x.astype(BF16)
        rows_scr[p, rows, :] = x


def _store_gathered(by4_ref, by16_ref, rows_scr, by4_scr):
    tm = rows_scr.shape[1]
    for p in range(PAIRS):
        for r4 in range(4):
            a = rows_scr[p, pl.ds(r4, tm // 4, stride=4), :]
            by4_ref[p, :, r4 * LANES:(r4 + 1) * LANES] = a.astype(BF16)
            by4_scr[p * 4 + r4] = a
    for p in range(PAIRS):
        for r4 in range(4):
            for rr in range(4):
                r16 = r4 + 4 * rr
                b = by4_scr[p * 4 + r4, pl.ds(rr, tm // 16, stride=4), :]
                by16_ref[p, :, r16 * LANES:(r16 + 1) * LANES] = b.astype(BF16)


def _retention_qkvg(xb, w_ref, cos, sin):
    return (_rotary(_proj(xb, w_ref, 3), cos, sin), _rotary(_proj(xb, w_ref, 4), cos, sin) * DK_RET ** -0.5,
            _proj(xb, w_ref, 5), _proj(xb, w_ref, 6))


N_SAMPLE_IN = 3
N_SAMPLE_OUT = 6


def _inproj_prompt_body(*refs, win_first_step, n_steps, n_cast):
    x_ref, w_f32_ref, cos_row_ref, sin_row_ref, cos_step_ref, sin_step_ref = refs[:6]
    sample_in = refs[6:6 + N_SAMPLE_IN]
    cast_in, outs = refs[6 + N_SAMPLE_IN:6 + N_SAMPLE_IN + n_cast], refs[6 + N_SAMPLE_IN + n_cast:]
    q1_ref, q4_ref, q16_ref, k1_ref, k4_ref, k16_ref, v1_ref, v4_ref, v16_ref, kvw_ref, yret_ref, st_ref = outs[:12]
    sample_out = outs[12:12 + N_SAMPLE_OUT]
    cast_out = outs[12 + N_SAMPLE_OUT:12 + N_SAMPLE_OUT + n_cast]
    q_rows, q_by4, k_rows, k_by4, v_rows, v_by4, state, w_ref = outs[12 + N_SAMPLE_OUT + n_cast:]

    @pl.when(pl.program_id(0) == 0)
    def _():
        state[...] = jnp.zeros_like(state)
        for j in range(N_SPLITS):
            cols = slice(j * W_ATT, (j + 1) * W_ATT)
            w_ref[:, cols] = w_f32_ref[:, cols].astype(BF16)

    for src, dst in zip(cast_in, cast_out):
        dst[...] = src[...].astype(BF16)

    xb = x_ref[...].astype(BF16)
    tm = xb.shape[0]
    half = tm // 2

    step = pl.ds(pl.program_id(0), 1)
    ca, sa = cos_step_ref[step, :], sin_step_ref[step, :]
    cb, sb = cos_row_ref[...], sin_row_ref[...]
    sign = jnp.where(lax.broadcasted_iota(jnp.int32, (1, LANES), 1) < LANES // 2, -1.0, 1.0)
    cos, sin = ca * cb - sa * sb, (sa * cb + ca * sb) * sign

    def retention_proj(j, rows):
        u = _proj(xb[rows], w_ref, 3 + j)
        if j < 2:
            u = _rotary(u, cos[rows], sin[rows])
        return u * DK_RET ** -0.5 if j == 1 else u

    attn_out = ((q1_ref, q4_ref, q16_ref, q_rows, q_by4), (k1_ref, k4_ref, k16_ref, k_rows, k_by4),
                (v1_ref, v4_ref, v16_ref, v_rows, v_by4))

    def attention_proj(j, rows, last):
        nat_ref, by4_ref, by16_ref, rows_scr, by4_scr = attn_out[j]
        val = _proj(xb[rows], w_ref, j)
        _store_rows(val * (HD_ATT ** -0.5 * LOG2E) if j == 0 else val, rows, nat_ref, rows_scr)
        if last:
            _store_gathered(by4_ref, by16_ref, rows_scr, by4_scr)

    halves = (slice(0, half), slice(half, tm))
    ret_in = [[retention_proj(j, halves[0]) for j in range(4)], [None] * 4]
    fills = [[("ret", 0), ("ret", 1), ("ret", 2), ("ret", 3)],
             [("att", 0), ("att", 1), ("att", 2)]]
    decays = [_retention_decays(h) for h in range(H_RET)]
    S = [state[h] for h in range(H_RET)]

    def emit(fill):
        if fill[0] == "ret":
            ret_in[1][fill[1]] = retention_proj(fill[1], halves[1])
        else:
            attention_proj(fill[1], slice(0, tm), last=True)

    for ph in range(2):
        qr, kr, vr, g = ret_in[ph]
        pending = list(fills[ph])
        for c in range(half // RET_CHUNK):
            rows = slice(c * RET_CHUNK, (c + 1) * RET_CHUNK)
            out_rows = slice(ph * half + c * RET_CHUNK, ph * half + (c + 1) * RET_CHUNK)
            staged = []
            for h in range(H_RET):
                cols = slice(h * DK_RET, (h + 1) * DK_RET)
                scores, cross, vb, S[h] = _retention_chunk_matmuls(qr[rows, cols], kr[rows, cols], vr[rows, cols],
                                                                  S[h], decays[h])
                staged.append((scores, cross, vb))
            if pending:
                emit(pending.pop(0))
            for h in range(H_RET):
                cols = slice(h * DK_RET, (h + 1) * DK_RET)
                yret_ref[out_rows, cols] = _retention_chunk_output(*staged[h], g[rows, cols]).astype(BF16)
            if pending:
                emit(pending.pop(0))
        while pending:
            emit(pending.pop(0))
    for h in range(H_RET):
        state[h] = S[h]
        st_ref[h] = S[h]

    @pl.when(pl.program_id(0) >= win_first_step)
    def _():
        for p in range(PAIRS):
            kvw_ref[p * LANES:(p + 1) * LANES, :] = k_rows[p].T
            kvw_ref[W_ATT + p * LANES:W_ATT + (p + 1) * LANES, :] = v_rows[p].T

    @pl.when(pl.program_id(0) == n_steps - 1)
    def _():
        _inproj_sample(*sample_in, w_ref, *sample_out)


def _inproj_sample(x_ref, cos_ref, sin_ref, w_ref, qa_ref, kvw_ref, qr_ref, kr_ref, vr_ref, g_ref):
    xb = x_ref[...].astype(BF16)
    qa_ref[...] = _proj(xb, w_ref, 0) * HD_ATT ** -0.5
    kvw_ref[:, :W_ATT] = _proj(xb, w_ref, 1)
    kvw_ref[:, W_ATT:] = _proj(xb, w_ref, 2)
    qr, kr, vr, g = _retention_qkvg(xb, w_ref, cos_ref[...], sin_ref[...])
    qr_ref[...] = qr
    kr_ref[...] = kr
    vr_ref[...] = vr
    g_ref[...] = g


def _rope_tables(pos):
    half = DK_RET // 2
    inv_freq = 1.0 / (ROPE_BASE ** jnp.linspace(0.0, 1.0, half, dtype=F32))
    ang = pos.astype(F32)[:, None] * inv_freq[None, :]
    cos, sin = jnp.cos(ang), jnp.sin(ang)
    return jnp.concatenate([cos, cos], axis=1), jnp.concatenate([sin, sin], axis=1)


def _inproj_prompt(x, x_sample, w_in, later_weights, tm):
    S = x.shape[0]
    B = x_sample.shape[0]
    win = min(MAX_WINDOW, S)
    steps = S // tm
    win_first_step = steps - win // tm
    cos, sin = _rope_tables(jnp.concatenate([jnp.arange(tm, dtype=jnp.int32),
                                             jnp.arange(steps, dtype=jnp.int32) * tm]))
    const = lambda i: (0, 0)
    row = lambda i: (i, 0)
    view_specs = [pl.BlockSpec((PAIRS, tm // d, d * LANES), lambda i: (0, i, 0)) for d in (1, 4, 16)] * 3
    view_shapes = [jax.ShapeDtypeStruct((PAIRS, S // d, d * LANES), BF16) for d in (1, 4, 16)] * 3
    state_shape = (H_RET, DK_RET, DV_RET)
    gather_scratch = [pltpu.VMEM((PAIRS, tm, LANES), F32), pltpu.VMEM((4 * PAIRS, tm // 4, LANES), F32)]
    slab_specs = [pl.BlockSpec((w.shape[0] // steps, w.shape[1]), row) for w in later_weights]
    cos_s, sin_s = _rope_tables(jnp.full((B,), PAST_LEN, dtype=jnp.int32))
    sin_s = sin_s * jnp.where(jnp.arange(LANES) < LANES // 2, -1.0, 1.0)
    whole = lambda width: pl.BlockSpec((B, width), const)
    sample_widths = (W_ATT, 2 * W_ATT, W_RET, W_RET, W_RET, W_RET)
    return pl.pallas_call(
        functools.partial(_inproj_prompt_body, win_first_step=win_first_step, n_steps=steps,
                          n_cast=len(later_weights)),
        grid=(steps,),
        in_specs=[pl.BlockSpec((tm, D_MODEL), row),
                  pl.BlockSpec((D_MODEL, N_SPLITS * W_ATT), lambda i: (0, 0)),
                  pl.BlockSpec((tm, LANES), const), pl.BlockSpec((tm, LANES), const),
                  pl.BlockSpec((steps, LANES), const), pl.BlockSpec((steps, LANES), const),
                  whole(D_MODEL), whole(LANES), whole(LANES)] + slab_specs,
        out_specs=view_specs + [
            pl.BlockSpec((2 * W_ATT, tm), lambda i: (0, jnp.maximum(i - win_first_step, 0))),
            pl.BlockSpec((tm, W_RET), row),
            pl.BlockSpec(state_shape, lambda i: (0, 0, 0))] + [whole(n) for n in sample_widths] + slab_specs,
        out_shape=view_shapes + [
            jax.ShapeDtypeStruct((2 * W_ATT, win), F32),
            jax.ShapeDtypeStruct((S, W_RET), BF16),
            jax.ShapeDtypeStruct(state_shape, F32)] + [jax.ShapeDtypeStruct((B, n), F32) for n in sample_widths]
        + [jax.ShapeDtypeStruct(w.shape, BF16) for w in later_weights],
        scratch_shapes=gather_scratch * 3 + [pltpu.VMEM(state_shape, F32),
                                             pltpu.VMEM((D_MODEL, N_SPLITS * W_ATT), BF16)],
        compiler_params=_cparams(1),
        name="inproj_retention_prompt",
    )(x, w_in, cos[:tm], sin[:tm], cos[tm:], sin[tm:], x_sample, cos_s, sin_s, *later_weights)


def _t5_bucket(dist):
    is_small = dist < MAX_EXACT
    d_f = jnp.maximum(dist, 1).astype(F32)
    large = MAX_EXACT + (jnp.log(d_f / MAX_EXACT) / math.log(MAX_WINDOW / MAX_EXACT)
                         * (N_BUCKETS - MAX_EXACT)).astype(jnp.int32)
    large = jnp.minimum(large, N_BUCKETS - 1)
    return jnp.where(is_small, dist, large)


def _tap_bias(rel_bias):
    taps = jnp.arange(TAPS + 1)
    buckets = jnp.stack([_t5_bucket(taps * d) for _, d in DILATIONS])
    return jnp.transpose(rel_bias[buckets].astype(F32), (0, 2, 1))


def _band_bias_rows(tap_bias):
    rows = jnp.concatenate([tap_bias[..., ::-1] * LOG2E,
                            jnp.full(tap_bias.shape[:-1] + (2 * Q_ROWS - TAPS - 1,), NEG, F32)], axis=-1)
    return rows.reshape(len(DILATIONS), PAIRS, 2, 2 * Q_ROWS)


def _window_bias(tap_bias, window):
    rows = []
    for i, (_, d) in enumerate(DILATIONS):
        taps_rev = tap_bias[i, :, TAPS:0:-1]
        seg = jnp.concatenate([taps_rev[..., None], jnp.full((H_ATT, TAPS, d - 1), NEG, F32)], axis=-1)
        rows.append(jnp.concatenate([jnp.full((H_ATT, window - TAPS * d), NEG, F32),
                                     seg.reshape(H_ATT, TAPS * d)], axis=-1))
    return jnp.stack(rows)


def _attn_probs(q, k, bias2, head0_b):
    qs = jnp.concatenate([q * head0_b, q * (1 - head0_b)], axis=0)
    s = lax.dot_general(qs, k, (((1,), (1,)), ((), ())), preferred_element_type=F32) + bias2
    m = jnp.max(s, axis=1, keepdims=True)
    p = jnp.exp2(s - m)
    return p.astype(BF16), jnp.sum(p, axis=1, keepdims=True), m


def _attn_output(p, l, m, v, head0):
    pv = jnp.dot(p, v, preferred_element_type=F32)

    def pair(x):
        tile = (Q_ROWS, LANES)
        return jnp.where(head0, jnp.broadcast_to(x[:Q_ROWS], tile), jnp.broadcast_to(x[Q_ROWS:], tile))

    l_pair = pair(l)
    return pair(pv) / l_pair, pair(m) + jnp.log2(l_pair)


def _attn_body(q1, q4, q16, k1c, k1p, k4c, k4p, k16c, k16p, v1c, v1p, v4c, v4p, v16c, v16p, bias_ref,
               out_ref, o16, l16, o4, l4, band):
    first = pl.program_id(1) == 0
    lane = lax.broadcasted_iota(jnp.int32, (Q_ROWS, LANES), 1)
    head0 = lane < HD_ATT
    head0_b = jnp.where(head0, 1.0, 0.0).astype(BF16)
    col = lax.broadcasted_iota(jnp.int32, (2 * Q_ROWS, 2 * Q_ROWS), 1)
    no_prev = jnp.where(jnp.logical_and(first, col < Q_ROWS), NEG, 0.0)

    @pl.when(first)
    def _():
        for branch in range(len(DILATIONS)):
            for hh in range(2):
                row = jnp.broadcast_to(bias_ref[branch, hh:hh + 1, :], (Q_ROWS, 2 * Q_ROWS))
                band[branch, hh * Q_ROWS:(hh + 1) * Q_ROWS, :] = pltpu.roll(row, 0, 1, stride=1, stride_axis=0)

    def bias2(branch, masked_prev):
        return band[branch] + no_prev if masked_prev else band[branch]

    def cat(a, b):
        return jnp.concatenate([a, b], axis=0)

    def window(cur, prev, b, sl):
        if b == 0:
            return cat(prev[:, sl], cur[0:Q_ROWS, sl])
        return cur[(b - 1) * Q_ROWS:(b + 1) * Q_ROWS, sl]

    units = []
    b16 = (bias2(2, True), bias2(2, False))
    for r in range(16):
        sl = slice(r * LANES, (r + 1) * LANES)
        for b in range(ATT_BLOCK // (16 * Q_ROWS)):
            rows = slice(b * Q_ROWS, (b + 1) * Q_ROWS)

            def store16(o, l, r=r, b=b):
                o16[pl.ds(b * 16 * Q_ROWS + r, Q_ROWS, stride=16), :] = o
                l16[pl.ds(b * 16 * Q_ROWS + r, Q_ROWS, stride=16), :] = l

            units.append((lambda rows=rows, sl=sl: q16[rows, sl], lambda b=b, sl=sl: window(k16c, k16p, b, sl),
                          lambda b=b, sl=sl: window(v16c, v16p, b, sl), b16[min(b, 1)], store16))

    b4 = (bias2(1, True), bias2(1, False))
    for r in range(4):
        sl = slice(r * LANES, (r + 1) * LANES)
        for b in range(ATT_BLOCK // (4 * Q_ROWS)):
            rows = slice(b * Q_ROWS, (b + 1) * Q_ROWS)

            def store4(o, l, r=r, b=b):
                o4[pl.ds(b * 4 * Q_ROWS + r, Q_ROWS, stride=4), :] = o
                l4[pl.ds(b * 4 * Q_ROWS + r, Q_ROWS, stride=4), :] = l

            units.append((lambda rows=rows, sl=sl: q4[rows, sl], lambda b=b, sl=sl: window(k4c, k4p, b, sl),
                          lambda b=b, sl=sl: window(v4c, v4p, b, sl), b4[min(b, 1)], store4))

    b1 = (bias2(0, True), bias2(0, False))
    everything = slice(None)
    for b in range(ATT_BLOCK // Q_ROWS):
        rows = slice(b * Q_ROWS, (b + 1) * Q_ROWS)

        def mix(oa, la, rows=rows):
            ob, lb = o4[rows, :], l4[rows, :]
            oc, lc = o16[rows, :], l16[rows, :]
            top = jnp.maximum(jnp.maximum(la, lb), lc)
            ea, eb, ec = jnp.exp2(la - top), jnp.exp2(lb - top), jnp.exp2(lc - top)
            out_ref[rows, :] = ((ea * oa + eb * ob + ec * oc) / (ea + eb + ec)).astype(BF16)

        units.append((lambda rows=rows: q1[rows, :], lambda b=b: window(k1c, k1p, b, everything),
                      lambda b=b: window(v1c, v1p, b, everything), b1[min(b, 1)], mix))

    pending = None
    for q, k, v, bias, finish in units:
        probs = _attn_probs(q(), k(), bias, head0_b)
        if pending is not None:
            p_probs, p_v, p_finish = pending
            p_finish(*_attn_output(*p_probs, p_v(), head0))
        pending = (probs, v, finish)
    p_probs, p_v, p_finish = pending
    p_finish(*_attn_output(*p_probs, p_v(), head0))


def _attn_prompt(q1, q4, q16, k1, k4, k16, v1, v4, v16, band_bias):
    S = q1.shape[1]
    assert S % ATT_BLOCK == 0
    steps = S // ATT_BLOCK

    def cur(d):
        return pl.BlockSpec((None, ATT_BLOCK // d, d * LANES), lambda p, c: (p, c, 0))

    def prev(d):
        per_step = ATT_BLOCK // d // Q_ROWS
        return pl.BlockSpec((None, Q_ROWS, d * LANES), lambda p, c: (p, jnp.maximum(c * per_step - 1, 0), 0))

    scratch = pltpu.VMEM((ATT_BLOCK, LANES), F32)
    return pl.pallas_call(
        _attn_body,
        grid=(PAIRS, steps),
        in_specs=[cur(1), cur(4), cur(16),
                  cur(1), prev(1), cur(4), prev(4), cur(16), prev(16),
                  cur(1), prev(1), cur(4), prev(4), cur(16), prev(16),
                  pl.BlockSpec((len(DILATIONS), None, 2, 2 * Q_ROWS), lambda p, c: (0, p, 0, 0))],
        out_specs=pl.BlockSpec((ATT_BLOCK, LANES), lambda p, c: (c, p)),
        out_shape=jax.ShapeDtypeStruct((S, W_ATT), BF16),
        scratch_shapes=[scratch, scratch, scratch, scratch,
                        pltpu.VMEM((len(DILATIONS), 2 * Q_ROWS, 2 * Q_ROWS), F32)],
        compiler_params=_cparams(2),
        name="attn_prompt",
    )(q1, q4, q16, k1, k1, k4, k4, k16, k16, v1, v1, v4, v4, v16, v16, band_bias)


def _group_norm_gate(o, g):
    mu = jnp.mean(o, axis=1, keepdims=True)
    var = jnp.mean(jnp.square(o - mu), axis=1, keepdims=True)
    return g * jax.nn.sigmoid(g) * ((o - mu) * lax.rsqrt(var + GN_EPS))


def _retention_decays(h):
    C = RET_CHUNK
    lg = LOG_GAMMA[h]
    n = lax.broadcasted_iota(jnp.int32, (C, 1), 0).astype(F32)
    diff = (lax.broadcasted_iota(jnp.int32, (C, C), 0) - lax.broadcasted_iota(jnp.int32, (C, C), 1)).astype(F32)
    return (jnp.where(diff >= 0, jnp.exp(lg * jnp.maximum(diff, 0.0)), 0.0),
            jnp.exp(lg * (n + 1.0)), jnp.exp(lg * (C - 1.0 - n)), math.exp(lg * C))


def _retention_chunk_matmuls(q, k, v, S, decays):
    decay, q_decay, k_decay, chunk_decay = decays
    qb, vb = q.astype(BF16), v.astype(BF16)
    scores = lax.dot_general(qb, k.astype(BF16), (((1,), (1,)), ((), ())), preferred_element_type=F32) * decay
    cross = jnp.dot(qb, S.astype(BF16), preferred_element_type=F32) * q_decay
    k_dec_t = (k * k_decay).T.astype(BF16)
    return scores.astype(BF16), cross, vb, chunk_decay * S + jnp.dot(k_dec_t, vb, preferred_element_type=F32)


def _retention_chunk_output(scores, cross, vb, g):
    return _group_norm_gate(jnp.dot(scores, vb, preferred_element_type=F32) + cross, g)


def _as_column(row_pair):
    return jnp.broadcast_to(row_pair, (LANES, LANES)).T


def _sample_mixer_body(qa_ref, kvw_ref, qr_ref, kr_ref, vr_ref, g_ref, cache_ref, st_ref, bias_ref, bias0_ref,
                       oatt_ref, yret_ref, stout_ref):
    W = cache_ref.shape[-1]
    row = pl.ds(pl.program_id(0), 1)
    qa, kvw = qa_ref[row, :], kvw_ref[row, :]
    logit_rows, self_rows = [], []
    for p in range(PAIRS):
        lanes = slice(p * LANES, (p + 1) * LANES)
        q_pair = qa[:, lanes]
        k_pair = kvw[:, lanes]
        q_col = _as_column(q_pair)
        for hh in range(2):
            feat = slice(hh * HD_ATT, (hh + 1) * HD_ATT)
            logit_rows.append(jnp.sum(cache_ref[0, 2 * p + hh] * q_col[feat, 0:1], axis=0, keepdims=True))
            self_rows.append(jnp.sum(q_pair[:, feat] * k_pair[:, feat], axis=1, keepdims=True))
    logits = jnp.concatenate(logit_rows, axis=0)
    s_self = jnp.concatenate(self_rows, axis=0) + bias0_ref[:, 0:1]

    probs, p_selfs, denoms, lses = [], [], [], []
    for i, (_, d) in enumerate(DILATIONS):
        lo = W - TAPS * d
        s = logits[:, lo:] + bias_ref[i, :, lo:]
        m = jnp.maximum(jnp.max(s, axis=1, keepdims=True), s_self)
        pr = jnp.exp(s - m)
        p_self = jnp.exp(s_self - m)
        l = jnp.sum(pr, axis=1, keepdims=True) + p_self
        probs.append(pr)
        p_selfs.append(p_self)
        denoms.append(l)
        lses.append(m + jnp.log(l))
    top = jnp.maximum(jnp.maximum(lses[0], lses[1]), lses[2])
    e = [jnp.exp(x - top) for x in lses]
    e_sum = e[0] + e[1] + e[2]
    coef = [e[i] / (e_sum * denoms[i]) for i in range(3)]
    lo4, lo1 = W - TAPS * 4, W - TAPS
    p16, p4, p1 = coef[2] * probs[2], coef[1] * probs[1], coef[0] * probs[0]
    p_all = jnp.concatenate([p16[:, :lo4], p16[:, lo4:lo1] + p4[:, :lo1 - lo4],
                             p16[:, lo1:] + p4[:, lo1 - lo4:] + p1], axis=1)
    c_self = coef[0] * p_selfs[0] + coef[1] * p_selfs[1] + coef[2] * p_selfs[2]
    out_pairs = []
    for p in range(PAIRS):
        lanes = slice(p * LANES, (p + 1) * LANES)
        v_col = _as_column(kvw[:, W_ATT + p * LANES:W_ATT + (p + 1) * LANES])
        out_cols = []
        for hh in range(2):
            h = 2 * p + hh
            feat = slice(hh * HD_ATT, (hh + 1) * HD_ATT)
            pv = jnp.sum(cache_ref[1, h] * p_all[h:h + 1, :], axis=1, keepdims=True)
            out_cols.append(pv + c_self[h:h + 1, :] * v_col[feat, 0:1])
        col = jnp.broadcast_to(jnp.concatenate(out_cols, axis=0), (LANES, LANES))
        out_pairs.append(col.T[0:1, :])
    oatt_ref[row, :] = jnp.concatenate(out_pairs, axis=1)

    qr, kr, vr, g = qr_ref[row, :], kr_ref[row, :], vr_ref[row, :], g_ref[row, :]
    y_heads = []
    for h in range(H_RET):
        cols = slice(h * DK_RET, (h + 1) * DK_RET)
        gamma = math.exp(LOG_GAMMA[h])
        qh, kh, vh = qr[:, cols], kr[:, cols], vr[:, cols]
        S = st_ref[h]
        qk = jnp.sum(qh * kh, axis=1, keepdims=True)
        cross = jnp.dot(jnp.broadcast_to(qh, (16, DK_RET)).astype(BF16), S.astype(BF16),
                        preferred_element_type=F32)[:1]
        o = qk * vh + cross * gamma
        stout_ref[h] = gamma * S + _as_column(kh) * vh
        y_heads.append(_group_norm_gate(o, g[:, cols]))
    yret_ref[row, :] = jnp.concatenate(y_heads, axis=1)


def _mixer_operands(qa, kvw, qr, kr, vr, g, cache_t, state, tap_bias):
    B, _, _, _, W = cache_t.shape
    assert W == MAX_WINDOW and PAST_LEN >= MAX_WINDOW
    vec = lambda width: pl.BlockSpec((B, width), lambda b: (0, 0))
    bias0 = jnp.broadcast_to(tap_bias[0, :, 0:1], (H_ATT, LANES))
    token = lambda b: jnp.minimum(b, B - 1)
    st_spec = pl.BlockSpec((None, H_RET, DK_RET, DV_RET), lambda b: (token(b), 0, 0, 0))
    in_specs = [vec(W_ATT), vec(2 * W_ATT), vec(W_RET), vec(W_RET), vec(W_RET), vec(W_RET),
                pl.BlockSpec((None, 2, H_ATT, HD_ATT, W), lambda b: (token(b), 0, 0, 0, 0)),
                st_spec,
                pl.BlockSpec((len(DILATIONS), H_ATT, W), lambda b: (0, 0, 0)),
                pl.BlockSpec((H_ATT, LANES), lambda b: (0, 0))]
    args = (qa, kvw, qr, kr, vr, g, cache_t, state, _window_bias(tap_bias, W), bias0)
    out_specs = [vec(W_ATT), vec(W_RET), st_spec]
    out_shape = [jax.ShapeDtypeStruct((B, W_ATT), F32), jax.ShapeDtypeStruct((B, W_RET), F32),
                 jax.ShapeDtypeStruct(state.shape, F32)]
    return in_specs, args, out_specs, out_shape


FF_CHUNK = 1024
FFN_MIN_ROWS = 128


def _layernorm(z, g, b):
    mu = jnp.mean(z, axis=1, keepdims=True)
    var = jnp.mean(jnp.square(z - mu), axis=1, keepdims=True)
    return (z - mu) * lax.rsqrt(var + LN_EPS) * g + b


def _out_ffn_body(x_ref, oatt_ref, yret_ref, wo_ref, g1_ref, b1_ref, wu_ref, wd_ref, g2_ref, b2_ref, y_ref):
    tm = x_ref.shape[0]
    n_parts = 2 if tm >= 2 * FFN_MIN_ROWS else 1
    parts = [slice(i * (tm // n_parts), (i + 1) * (tm // n_parts)) for i in range(n_parts)]
    mixes = [jnp.dot(oatt_ref[r, :].astype(BF16), wo_ref[:W_ATT, :], preferred_element_type=F32)
             + jnp.dot(yret_ref[r, :].astype(BF16), wo_ref[W_ATT:, :], preferred_element_type=F32) for r in parts]
    n_chunks = D_FF // FF_CHUNK
    for r, mix in zip(parts, mixes):
        x1 = _layernorm(ALPHA * x_ref[r, :] + mix, g1_ref[...], b1_ref[...])
        x1b = x1.astype(BF16)

        def up(j):
            cols = slice(j * FF_CHUNK, (j + 1) * FF_CHUNK)
            return jnp.square(jnp.maximum(jnp.dot(x1b, wu_ref[:, cols], preferred_element_type=F32),
                                          0.0)).astype(BF16)

        ffn = jnp.zeros_like(x1)
        h = up(0)
        for j in range(n_chunks):
            h_next = up(j + 1) if j + 1 < n_chunks else None
            ffn = ffn + jnp.dot(h, wd_ref[j * FF_CHUNK:(j + 1) * FF_CHUNK, :], preferred_element_type=F32)
            h = h_next
        y_ref[r, :] = _layernorm(ALPHA * x1 + ffn, g2_ref[...], b2_ref[...])


def _ffn_operands(x, oatt, yret, wo_b, g1, b1, wu_b, wd_b, g2, b2, tm):
    row = lambda i: (jnp.minimum(i, x.shape[0] // tm - 1), 0)
    const = lambda shape: pl.BlockSpec(shape, lambda i: (0, 0), pipeline_mode=pl.Buffered(1))
    in_specs = [pl.BlockSpec((tm, D_MODEL), row), pl.BlockSpec((tm, W_ATT), row), pl.BlockSpec((tm, W_RET), row),
                const((W_ATT + W_RET, D_MODEL)), const((1, D_MODEL)), const((1, D_MODEL)),
                const((D_MODEL, D_FF)), const((D_FF, D_MODEL)), const((1, D_MODEL)), const((1, D_MODEL))]
    args = (x, oatt, yret, wo_b, g1, b1, wu_b, wd_b, g2, b2)
    return in_specs, args, pl.BlockSpec((tm, D_MODEL), row), jax.ShapeDtypeStruct((x.shape[0], D_MODEL), F32)


N_FFN_ACTIVATIONS = 3


def _ffn_mixer_tail_body(*refs, n_ffn_in, n_mix_in, n_tokens):
    ffn_in, mix_in = refs[:n_ffn_in], refs[n_ffn_in:n_ffn_in + n_mix_in]
    xs_ref = refs[n_ffn_in + n_mix_in]
    y_ref, oatt_s_ref, yret_s_ref, st_ref, ys_ref = refs[n_ffn_in + n_mix_in + 1:]
    step = pl.program_id(0)

    @pl.when(step < n_tokens)
    def _():
        _sample_mixer_body(*mix_in, oatt_s_ref, yret_s_ref, st_ref)
        _out_ffn_body(*ffn_in, y_ref)

    @pl.when(step == n_tokens)
    def _():
        _out_ffn_body(xs_ref, oatt_s_ref, yret_s_ref, *ffn_in[N_FFN_ACTIVATIONS:], ys_ref)


def _out_ffn_prompt_and_sample(ffn_args, mixer_args, x_sample):
    B = mixer_args[0].shape[0]
    tm = ffn_args[0].shape[0] // B
    f_in, f_args, f_out, f_shape = _ffn_operands(*ffn_args, tm)
    m_in, m_args, m_out, m_shape = _mixer_operands(*mixer_args)
    whole = pl.BlockSpec((B, D_MODEL), lambda i: (0, 0))
    y, oatt, yret, st, y_s = pl.pallas_call(
        functools.partial(_ffn_mixer_tail_body, n_ffn_in=len(f_in), n_mix_in=len(m_in), n_tokens=B),
        grid=(B + 1,), in_specs=f_in + m_in + [whole], out_specs=[f_out] + m_out + [whole],
        out_shape=[f_shape] + m_shape + [jax.ShapeDtypeStruct((B, D_MODEL), F32)],
        compiler_params=_cparams(1), name="out_ffn_prompt_sample_mixer",
    )(*f_args, *m_args, x_sample)
    return y, st, y_s


PROMPT_ROWS = 512


def kernel(x_prompt, x_sample, cache_kv_win, state_ret, w_in, rel_bias, w_out,
           ln1_g, ln1_b, w_up, w_down, ln2_g, ln2_b):
    assert x_prompt.shape[0] == 1 and x_sample.shape[1] == 1 and w_in.shape[0] == DEPTH
    B = x_sample.shape[0]
    g1, b1, g2, b2 = ln1_g[0][None], ln1_b[0][None], ln2_g[0][None], ln2_b[0][None]
    tap_bias = _tap_bias(rel_bias)

    xp = x_prompt[0]
    xs = x_sample[:, 0]
    *qkv_views, kvw_p, yret, st_p, qa_s, kvw_s, qr_s, kr_s, vr_s, g_s, wo_b, wu_b, wd_b = _inproj_prompt(
        xp, xs, w_in[0], (w_out[0], w_up[0], w_down[0]), PROMPT_ROWS)
    oatt = _attn_prompt(*qkv_views, _band_bias_rows(tap_bias))

    cache_t = jnp.transpose(cache_kv_win[0], (0, 2, 3, 4, 1))
    ffn_weights = (wo_b, g1, b1, wu_b, wd_b, g2, b2)
    y_p, st_s, y_s = _out_ffn_prompt_and_sample(
        (xp, oatt, yret) + ffn_weights, (qa_s, kvw_s, qr_s, kr_s, vr_s, g_s, cache_t, state_ret[0], tap_bias), xs)

    win = kvw_p.shape[1]
    kv_win_p = jnp.transpose(kvw_p.reshape(2, H_ATT, HD_ATT, win), (3, 0, 1, 2))
    return (y_p[None], y_s[:, None],
            kv_win_p[None, None], kvw_s.reshape(1, B, 1, 2, H_ATT, HD_ATT),
            st_p[None, None], st_s[None])
```

```python
import functools
import math

import jax
import jax.numpy as jnp
from jax import lax
from jax.experimental import pallas as pl
from jax.experimental.pallas import tpu as pltpu

F32 = jnp.float32
BF16 = jnp.bfloat16

D_MODEL = 1024
DEPTH = 1
PAST_LEN = 16384
W_ATT = 512
HD_ATT = 64
H_ATT = 8
DILATIONS = ((128, 1), (512, 4), (2048, 16))
TAPS = 128
MAX_WINDOW = 2048
N_BUCKETS = 32
MAX_EXACT = N_BUCKETS // 2
W_RET = 512
H_RET = 4
DK_RET = 128
DV_RET = 128
RET_CHUNK = 128
ROPE_BASE = 10000.0
D_FF = 4096
N_SPLITS = 7
ALPHA = (2.0 * DEPTH) ** 0.25
LN_EPS = 1e-5
GN_EPS = 1e-6
LOG_GAMMA = tuple(math.log(1.0 - 2.0 ** (-5.0 - h)) for h in range(H_RET))

LANES = 128
PAIRS = W_ATT // LANES
NEG = -1e30
LOG2E = math.log2(math.e)

ATT_BLOCK = 4096
Q_ROWS = 128
MIB = 1024 * 1024


def _cparams(n_axes, vmem_mib=56):
    return pltpu.CompilerParams(dimension_semantics=("arbitrary",) * n_axes, vmem_limit_bytes=vmem_mib * MIB)


def _rotary(u, cos, sin_signed):
    outs = []
    for h in range(H_RET):
        xh = u[:, h * LANES:(h + 1) * LANES]
        outs.append(xh * cos + pltpu.roll(xh, LANES // 2, 1) * sin_signed)
    return jnp.concatenate(outs, axis=1)


def _proj(xb, w_ref, j):
    return jnp.dot(xb, w_ref[:, j * W_ATT:(j + 1) * W_ATT], preferred_element_type=F32)


def _store_rows(val, rows, nat_ref, rows_scr):
    for p in range(PAIRS):
        x = val[:, p * LANES:(p + 1) * LANES]
        nat_ref[p, rows, :] = x.astype(BF16)
        rows_scr[p, rows, :] = x


def _store_gathered(by4_ref, by16_ref, rows_scr, by4_scr):
    tm = rows_scr.shape[1]
    for p in range(PAIRS):
        for r4 in range(4):
            a = rows_scr[p, pl.ds(r4, tm // 4, stride=4), :]
            by4_ref[p, :, r4 * LANES:(r4 + 1) * LANES] = a.astype(BF16)
            by4_scr[p * 4 + r4] = a
    for p in range(PAIRS):
        for r4 in range(4):
            for rr in range(4):
                r16 = r4 + 4 * rr
                b = by4_scr[p * 4 + r4, pl.ds(rr, tm // 16, stride=4), :]
                by16_ref[p, :, r16 * LANES:(r16 + 1) * LANES] = b.astype(BF16)


def _retention_qkvg(xb, w_ref, cos, sin):
    return (_rotary(_proj(xb, w_ref, 3), cos, sin), _rotary(_proj(xb, w_ref, 4), cos, sin) * DK_RET ** -0.5,
            _proj(xb, w_ref, 5), _proj(xb, w_ref, 6))


N_SAMPLE_IN = 3
N_SAMPLE_OUT = 6


def _inproj_prompt_body(*refs, win_first_step, n_steps, n_cast):
    x_ref, w_f32_ref, cos_row_ref, sin_row_ref, cos_step_ref, sin_step_ref = refs[:6]
    sample_in = refs[6:6 + N_SAMPLE_IN]
    cast_in, outs = refs[6 + N_SAMPLE_IN:6 + N_SAMPLE_IN + n_cast], refs[6 + N_SAMPLE_IN + n_cast:]
    q1_ref, q4_ref, q16_ref, k1_ref, k4_ref, k16_ref, v1_ref, v4_ref, v16_ref, kvw_ref, yret_ref, st_ref = outs[:12]
    sample_out = outs[12:12 + N_SAMPLE_OUT]
    cast_out = outs[12 + N_SAMPLE_OUT:12 + N_SAMPLE_OUT + n_cast]
    q_rows, q_by4, k_rows, k_by4, v_rows, v_by4, state, w_ref = outs[12 + N_SAMPLE_OUT + n_cast:]

    @pl.when(pl.program_id(0) == 0)
    def _():
        state[...] = jnp.zeros_like(state)
        for j in range(N_SPLITS):
            cols = slice(j * W_ATT, (j + 1) * W_ATT)
            w_ref[:, cols] = w_f32_ref[:, cols].astype(BF16)

    for src, dst in zip(cast_in, cast_out):
        dst[...] = src[...].astype(BF16)

    xb = x_ref[...].astype(BF16)
    tm = xb.shape[0]
    half = tm // 2

    step = pl.ds(pl.program_id(0), 1)
    ca, sa = cos_step_ref[step, :], sin_step_ref[step, :]
    cb, sb = cos_row_ref[...], sin_row_ref[...]
    sign = jnp.where(lax.broadcasted_iota(jnp.int32, (1, LANES), 1) < LANES // 2, -1.0, 1.0)
    cos, sin = ca * cb - sa * sb, (sa * cb + ca * sb) * sign

    def retention_proj(j, rows):
        u = _proj(xb[rows], w_ref, 3 + j)
        if j < 2:
            u = _rotary(u, cos[rows], sin[rows])
        return u * DK_RET ** -0.5 if j == 1 else u

    attn_out = ((q1_ref, q4_ref, q16_ref, q_rows, q_by4), (k1_ref, k4_ref, k16_ref, k_rows, k_by4),
                (v1_ref, v4_ref, v16_ref, v_rows, v_by4))

    def attention_proj(j, rows, last):
        nat_ref, by4_ref, by16_ref, rows_scr, by4_scr = attn_out[j]
        val = _proj(xb[rows], w_ref, j)
        _store_rows(val * (HD_ATT ** -0.5 * LOG2E) if j == 0 else val, rows, nat_ref, rows_scr)
        if last:
            _store_gathered(by4_ref, by16_ref, rows_scr, by4_scr)

    halves = (slice(0, half), slice(half, tm))
    ret_in = [[retention_proj(j, halves[0]) for j in range(4)], [None] * 4]
    fills = [[("ret", 0), ("ret", 1), ("ret", 2), ("ret", 3)],
             [("att", 0), ("att", 1), ("att", 2)]]
    decays = [_retention_decays(h) for h in range(H_RET)]
    S = [state[h] for h in range(H_RET)]

    def emit(fill):
        if fill[0] == "ret":
            ret_in[1][fill[1]] = retention_proj(fill[1], halves[1])
        else:
            attention_proj(fill[1], slice(0, tm), last=True)

    for ph in range(2):
        qr, kr, vr, g = ret_in[ph]
        pending = list(fills[ph])
        for c in range(half // RET_CHUNK):
            rows = slice(c * RET_CHUNK, (c + 1) * RET_CHUNK)
            out_rows = slice(ph * half + c * RET_CHUNK, ph * half + (c + 1) * RET_CHUNK)
            staged = []
            for h in range(H_RET):
                cols = slice(h * DK_RET, (h + 1) * DK_RET)
                scores, cross, vb, S[h] = _retention_chunk_matmuls(qr[rows, cols], kr[rows, cols], vr[rows, cols],
                                                                  S[h], decays[h])
                staged.append((scores, cross, vb))
            if pending:
                emit(pending.pop(0))
            for h in range(H_RET):
                cols = slice(h * DK_RET, (h + 1) * DK_RET)
                yret_ref[out_rows, cols] = _retention_chunk_output(*staged[h], g[rows, cols]).astype(BF16)
            if pending:
                emit(pending.pop(0))
        while pending:
            emit(pending.pop(0))
    for h in range(H_RET):
        state[h] = S[h]
        st_ref[h] = S[h]

    @pl.when(pl.program_id(0) >= win_first_step)
    def _():
        for p in range(PAIRS):
            kvw_ref[p * LANES:(p + 1) * LANES, :] = k_rows[p].T
            kvw_ref[W_ATT + p * LANES:W_ATT + (p + 1) * LANES, :] = v_rows[p].T

    @pl.when(pl.program_id(0) == n_steps - 1)
    def _():
        _inproj_sample(*sample_in, w_ref, *sample_out)


def _inproj_sample(x_ref, cos_ref, sin_ref, w_ref, qa_ref, kvw_ref, qr_ref, kr_ref, vr_ref, g_ref):
    xb = x_ref[...].astype(BF16)
    qa_ref[...] = _proj(xb, w_ref, 0) * HD_ATT ** -0.5
    kvw_ref[:, :W_ATT] = _proj(xb, w_ref, 1)
    kvw_ref[:, W_ATT:] = _proj(xb, w_ref, 2)
    qr, kr, vr, g = _retention_qkvg(xb, w_ref, cos_ref[...], sin_ref[...])
    qr_ref[...] = qr
    kr_ref[...] = kr
    vr_ref[...] = vr
    g_ref[...] = g


def _rope_tables(pos):
    half = DK_RET // 2
    inv_freq = 1.0 / (ROPE_BASE ** jnp.linspace(0.0, 1.0, half, dtype=F32))
    ang = pos.astype(F32)[:, None] * inv_freq[None, :]
    cos, sin = jnp.cos(ang), jnp.sin(ang)
    return jnp.concatenate([cos, cos], axis=1), jnp.concatenate([sin, sin], axis=1)


def _inproj_prompt(x, x_sample, w_in, later_weights, tm):
    S = x.shape[0]
    B = x_sample.shape[0]
    win = min(MAX_WINDOW, S)
    steps = S // tm
    win_first_step = steps - win // tm
    cos, sin = _rope_tables(jnp.concatenate([jnp.arange(tm, dtype=jnp.int32),
                                             jnp.arange(steps, dtype=jnp.int32) * tm]))
    const = lambda i: (0, 0)
    row = lambda i: (i, 0)
    view_specs = [pl.BlockSpec((PAIRS, tm // d, d * LANES), lambda i: (0, i, 0)) for d in (1, 4, 16)] * 3
    view_shapes = [jax.ShapeDtypeStruct((PAIRS, S // d, d * LANES), BF16) for d in (1, 4, 16)] * 3
    state_shape = (H_RET, DK_RET, DV_RET)
    gather_scratch = [pltpu.VMEM((PAIRS, tm, LANES), F32), pltpu.VMEM((4 * PAIRS, tm // 4, LANES), F32)]
    slab_specs = [pl.BlockSpec((w.shape[0] // steps, w.shape[1]), row) for w in later_weights]
    cos_s, sin_s = _rope_tables(jnp.full((B,), PAST_LEN, dtype=jnp.int32))
    sin_s = sin_s * jnp.where(jnp.arange(LANES) < LANES // 2, -1.0, 1.0)
    whole = lambda width: pl.BlockSpec((B, width), const)
    sample_widths = (W_ATT, 2 * W_ATT, W_RET, W_RET, W_RET, W_RET)
    return pl.pallas_call(
        functools.partial(_inproj_prompt_body, win_first_step=win_first_step, n_steps=steps,
                          n_cast=len(later_weights)),
        grid=(steps,),
        in_specs=[pl.BlockSpec((tm, D_MODEL), row),
                  pl.BlockSpec((D_MODEL, N_SPLITS * W_ATT), lambda i: (0, 0), pipeline_mode=pl.Buffered(1)),
                  pl.BlockSpec((tm, LANES), const), pl.BlockSpec((tm, LANES), const),
                  pl.BlockSpec((steps, LANES), const), pl.BlockSpec((steps, LANES), const),
                  whole(D_MODEL), whole(LANES), whole(LANES)] + slab_specs,
        out_specs=view_specs + [
            pl.BlockSpec((2 * W_ATT, tm), lambda i: (0, jnp.maximum(i - win_first_step, 0))),
            pl.BlockSpec((tm, W_RET), row),
            pl.BlockSpec(state_shape, lambda i: (0, 0, 0))] + [whole(n) for n in sample_widths] + slab_specs,
        out_shape=view_shapes + [
            jax.ShapeDtypeStruct((2 * W_ATT, win), F32),
            jax.ShapeDtypeStruct((S, W_RET), BF16),
            jax.ShapeDtypeStruct(state_shape, F32)] + [jax.ShapeDtypeStruct((B, n), F32) for n in sample_widths]
        + [jax.ShapeDtypeStruct(w.shape, BF16) for w in later_weights],
        scratch_shapes=gather_scratch * 3 + [pltpu.VMEM(state_shape, F32),
                                             pltpu.VMEM((D_MODEL, N_SPLITS * W_ATT), BF16)],
        compiler_params=_cparams(1),
        name="inproj_retention_prompt",
    )(x, w_in, cos[:tm], sin[:tm], cos[tm:], sin[tm:], x_sample, cos_s, sin_s, *later_weights)


def _t5_bucket(dist):
    is_small = dist < MAX_EXACT
    d_f = jnp.maximum(dist, 1).astype(F32)
    large = MAX_EXACT + (jnp.log(d_f / MAX_EXACT) / math.log(MAX_WINDOW / MAX_EXACT)
                         * (N_BUCKETS - MAX_EXACT)).astype(jnp.int32)
    large = jnp.minimum(large, N_BUCKETS - 1)
    return jnp.where(is_small, dist, large)


def _tap_bias(rel_bias):
    taps = jnp.arange(TAPS + 1)
    buckets = jnp.stack([_t5_bucket(taps * d) for _, d in DILATIONS])
    return jnp.transpose(rel_bias[buckets].astype(F32), (0, 2, 1))


def _band_bias_rows(tap_bias):
    rows = jnp.concatenate([tap_bias[..., ::-1] * LOG2E,
                            jnp.full(tap_bias.shape[:-1] + (2 * Q_ROWS - TAPS - 1,), NEG, F32)], axis=-1)
    return rows.reshape(len(DILATIONS), PAIRS, 2, 2 * Q_ROWS)


def _window_bias(tap_bias, window):
    rows = []
    for i, (_, d) in enumerate(DILATIONS):
        taps_rev = tap_bias[i, :, TAPS:0:-1]
        seg = jnp.concatenate([taps_rev[..., None], jnp.full((H_ATT, TAPS, d - 1), NEG, F32)], axis=-1)
        rows.append(jnp.concatenate([jnp.full((H_ATT, window - TAPS * d), NEG, F32),
                                     seg.reshape(H_ATT, TAPS * d)], axis=-1))
    return jnp.stack(rows)


def _attn_probs(q, k, bias2, head0_b):
    qs = jnp.concatenate([q * head0_b, q * (1 - head0_b)], axis=0)
    s = lax.dot_general(qs, k, (((1,), (1,)), ((), ())), preferred_element_type=F32) + bias2
    m = jnp.max(s, axis=1, keepdims=True)
    p = jnp.exp2(s - m)
    return p.astype(BF16), jnp.sum(p, axis=1, keepdims=True), m


def _attn_output(p, l, m, v, head0):
    pv = jnp.dot(p, v, preferred_element_type=F32)

    def pair(x):
        tile = (Q_ROWS, LANES)
        return jnp.where(head0, jnp.broadcast_to(x[:Q_ROWS], tile), jnp.broadcast_to(x[Q_ROWS:], tile))

    l_pair = pair(l)
    return pair(pv) / l_pair, pair(m) + jnp.log2(l_pair)


def _attn_body(q1, q4, q16, k1c, k1p, k4c, k4p, k16c, k16p, v1c, v1p, v4c, v4p, v16c, v16p, bias_ref,
               out_ref, o16, l16, o4, l4, band):
    first = pl.program_id(1) == 0
    lane = lax.broadcasted_iota(jnp.int32, (Q_ROWS, LANES), 1)
    head0 = lane < HD_ATT
    head0_b = jnp.where(head0, 1.0, 0.0).astype(BF16)
    col = lax.broadcasted_iota(jnp.int32, (2 * Q_ROWS, 2 * Q_ROWS), 1)
    no_prev = jnp.where(jnp.logical_and(first, col < Q_ROWS), NEG, 0.0)

    @pl.when(first)
    def _():
        for branch in range(len(DILATIONS)):
            for hh in range(2):
                row = jnp.broadcast_to(bias_ref[branch, hh:hh + 1, :], (Q_ROWS, 2 * Q_ROWS))
                band[branch, hh * Q_ROWS:(hh + 1) * Q_ROWS, :] = pltpu.roll(row, 0, 1, stride=1, stride_axis=0)

    def bias2(branch, masked_prev):
        return band[branch] + no_prev if masked_prev else band[branch]

    def cat(a, b):
        return jnp.concatenate([a, b], axis=0)

    def window(cur, prev, b, sl):
        if b == 0:
            return cat(prev[:, sl], cur[0:Q_ROWS, sl])
        return cur[(b - 1) * Q_ROWS:(b + 1) * Q_ROWS, sl]

    units = []
    b16 = (bias2(2, True), bias2(2, False))
    for r in range(16):
        sl = slice(r * LANES, (r + 1) * LANES)
        for b in range(ATT_BLOCK // (16 * Q_ROWS)):
            rows = slice(b * Q_ROWS, (b + 1) * Q_ROWS)

            def store16(o, l, r=r, b=b):
                o16[pl.ds(b * 16 * Q_ROWS + r, Q_ROWS, stride=16), :] = o
                l16[pl.ds(b * 16 * Q_ROWS + r, Q_ROWS, stride=16), :] = l

            units.append((lambda rows=rows, sl=sl: q16[rows, sl], lambda b=b, sl=sl: window(k16c, k16p, b, sl),
                          lambda b=b, sl=sl: window(v16c, v16p, b, sl), b16[min(b, 1)], store16))

    b4 = (bias2(1, True), bias2(1, False))
    for r in range(4):
        sl = slice(r * LANES, (r + 1) * LANES)
        for b in range(ATT_BLOCK // (4 * Q_ROWS)):
            rows = slice(b * Q_ROWS, (b + 1) * Q_ROWS)

            def store4(o, l, r=r, b=b):
                o4[pl.ds(b * 4 * Q_ROWS + r, Q_ROWS, stride=4), :] = o
                l4[pl.ds(b * 4 * Q_ROWS + r, Q_ROWS, stride=4), :] = l

            units.append((lambda rows=rows, sl=sl: q4[rows, sl], lambda b=b, sl=sl: window(k4c, k4p, b, sl),
                          lambda b=b, sl=sl: window(v4c, v4p, b, sl), b4[min(b, 1)], store4))

    b1 = (bias2(0, True), bias2(0, False))
    everything = slice(None)
    for b in range(ATT_BLOCK // Q_ROWS):
        rows = slice(b * Q_ROWS, (b + 1) * Q_ROWS)

        def mix(oa, la, rows=rows):
            ob, lb = o4[rows, :], l4[rows, :]
            oc, lc = o16[rows, :], l16[rows, :]
            top = jnp.maximum(jnp.maximum(la, lb), lc)
            ea, eb, ec = jnp.exp2(la - top), jnp.exp2(lb - top), jnp.exp2(lc - top)
            out_ref[rows, :] = ((ea * oa + eb * ob + ec * oc) / (ea + eb + ec)).astype(BF16)

        units.append((lambda rows=rows: q1[rows, :], lambda b=b: window(k1c, k1p, b, everything),
                      lambda b=b: window(v1c, v1p, b, everything), b1[min(b, 1)], mix))

    pending = None
    for q, k, v, bias, finish in units:
        probs = _attn_probs(q(), k(), bias, head0_b)
        if pending is not None:
            p_probs, p_v, p_finish = pending
            p_finish(*_attn_output(*p_probs, p_v(), head0))
        pending = (probs, v, finish)
    p_probs, p_v, p_finish = pending
    p_finish(*_attn_output(*p_probs, p_v(), head0))


def _attn_prompt(q1, q4, q16, k1, k4, k16, v1, v4, v16, band_bias):
    S = q1.shape[1]
    assert S % ATT_BLOCK == 0
    steps = S // ATT_BLOCK

    def cur(d):
        return pl.BlockSpec((None, ATT_BLOCK // d, d * LANES), lambda p, c: (p, c, 0))

    def prev(d):
        per_step = ATT_BLOCK // d // Q_ROWS
        return pl.BlockSpec((None, Q_ROWS, d * LANES), lambda p, c: (p, jnp.maximum(c * per_step - 1, 0), 0))

    scratch = pltpu.VMEM((ATT_BLOCK, LANES), F32)
    return pl.pallas_call(
        _attn_body,
        grid=(PAIRS, steps),
        in_specs=[cur(1), cur(4), cur(16),
                  cur(1), prev(1), cur(4), prev(4), cur(16), prev(16),
                  cur(1), prev(1), cur(4), prev(4), cur(16), prev(16),
                  pl.BlockSpec((len(DILATIONS), None, 2, 2 * Q_ROWS), lambda p, c: (0, p, 0, 0))],
        out_specs=pl.BlockSpec((ATT_BLOCK, LANES), lambda p, c: (c, p)),
        out_shape=jax.ShapeDtypeStruct((S, W_ATT), BF16),
        scratch_shapes=[scratch, scratch, scratch, scratch,
                        pltpu.VMEM((len(DILATIONS), 2 * Q_ROWS, 2 * Q_ROWS), F32)],
        compiler_params=_cparams(2, vmem_mib=44),
        name="attn_prompt",
    )(q1, q4, q16, k1, k1, k4, k4, k16, k16, v1, v1, v4, v4, v16, v16, band_bias)


def _group_norm_gate(o, g):
    mu = jnp.mean(o, axis=1, keepdims=True)
    var = jnp.mean(jnp.square(o - mu), axis=1, keepdims=True)
    return g * jax.nn.sigmoid(g) * ((o - mu) * lax.rsqrt(var + GN_EPS))


def _retention_decays(h):
    C = RET_CHUNK
    lg = LOG_GAMMA[h]
    n = lax.broadcasted_iota(jnp.int32, (C, 1), 0).astype(F32)
    diff = (lax.broadcasted_iota(jnp.int32, (C, C), 0) - lax.broadcasted_iota(jnp.int32, (C, C), 1)).astype(F32)
    return (jnp.where(diff >= 0, jnp.exp(lg * jnp.maximum(diff, 0.0)), 0.0),
            jnp.exp(lg * (n + 1.0)), jnp.exp(lg * (C - 1.0 - n)), math.exp(lg * C))


def _retention_chunk_matmuls(q, k, v, S, decays):
    decay, q_decay, k_decay, chunk_decay = decays
    qb, vb = q.astype(BF16), v.astype(BF16)
    scores = lax.dot_general(qb, k.astype(BF16), (((1,), (1,)), ((), ())), preferred_element_type=F32) * decay
    cross = jnp.dot(qb, S.astype(BF16), preferred_element_type=F32) * q_decay
    k_dec_t = (k * k_decay).T.astype(BF16)
    return scores.astype(BF16), cross, vb, chunk_decay * S + jnp.dot(k_dec_t, vb, preferred_element_type=F32)


def _retention_chunk_output(scores, cross, vb, g):
    return _group_norm_gate(jnp.dot(scores, vb, preferred_element_type=F32) + cross, g)


def _as_column(row_pair):
    return jnp.broadcast_to(row_pair, (LANES, LANES)).T


def _sample_mixer_body(qa_ref, kvw_ref, qr_ref, kr_ref, vr_ref, g_ref, cache_ref, st_ref, bias_ref, bias0_ref,
                       oatt_ref, yret_ref, stout_ref):
    W = cache_ref.shape[-1]
    row = pl.ds(pl.program_id(0), 1)
    qa, kvw = qa_ref[row, :], kvw_ref[row, :]
    logit_rows, self_rows = [], []
    for p in range(PAIRS):
        lanes = slice(p * LANES, (p + 1) * LANES)
        q_pair = qa[:, lanes]
        k_pair = kvw[:, lanes]
        q_col = _as_column(q_pair)
        for hh in range(2):
            feat = slice(hh * HD_ATT, (hh + 1) * HD_ATT)
            logit_rows.append(jnp.sum(cache_ref[0, 2 * p + hh] * q_col[feat, 0:1], axis=0, keepdims=True))
            self_rows.append(jnp.sum(q_pair[:, feat] * k_pair[:, feat], axis=1, keepdims=True))
    logits = jnp.concatenate(logit_rows, axis=0)
    s_self = jnp.concatenate(self_rows, axis=0) + bias0_ref[:, 0:1]

    probs, p_selfs, denoms, lses = [], [], [], []
    for i, (_, d) in enumerate(DILATIONS):
        lo = W - TAPS * d
        s = logits[:, lo:] + bias_ref[i, :, lo:]
        m = jnp.maximum(jnp.max(s, axis=1, keepdims=True), s_self)
        pr = jnp.exp(s - m)
        p_self = jnp.exp(s_self - m)
        l = jnp.sum(pr, axis=1, keepdims=True) + p_self
        probs.append(pr)
        p_selfs.append(p_self)
        denoms.append(l)
        lses.append(m + jnp.log(l))
    top = jnp.maximum(jnp.maximum(lses[0], lses[1]), lses[2])
    e = [jnp.exp(x - top) for x in lses]
    e_sum = e[0] + e[1] + e[2]
    coef = [e[i] / (e_sum * denoms[i]) for i in range(3)]
    lo4, lo1 = W - TAPS * 4, W - TAPS
    p16, p4, p1 = coef[2] * probs[2], coef[1] * probs[1], coef[0] * probs[0]
    p_all = jnp.concatenate([p16[:, :lo4], p16[:, lo4:lo1] + p4[:, :lo1 - lo4],
                             p16[:, lo1:] + p4[:, lo1 - lo4:] + p1], axis=1)
    c_self = coef[0] * p_selfs[0] + coef[1] * p_selfs[1] + coef[2] * p_selfs[2]
    out_pairs = []
    for p in range(PAIRS):
        lanes = slice(p * LANES, (p + 1) * LANES)
        v_col = _as_column(kvw[:, W_ATT + p * LANES:W_ATT + (p + 1) * LANES])
        out_cols = []
        for hh in range(2):
            h = 2 * p + hh
            feat = slice(hh * HD_ATT, (hh + 1) * HD_ATT)
            pv = jnp.sum(cache_ref[1, h] * p_all[h:h + 1, :], axis=1, keepdims=True)
            out_cols.append(pv + c_self[h:h + 1, :] * v_col[feat, 0:1])
        col = jnp.broadcast_to(jnp.concatenate(out_cols, axis=0), (LANES, LANES))
        out_pairs.append(col.T[0:1, :])
    oatt_ref[row, :] = jnp.concatenate(out_pairs, axis=1)

    qr, kr, vr, g = qr_ref[row, :], kr_ref[row, :], vr_ref[row, :], g_ref[row, :]
    y_heads = []
    for h in range(H_RET):
        cols = slice(h * DK_RET, (h + 1) * DK_RET)
        gamma = math.exp(LOG_GAMMA[h])
        qh, kh, vh = qr[:, cols], kr[:, cols], vr[:, cols]
        S = st_ref[h]
        qk = jnp.sum(qh * kh, axis=1, keepdims=True)
        cross = jnp.dot(jnp.broadcast_to(qh, (16, DK_RET)).astype(BF16), S.astype(BF16),
                        preferred_element_type=F32)[:1]
        o = qk * vh + cross * gamma
        stout_ref[h] = gamma * S + _as_column(kh) * vh
        y_heads.append(_group_norm_gate(o, g[:, cols]))
    yret_ref[row, :] = jnp.concatenate(y_heads, axis=1)


def _mixer_operands(qa, kvw, qr, kr, vr, g, cache_t, state, tap_bias):
    B, _, _, _, W = cache_t.shape
    assert W == MAX_WINDOW and PAST_LEN >= MAX_WINDOW
    vec = lambda width: pl.BlockSpec((B, width), lambda b: (0, 0))
    bias0 = jnp.broadcast_to(tap_bias[0, :, 0:1], (H_ATT, LANES))
    token = lambda b: jnp.minimum(b, B - 1)
    st_spec = pl.BlockSpec((None, H_RET, DK_RET, DV_RET), lambda b: (token(b), 0, 0, 0))
    in_specs = [vec(W_ATT), vec(2 * W_ATT), vec(W_RET), vec(W_RET), vec(W_RET), vec(W_RET),
                pl.BlockSpec((None, 2, H_ATT, HD_ATT, W), lambda b: (token(b), 0, 0, 0, 0)),
                st_spec,
                pl.BlockSpec((len(DILATIONS), H_ATT, W), lambda b: (0, 0, 0)),
                pl.BlockSpec((H_ATT, LANES), lambda b: (0, 0))]
    args = (qa, kvw, qr, kr, vr, g, cache_t, state, _window_bias(tap_bias, W), bias0)
    out_specs = [vec(W_ATT), vec(W_RET), st_spec]
    out_shape = [jax.ShapeDtypeStruct((B, W_ATT), F32), jax.ShapeDtypeStruct((B, W_RET), F32),
                 jax.ShapeDtypeStruct(state.shape, F32)]
    return in_specs, args, out_specs, out_shape


FF_CHUNK = 1024
FFN_MIN_ROWS = 128


def _layernorm(z, g, b):
    mu = jnp.mean(z, axis=1, keepdims=True)
    var = jnp.mean(jnp.square(z - mu), axis=1, keepdims=True)
    return (z - mu) * lax.rsqrt(var + LN_EPS) * g + b


def _out_ffn_body(x_ref, oatt_ref, yret_ref, wo_ref, g1_ref, b1_ref, wu_ref, wd_ref, g2_ref, b2_ref, y_ref):
    tm = x_ref.shape[0]
    n_parts = 2 if tm >= 2 * FFN_MIN_ROWS else 1
    parts = [slice(i * (tm // n_parts), (i + 1) * (tm // n_parts)) for i in range(n_parts)]
    mixes = [jnp.dot(oatt_ref[r, :].astype(BF16), wo_ref[:W_ATT, :], preferred_element_type=F32)
             + jnp.dot(yret_ref[r, :].astype(BF16), wo_ref[W_ATT:, :], preferred_element_type=F32) for r in parts]
    n_chunks = D_FF // FF_CHUNK
    for r, mix in zip(parts, mixes):
        x1 = _layernorm(ALPHA * x_ref[r, :] + mix, g1_ref[...], b1_ref[...])
        x1b = x1.astype(BF16)

        def up(j):
            cols = slice(j * FF_CHUNK, (j + 1) * FF_CHUNK)
            return jnp.square(jnp.maximum(jnp.dot(x1b, wu_ref[:, cols], preferred_element_type=F32),
                                          0.0)).astype(BF16)

        ffn = jnp.zeros_like(x1)
        h = up(0)
        for j in range(n_chunks):
            h_next = up(j + 1) if j + 1 < n_chunks else None
            ffn = ffn + jnp.dot(h, wd_ref[j * FF_CHUNK:(j + 1) * FF_CHUNK, :], preferred_element_type=F32)
            h = h_next
        y_ref[r, :] = _layernorm(ALPHA * x1 + ffn, g2_ref[...], b2_ref[...])


def _ffn_operands(x, oatt, yret, wo_b, g1, b1, wu_b, wd_b, g2, b2, tm):
    row = lambda i: (jnp.minimum(i, x.shape[0] // tm - 1), 0)
    const = lambda shape: pl.BlockSpec(shape, lambda i: (0, 0), pipeline_mode=pl.Buffered(1))
    in_specs = [pl.BlockSpec((tm, D_MODEL), row), pl.BlockSpec((tm, W_ATT), row), pl.BlockSpec((tm, W_RET), row),
                const((W_ATT + W_RET, D_MODEL)), const((1, D_MODEL)), const((1, D_MODEL)),
                const((D_MODEL, D_FF)), const((D_FF, D_MODEL)), const((1, D_MODEL)), const((1, D_MODEL))]
    args = (x, oatt, yret, wo_b, g1, b1, wu_b, wd_b, g2, b2)
    return in_specs, args, pl.BlockSpec((tm, D_MODEL), row), jax.ShapeDtypeStruct((x.shape[0], D_MODEL), F32)


N_FFN_ACTIVATIONS = 3


def _ffn_mixer_tail_body(*refs, n_ffn_in, n_mix_in, n_tokens):
    ffn_in, mix_in = refs[:n_ffn_in], refs[n_ffn_in:n_ffn_in + n_mix_in]
    xs_ref = refs[n_ffn_in + n_mix_in]
    y_ref, oatt_s_ref, yret_s_ref, st_ref, ys_ref = refs[n_ffn_in + n_mix_in + 1:]
    step = pl.program_id(0)

    @pl.when(step < n_tokens)
    def _():
        _sample_mixer_body(*mix_in, oatt_s_ref, yret_s_ref, st_ref)
        _out_ffn_body(*ffn_in, y_ref)

    @pl.when(step == n_tokens)
    def _():
        _out_ffn_body(xs_ref, oatt_s_ref, yret_s_ref, *ffn_in[N_FFN_ACTIVATIONS:], ys_ref)


def _out_ffn_prompt_and_sample(ffn_args, mixer_args, x_sample):
    B = mixer_args[0].shape[0]
    tm = ffn_args[0].shape[0] // B
    f_in, f_args, f_out, f_shape = _ffn_operands(*ffn_args, tm)
    m_in, m_args, m_out, m_shape = _mixer_operands(*mixer_args)
    whole = pl.BlockSpec((B, D_MODEL), lambda i: (0, 0))
    y, oatt, yret, st, y_s = pl.pallas_call(
        functools.partial(_ffn_mixer_tail_body, n_ffn_in=len(f_in), n_mix_in=len(m_in), n_tokens=B),
        grid=(B + 1,), in_specs=f_in + m_in + [whole], out_specs=[f_out] + m_out + [whole],
        out_shape=[f_shape] + m_shape + [jax.ShapeDtypeStruct((B, D_MODEL), F32)],
        compiler_params=_cparams(1), name="out_ffn_prompt_sample_mixer",
    )(*f_args, *m_args, x_sample)
    return y, st, y_s


PROMPT_ROWS = 512


def kernel(x_prompt, x_sample, cache_kv_win, state_ret, w_in, rel_bias, w_out,
           ln1_g, ln1_b, w_up, w_down, ln2_g, ln2_b):
    assert x_prompt.shape[0] == 1 and x_sample.shape[1] == 1 and w_in.shape[0] == DEPTH
    B = x_sample.shape[0]
    g1, b1, g2, b2 = ln1_g[0][None], ln1_b[0][None], ln2_g[0][None], ln2_b[0][None]
    tap_bias = _tap_bias(rel_bias)

    xp = x_prompt[0]
    xs = x_sample[:, 0]
    *qkv_views, kvw_p, yret, st_p, qa_s, kvw_s, qr_s, kr_s, vr_s, g_s, wo_b, wu_b, wd_b = _inproj_prompt(
        xp, xs, w_in[0], (w_out[0], w_up[0], w_down[0]), PROMPT_ROWS)
    oatt = _attn_prompt(*qkv_views, _band_bias_rows(tap_bias))

    cache_t = jnp.transpose(cache_kv_win[0], (0, 2, 3, 4, 1))
    ffn_weights = (wo_b, g1, b1, wu_b, wd_b, g2, b2)
    y_p, st_s, y_s = _out_ffn_prompt_and_sample(
        (xp, oatt, yret) + ffn_weights, (qa_s, kvw_s, qr_s, kr_s, vr_s, g_s, cache_t, state_ret[0], tap_bias), xs)

    win = kvw_p.shape[1]
    kv_win_p = jnp.transpose(kvw_p.reshape(2, H_ATT, HD_ATT, win), (3, 0, 1, 2))
    return (y_p[None], y_s[:, None],
            kv_win_p[None, None], kvw_s.reshape(1, B, 1, 2, H_ATT, HD_ATT),
            st_p[None, None], st_s[None])
```
